```python
import math
import jax, jax.numpy as jnp
from jax import lax
import numpy as np

D_MODEL = 1024
BATCH = 8
SEQ = 4096
DEPTH = 2
DEC_BATCH = 1
DEC_SEQ = 16384
PAST_LEN = 128

MEM_LEN = 256
ROPE_THETA = 10000.0
NORM_EPS = 1e-6
NEG_INF = -1e30
Q_BLOCK = 128
DA_HEADS = 4
DA_DH = 64
WA_HEADS = 8
WA_KV_HEADS = 2
WA_DH = 64
WINDOW = 128
MA_HEADS = 4
MA_DH = 128
N_BRANCH = 3
BRANCH_WIDTH = 512
DA_Q_OFF = 0
DA_K_OFF = DA_Q_OFF + DA_HEADS * 2 * DA_DH
DA_V_OFF = DA_K_OFF + DA_HEADS * 2 * DA_DH
WA_Q_OFF = DA_V_OFF + DA_HEADS * 2 * DA_DH
WA_K_OFF = WA_Q_OFF + WA_HEADS * WA_DH
WA_V_OFF = WA_K_OFF + WA_KV_HEADS * WA_DH
MA_Q_OFF = WA_V_OFF + WA_KV_HEADS * WA_DH
GATE_OFF = MA_Q_OFF + MA_HEADS * MA_DH
IN_COLS = GATE_OFF + N_BRANCH * D_MODEL
PEER_HEADS = 8
N_KEYS = 128
N_EXPERTS = N_KEYS * N_KEYS
PEER_QDIM = 256
PEER_HALF = PEER_QDIM // 2
PEER_TOPK = 16
PEER_BLOCK = 128

kernel_name = 'hybrid_diffattn_window_mem_peer_encoder'


def rms_norm(x, g):
    xf = x.astype(jnp.float32)
    y = xf * lax.rsqrt(jnp.mean(xf * xf, axis=-1, keepdims=True) + NORM_EPS)
    return (y * g.astype(jnp.float32)).astype(x.dtype)


def rope_tables(seq, dim):
    inv = 1.0 / (ROPE_THETA ** (jnp.arange(0, dim, 2, dtype=jnp.float32) / dim))
    ang = jnp.arange(seq, dtype=jnp.float32)[:, None] * inv[None, :]
    return jnp.cos(ang), jnp.sin(ang)


def apply_rope(x, cos, sin):
    shape = (1, x.shape[1]) + (1,) * (x.ndim - 3) + (x.shape[-1] // 2,)
    c = cos.reshape(shape).astype(x.dtype)
    s = sin.reshape(shape).astype(x.dtype)
    x1, x2 = jnp.split(x, 2, axis=-1)
    return jnp.concatenate([x1 * c - x2 * s, x1 * s + x2 * c], axis=-1)


def diff_attention(q1, q2, k1, k2, v, lam):
    B, S, H, dh = q1.shape
    nb = S // Q_BLOCK
    scale = dh ** -0.5

    def block(qs):
        qa, qb = qs
        sa = jnp.einsum('bqhd,bkhd->bhqk', qa, k1).astype(jnp.float32) * scale
        sb = jnp.einsum('bqhd,bkhd->bhqk', qb, k2).astype(jnp.float32) * scale
        p = jax.nn.softmax(sa, axis=-1) - lam * jax.nn.softmax(sb, axis=-1)
        return jnp.einsum('bhqk,bkhe->bqhe', p.astype(v.dtype), v)

    def to_blocks(t):
        return t.reshape(B, nb, Q_BLOCK, H, dh).transpose(1, 0, 2, 3, 4)

    o = lax.map(block, (to_blocks(q1), to_blocks(q2)))
    return o.transpose(1, 0, 2, 3, 4).reshape(B, S, H, v.shape[-1])


def window_attention(q, k, v, sink):
    B, S, H, dh = q.shape
    G = k.shape[2]
    R = H // G
    nb = S // WINDOW
    scale = dh ** -0.5
    qb = q.reshape(B, nb, WINDOW, G, R, dh)

    def band(t):
        tp = jnp.pad(t, ((0, 0), (WINDOW, WINDOW), (0, 0), (0, 0)))
        tp = tp.reshape(B, nb + 2, WINDOW, G, t.shape[-1])
        return jnp.concatenate([tp[:, :-2], tp[:, 1:-1], tp[:, 2:]], axis=2)

    kw, vw = band(k), band(v)
    s = jnp.einsum('bnqgrd,bnkgd->bngrqk', qb, kw).astype(jnp.float32) * scale
    qpos = jnp.arange(nb)[:, None] * WINDOW + jnp.arange(WINDOW)[None, :]
    kpos = jnp.arange(nb)[:, None] * WINDOW - WINDOW + jnp.arange(3 * WINDOW)[None, :]
    valid = ((jnp.abs(qpos[:, :, None] - kpos[:, None, :]) <= WINDOW)
             & (kpos >= 0)[:, None, :] & (kpos < S)[:, None, :])
    s = jnp.where(valid[None, :, None, None], s, NEG_INF)
    sk = sink.astype(jnp.float32).reshape(G, R)[None, None, :, :, None, None]
    m = jnp.maximum(jnp.max(s, axis=-1, keepdims=True), sk)
    e = jnp.exp(s - m)
    p = e / (jnp.sum(e, axis=-1, keepdims=True) + jnp.exp(sk - m))
    o = jnp.einsum('bngrqk,bnkgd->bnqgrd', p.astype(v.dtype), vw)
    return o.reshape(B, S, H, dh)


def memory_attention(q, mk, mv):
    scale = q.shape[-1] ** -0.5
    s = jnp.einsum('bshd,bmhd->bhsm', q, mk).astype(jnp.float32) * scale
    p = jax.nn.softmax(s, axis=-1).astype(mv.dtype)
    return jnp.einsum('bhsm,bmhd->bshd', p, mv)


def peer_ffn(h, w_q, sub_keys, u_tab, v_tab):
    B, S, D = h.shape
    t = h.reshape(-1, D)
    nb = t.shape[0] // PEER_BLOCK

    def block(tb):
        q = (tb @ w_q).reshape(PEER_BLOCK, PEER_HEADS, 2, PEER_HALF)
        s = jnp.einsum('thpc,hpkc->thpk', q, sub_keys).astype(jnp.float32)
        sv, si = lax.top_k(s, PEER_TOPK)
        cand = (sv[:, :, 0, :, None] + sv[:, :, 1, None, :]).reshape(PEER_BLOCK, PEER_HEADS, PEER_TOPK * PEER_TOPK)
        cidx = (si[:, :, 0, :, None] * N_KEYS + si[:, :, 1, None, :]).reshape(PEER_BLOCK, PEER_HEADS, PEER_TOPK * PEER_TOPK)
        fv, fi = lax.top_k(cand, PEER_TOPK)
        eidx = jnp.take_along_axis(cidx, fi, axis=-1)
        g = jax.nn.softmax(fv, axis=-1)
        u = jnp.take(u_tab, eidx, axis=0)
        a = jax.nn.gelu(jnp.einsum('td,thkd->thk', tb, u).astype(jnp.float32), approximate=False)
        w = (g * a).astype(tb.dtype)
        return jnp.einsum('thk,thkd->td', w, jnp.take(v_tab, eidx, axis=0))

    out = lax.map(block, t.reshape(nb, PEER_BLOCK, D))
    return out.reshape(B, S, D)


def encoder_layer(x, mem, cos, sin, lam_init, mix_norm, w_in, da_q_norm, da_k_norm, da_lambda,
                  da_out_norm, wa_q_norm, wa_k_norm, wa_sink, mem_norm, w_mem_kv, ma_q_norm,
                  ma_k_norm, w_branch, w_out, ffn_norm, peer_wq, peer_keys, peer_u, peer_v):
    B, S, D = x.shape
    h = rms_norm(x, mix_norm)
    z = h @ w_in

    dq = z[..., DA_Q_OFF:DA_K_OFF].reshape(B, S, DA_HEADS, 2, DA_DH)
    dk = z[..., DA_K_OFF:DA_V_OFF].reshape(B, S, DA_HEADS, 2, DA_DH)
    dv = z[..., DA_V_OFF:WA_Q_OFF].reshape(B, S, DA_HEADS, 2 * DA_DH)
    dq = apply_rope(rms_norm(dq, da_q_norm), cos, sin)
    dk = apply_rope(rms_norm(dk, da_k_norm), cos, sin)
    lp = da_lambda.astype(jnp.float32)
    lam = jnp.exp(jnp.sum(lp[0] * lp[1])) - jnp.exp(jnp.sum(lp[2] * lp[3])) + lam_init
    o_da = diff_attention(dq[..., 0, :], dq[..., 1, :], dk[..., 0, :], dk[..., 1, :], dv, lam)
    o_da = (rms_norm(o_da, da_out_norm) * (1.0 - lam_init)).reshape(B, S, BRANCH_WIDTH)

    wq = z[..., WA_Q_OFF:WA_K_OFF].reshape(B, S, WA_HEADS, WA_DH)
    wk = z[..., WA_K_OFF:WA_V_OFF].reshape(B, S, WA_KV_HEADS, WA_DH)
    wv = z[..., WA_V_OFF:MA_Q_OFF].reshape(B, S, WA_KV_HEADS, WA_DH)
    wq = apply_rope(rms_norm(wq, wa_q_norm), cos, sin)
    wk = apply_rope(rms_norm(wk, wa_k_norm), cos, sin)
    o_wa = window_attention(wq, wk, wv, wa_sink).reshape(B, S, BRANCH_WIDTH)

    mq = rms_norm(z[..., MA_Q_OFF:GATE_OFF].reshape(B, S, MA_HEADS, MA_DH), ma_q_norm)
    mkv = (rms_norm(mem, mem_norm) @ w_mem_kv).reshape(B, mem.shape[1], 2, MA_HEADS, MA_DH)
    mk = rms_norm(mkv[:, :, 0], ma_k_norm)
    o_ma = memory_attention(mq, mk, mkv[:, :, 1]).reshape(B, S, BRANCH_WIDTH)

    branches = jnp.stack([o_da, o_wa, o_ma], axis=2)
    gates = jax.nn.sigmoid(z[..., GATE_OFF:].reshape(B, S, N_BRANCH, D))
    merged = jnp.sum(gates * jnp.einsum('bsnc,ncd->bsnd', branches, w_branch), axis=2)
    x = x + merged @ w_out

    x = x + peer_ffn(rms_norm(x, ffn_norm), peer_wq, peer_keys, peer_u, peer_v)
    return x


def encoder_trunk(x, mem, weights):
    cos, sin = rope_tables(x.shape[1], DA_DH)
    for l in range(DEPTH):
        lam_init = 0.8 - 0.6 * math.exp(-0.3 * l)
        x = encoder_layer(x, mem, cos, sin, lam_init, *[w[l] for w in weights])
    return x


def setup_inputs(seed: int = 0) -> dict:
    key = jax.random.key(seed)
    ks = jax.random.split(key, 26)
    f32 = jnp.float32

    def nrm(k, shape, scale):
        return jax.random.normal(k, shape, f32) * scale

    def gain(k, shape):
        return 1.0 + 0.02 * jax.random.normal(k, shape, f32)

    D = D_MODEL
    return {
        'x_prompt': nrm(ks[0], (BATCH, SEQ, D), 1.0),
        'x_sample': nrm(ks[1], (DEC_BATCH, DEC_SEQ, D), 1.0),
        'mem_prompt': nrm(ks[2], (BATCH, MEM_LEN, D), 1.0),
        'mem_sample': nrm(ks[3], (DEC_BATCH, MEM_LEN, D), 1.0),
        'mix_norm': gain(ks[4], (DEPTH, D)),
        'w_in': nrm(ks[5], (DEPTH, D, IN_COLS), D ** -0.5),
        'da_q_norm': gain(ks[6], (DEPTH, DA_DH)),
        'da_k_norm': gain(ks[7], (DEPTH, DA_DH)),
        'da_lambda': nrm(ks[8], (DEPTH, 4, DA_DH), 0.1),
        'da_out_norm': gain(ks[9], (DEPTH, 2 * DA_DH)),
        'wa_q_norm': gain(ks[10], (DEPTH, WA_DH)),
        'wa_k_norm': gain(ks[11], (DEPTH, WA_DH)),
        'wa_sink': nrm(ks[12], (DEPTH, WA_HEADS), 0.5),
        'mem_norm': gain(ks[13], (DEPTH, D)),
        'w_mem_kv': nrm(ks[14], (DEPTH, D, 2 * MA_HEADS * MA_DH), D ** -0.5),
        'ma_q_norm': gain(ks[15], (DEPTH, MA_DH)),
        'ma_k_norm': gain(ks[16], (DEPTH, MA_DH)),
        'w_branch': nrm(ks[17], (DEPTH, N_BRANCH, BRANCH_WIDTH, D), BRANCH_WIDTH ** -0.5),
        'w_out': nrm(ks[18], (DEPTH, D, D), D ** -0.5),
        'ffn_norm': gain(ks[19], (DEPTH, D)),
        'peer_wq': nrm(ks[20], (DEPTH, D, PEER_HEADS * PEER_QDIM), D ** -0.5),
        'peer_keys': nrm(ks[21], (DEPTH, PEER_HEADS, 2, N_KEYS, PEER_HALF), PEER_HALF ** -0.5),
        'peer_u': nrm(ks[22], (DEPTH, N_EXPERTS, D), D ** -0.5),
        'peer_v': nrm(ks[23], (DEPTH, N_EXPERTS, D), PEER_HEADS ** -0.5),
    }


def reference(x_prompt, x_sample, mem_prompt, mem_sample, mix_norm, w_in, da_q_norm, da_k_norm,
              da_lambda, da_out_norm, wa_q_norm, wa_k_norm, wa_sink, mem_norm, w_mem_kv, ma_q_norm,
              ma_k_norm, w_branch, w_out, ffn_norm, peer_wq, peer_keys, peer_u, peer_v):
    weights = (mix_norm, w_in, da_q_norm, da_k_norm, da_lambda, da_out_norm, wa_q_norm, wa_k_norm,
               wa_sink, mem_norm, w_mem_kv, ma_q_norm, ma_k_norm, w_branch, w_out, ffn_norm,
               peer_wq, peer_keys, peer_u, peer_v)
    y_prompt = encoder_trunk(x_prompt, mem_prompt, weights)
    y_sample = encoder_trunk(x_sample, mem_sample, weights)
    return (y_prompt, y_sample)
```

```python
import functools
import math

import jax
import jax.numpy as jnp
import numpy as np
from jax import lax
from jax.experimental import pallas as pl
from jax.experimental.pallas import tpu as pltpu

F32 = jnp.float32
BF16 = jnp.bfloat16

D_MODEL = 1024
NORM_EPS = 1e-6
NEG_INF = -1e30
ROPE_THETA = 10000.0
LANES = 128

DA_HEADS, DA_DH = 4, 64
WA_HEADS, WA_KV_HEADS, WA_DH, WINDOW = 8, 2, 64, 128
MA_HEADS, MA_DH = 4, 128
N_BRANCH, BRANCH_WIDTH = 3, 512
PEER_HEADS, N_KEYS, PEER_HALF, PEER_TOPK = 8, 128, 128, 16
N_EXPERTS = N_KEYS * N_KEYS

DA_Q_OFF = 0
DA_K_OFF = 512
DA_V_OFF = 1024
WA_Q_OFF = 1536
WA_K_OFF = 2048
WA_V_OFF = 2176
MA_Q_OFF = 2304
GATE_OFF = 2816
IN_CHUNK = 1024
N_IN_CHUNKS = 6

VMEM_LIMIT = 48 * 1024 * 1024


def _tile(n, pref):
    t = min(pref, n)
    t -= t % LANES
    while n % t:
        t -= LANES
    return t


def _params(sem):
    return pltpu.CompilerParams(dimension_semantics=sem, vmem_limit_bytes=VMEM_LIMIT)


def _lane_iota(shape):
    return lax.broadcasted_iota(jnp.int32, shape, len(shape) - 1)


def _rms(xf, gain_row, width):
    ms = jnp.sum(xf * xf, axis=-1, keepdims=True) * (1.0 / width)
    return xf * lax.rsqrt(ms + NORM_EPS) * gain_row


def _seg64_norm_rope(zt, gain_row, cosb, sinb):
    lane = _lane_iota(zt.shape)
    lo = lane < 64
    sq = zt * zt
    s0 = jnp.sum(jnp.where(lo, sq, 0.0), axis=-1, keepdims=True)
    s1 = jnp.sum(jnp.where(lo, 0.0, sq), axis=-1, keepdims=True)
    ms = jnp.where(lo, s0, s1) * (1.0 / 64)
    y = zt * lax.rsqrt(ms + NORM_EPS) * gain_row
    first = (lane % 64) < 32
    rot = jnp.where(first, pltpu.roll(y, 96, 1), pltpu.roll(y, 32, 1))
    return y * cosb + rot * sinb


def _in_proj_kernel(x_ref, mixg_ref, w_ref, cos_ref, sin_ref, hg_ref,
                    dq_ref, dk_ref, dv_ref, wq_ref, wkd_ref, wvd_ref, mq_ref, gate_ref, xn_sc):
    j = pl.program_id(1)

    @pl.when(j == 0)
    def _():
        xn_sc[...] = _rms(x_ref[...], mixg_ref[...], D_MODEL).astype(BF16)

    z = jnp.dot(xn_sc[...], w_ref[...], preferred_element_type=F32)
    cosb = cos_ref[...]
    sinb = sin_ref[...]

    @pl.when(j == 0)
    def _():
        for t in range(4):
            sl = slice(t * LANES, (t + 1) * LANES)
            dq_ref[:, sl] = _seg64_norm_rope(z[:, sl], hg_ref[0:1, :], cosb, sinb).astype(BF16)
            zs = z[:, 512 + t * LANES:512 + (t + 1) * LANES]
            dk_ref[:, sl] = _seg64_norm_rope(zs, hg_ref[1:2, :], cosb, sinb).astype(BF16)

    @pl.when(j == 1)
    def _():
        for t in range(4):
            sl = slice(t * LANES, (t + 1) * LANES)
            wq_ref[:, sl] = _seg64_norm_rope(z[:, sl], hg_ref[2:3, :], cosb, sinb).astype(BF16)
        k = _seg64_norm_rope(z[:, 512:640], hg_ref[3:4, :], cosb, sinb)
        v = z[:, 640:768]
        lo = _lane_iota(k.shape) < 64
        for src, dst in ((k, wkd_ref), (v, wvd_ref)):
            sw = pltpu.roll(src, 64, 1)
            dst[:, 0:LANES] = jnp.where(lo, src, sw).astype(BF16)
            dst[:, LANES:2 * LANES] = jnp.where(lo, sw, src).astype(BF16)

    @pl.when(j == 2)
    def _():
        for t in range(4):
            sl = slice(t * LANES, (t + 1) * LANES)
            mq_ref[:, sl] = _rms(z[:, sl], hg_ref[4:5, :], MA_DH).astype(BF16)
        dv_ref[...] = z[:, 512:1024].astype(BF16)

    @pl.when(j >= 3)
    def _():
        gate_ref[...] = jax.nn.sigmoid(z)


def _in_proj(x2d, seq, mixg, w_r, cos_t, sin_t, head_gains):
    n = x2d.shape[0]
    tm = _tile(seq, 1024)
    nsb = seq // tm
    bf = lambda w: jax.ShapeDtypeStruct((n, w), BF16)
    row = lambda w: pl.BlockSpec((tm, w), lambda i, j: (i, 0))
    return pl.pallas_call(
        _in_proj_kernel,
        grid=(n // tm, N_IN_CHUNKS),
        in_specs=[
            pl.BlockSpec((tm, D_MODEL), lambda i, j: (i, 0)),
            pl.BlockSpec((1, D_MODEL), lambda i, j: (0, 0)),
            pl.BlockSpec((D_MODEL, IN_CHUNK), lambda i, j: (0, j)),
            pl.BlockSpec((tm, LANES), lambda i, j: (i % nsb, 0)),
            pl.BlockSpec((tm, LANES), lambda i, j: (i % nsb, 0)),
            pl.BlockSpec((8, LANES), lambda i, j: (0, 0)),
        ],
        out_specs=[row(512), row(512), row(512), row(512), row(256), row(256), row(512),
                   pl.BlockSpec((tm, IN_CHUNK), lambda i, j: (i, jnp.maximum(j - 3, 0)))],
        out_shape=[bf(512), bf(512), bf(512), bf(512), bf(256), bf(256), bf(512),
                   jax.ShapeDtypeStruct((n, N_BRANCH * D_MODEL), F32)],
        scratch_shapes=[pltpu.VMEM((tm, D_MODEL), BF16)],
        compiler_params=_params(("parallel", "arbitrary")),
        name="in_proj",
    )(x2d, mixg, w_r, cos_t, sin_t, head_gains)


def _diff_attn_kernel(q_ref, k_ref, v_ref, lam_ref, og_ref, o_ref, m_sc, l_sc, acc_sc, *, tk, lam_init):
    q = q_ref[...]
    lo = _lane_iota(q.shape) < 64
    zero = jnp.zeros_like(q)
    qs = (jnp.where(lo, q, zero), jnp.where(lo, zero, q))
    m_sc[...] = jnp.full(m_sc.shape, -jnp.inf, F32)
    l_sc[...] = jnp.zeros(l_sc.shape, F32)
    acc_sc[...] = jnp.zeros(acc_sc.shape, F32)
    nt = (((1,), (1,)), ((), ()))

    def body(j, carry):
        off = pl.multiple_of(j * tk, tk)
        k = k_ref[pl.ds(off, tk), :]
        v = v_ref[pl.ds(off, tk), :]
        for c in range(2):
            s = lax.dot_general(qs[c], k, nt, preferred_element_type=F32)
            m_old = m_sc[c]
            m_new = jnp.maximum(m_old, jnp.max(s, axis=-1, keepdims=True))
            p = jnp.exp(s - m_new)
            alpha = jnp.exp(m_old - m_new)
            l_sc[c] = alpha * l_sc[c] + jnp.sum(p, axis=-1, keepdims=True)
            acc_sc[c] = alpha * acc_sc[c] + jnp.dot(p.astype(BF16), v, preferred_element_type=F32)
            m_sc[c] = m_new
        return carry

    lax.fori_loop(0, k_ref.shape[0] // tk, body, 0)

    lp = lam_ref[...]
    lam = (jnp.exp(jnp.sum(lp[0:1] * lp[1:2], axis=-1, keepdims=True))
           - jnp.exp(jnp.sum(lp[2:3] * lp[3:4], axis=-1, keepdims=True)) + lam_init)
    o = acc_sc[0] / l_sc[0] - lam * (acc_sc[1] / l_sc[1])
    o_ref[...] = (_rms(o, og_ref[...], 2 * DA_DH) * (1.0 - lam_init)).astype(BF16)


def _diff_attn(dq, dk, dv, da_lambda, og, nb, seq, lam_init):
    n = dq.shape[0]
    tq = _tile(seq, 256)
    tk = _tile(seq, 512)
    nq = seq // tq
    kern = functools.partial(_diff_attn_kernel, tk=tk, lam_init=lam_init)
    return pl.pallas_call(
        kern,
        grid=(nb, DA_HEADS, nq),
        in_specs=[
            pl.BlockSpec((tq, LANES), lambda b, h, i: (b * nq + i, h)),
            pl.BlockSpec((seq, LANES), lambda b, h, i: (b, h)),
            pl.BlockSpec((seq, LANES), lambda b, h, i: (b, h)),
            pl.BlockSpec((4, DA_DH), lambda b, h, i: (0, 0)),
            pl.BlockSpec((1, LANES), lambda b, h, i: (0, 0)),
        ],
        out_specs=pl.BlockSpec((tq, LANES), lambda b, h, i: (b * nq + i, h)),
        out_shape=jax.ShapeDtypeStruct((n, BRANCH_WIDTH), BF16),
        scratch_shapes=[pltpu.VMEM((2, tq, 1), F32), pltpu.VMEM((2, tq, 1), F32),
                        pltpu.VMEM((2, tq, LANES), F32)],
        compiler_params=_params(("parallel", "parallel", "arbitrary")),
        name="diff_attn",
    )(dq, dk, dv, da_lambda, og)


def _window_attn_kernel(q_ref, kd_ref, vd_ref, sink_ref, o_ref, *, tq, seq):
    i = pl.program_id(1)
    nsub = tq // WINDOW
    kw_len = 3 * WINDOW
    nt = (((1,), (1,)), ((), ()))
    for n in range(nsub):
        gb = i * nsub + n
        start = pl.multiple_of(jnp.clip((gb - 1) * WINDOW, 0, seq - kw_len), WINDOW)
        kw = kd_ref[pl.ds(start, kw_len), :]
        vw = vd_ref[pl.ds(start, kw_len), :]
        qpos = gb * WINDOW + lax.broadcasted_iota(jnp.int32, (WINDOW, kw_len), 0)
        kpos = start + lax.broadcasted_iota(jnp.int32, (WINDOW, kw_len), 1)
        valid = jnp.abs(qpos - kpos) <= WINDOW
        for pair in range(WA_HEADS // 2):
            g = (2 * pair) // (WA_HEADS // WA_KV_HEADS)
            qp = q_ref[n * WINDOW:(n + 1) * WINDOW, pair * LANES:(pair + 1) * LANES]
            lo = _lane_iota(qp.shape) < 64
            zero = jnp.zeros_like(qp)
            kg = kw[:, g * LANES:(g + 1) * LANES]
            vg = vw[:, g * LANES:(g + 1) * LANES]
            res = []
            for e in range(2):
                h = 2 * pair + e
                qh = jnp.where(lo, qp, zero) if e == 0 else jnp.where(lo, zero, qp)
                s = lax.dot_general(qh, kg, nt, preferred_element_type=F32)
                s = jnp.where(valid, s, NEG_INF)
                sk = sink_ref[h:h + 1, 0:1]
                m = jnp.maximum(jnp.max(s, axis=-1, keepdims=True), sk)
                ex = jnp.exp(s - m)
                p = ex / (jnp.sum(ex, axis=-1, keepdims=True) + jnp.exp(sk - m))
                res.append(jnp.dot(p.astype(BF16), vg, preferred_element_type=F32))
            o_ref[n * WINDOW:(n + 1) * WINDOW, pair * LANES:(pair + 1) * LANES] = (
                jnp.where(lo, res[0], res[1]).astype(BF16))


def _window_attn(wq, wkd, wvd, sink_b, nb, seq):
    n = wq.shape[0]
    tq = _tile(seq, 512)
    nq = seq // tq
    kern = functools.partial(_window_attn_kernel, tq=tq, seq=seq)
    return pl.pallas_call(
        kern,
        grid=(nb, nq),
        in_specs=[
            pl.BlockSpec((tq, 512), lambda b, i: (b * nq + i, 0)),
            pl.BlockSpec((seq, 256), lambda b, i: (b, 0)),
            pl.BlockSpec((seq, 256), lambda b, i: (b, 0)),
            pl.BlockSpec((8, LANES), lambda b, i: (0, 0)),
        ],
        out_specs=pl.BlockSpec((tq, 512), lambda b, i: (b * nq + i, 0)),
        out_shape=jax.ShapeDtypeStruct((n, BRANCH_WIDTH), BF16),
        compiler_params=_params(("parallel", "arbitrary")),
        name="window_attn",
    )(wq, wkd, wvd, sink_b)


def _mem_kv_kernel(mem_ref, mg_ref, w_ref, kg_ref, mk_ref, mv_ref):
    h = _rms(mem_ref[...], mg_ref[...], D_MODEL).astype(BF16)
    kv = jnp.dot(h, w_ref[...], preferred_element_type=F32)
    for t in range(MA_HEADS):
        sl = slice(t * LANES, (t + 1) * LANES)
        mk_ref[:, sl] = _rms(kv[:, sl], kg_ref[...], MA_DH).astype(BF16)
    mv_ref[...] = kv[:, 512:1024].astype(BF16)


def _mem_kv(mem2d, mlen, mg, w_kv, kg):
    n = mem2d.shape[0]
    return pl.pallas_call(
        _mem_kv_kernel,
        grid=(n // mlen,),
        in_specs=[
            pl.BlockSpec((mlen, D_MODEL), lambda b: (b, 0)),
            pl.BlockSpec((1, D_MODEL), lambda b: (0, 0)),
            pl.BlockSpec((D_MODEL, 2 * MA_HEADS * MA_DH), lambda b: (0, 0)),
            pl.BlockSpec((1, LANES), lambda b: (0, 0)),
        ],
        out_specs=[pl.BlockSpec((mlen, 512), lambda b: (b, 0)), pl.BlockSpec((mlen, 512), lambda b: (b, 0))],
        out_shape=[jax.ShapeDtypeStruct((n, 512), BF16), jax.ShapeDtypeStruct((n, 512), BF16)],
        compiler_params=_params(("parallel",)),
        name="mem_kv",
    )(mem2d, mg, w_kv, kg)


def _mem_attn_kernel(q_ref, mk_ref, mv_ref, o_ref):
    nt = (((1,), (1,)), ((), ()))
    for t in range(MA_HEADS):
        sl = slice(t * LANES, (t + 1) * LANES)
        s = lax.dot_general(q_ref[:, sl], mk_ref[:, sl], nt, preferred_element_type=F32)
        m = jnp.max(s, axis=-1, keepdims=True)
        ex = jnp.exp(s - m)
        p = ex / jnp.sum(ex, axis=-1, keepdims=True)
        o_ref[:, sl] = jnp.dot(p.astype(BF16), mv_ref[:, sl], preferred_element_type=F32).astype(BF16)


def _mem_attn(mq, mk, mv, nb, seq, mlen):
    n = mq.shape[0]
    tq = _tile(seq, 1024)
    nq = seq // tq
    return pl.pallas_call(
        _mem_attn_kernel,
        grid=(nb, nq),
        in_specs=[
            pl.BlockSpec((tq, 512), lambda b, i: (b * nq + i, 0)),
            pl.BlockSpec((mlen, 512), lambda b, i: (b, 0)),
            pl.BlockSpec((mlen, 512), lambda b, i: (b, 0)),
        ],
        out_specs=pl.BlockSpec((tq, 512), lambda b, i: (b * nq + i, 0)),
        out_shape=jax.ShapeDtypeStruct((n, BRANCH_WIDTH), BF16),
        compiler_params=_params(("parallel", "arbitrary")),
        name="mem_attn",
    )(mq, mk, mv)


def _merge_kernel(x_ref, oda_ref, owa_ref, oma_ref, gate_ref, wb_ref, wo_ref, o_ref):
    merged = None
    for nbr, br in enumerate((oda_ref, owa_ref, oma_ref)):
        proj = jnp.dot(br[...], wb_ref[nbr], preferred_element_type=F32)
        term = gate_ref[:, nbr * D_MODEL:(nbr + 1) * D_MODEL] * proj
        merged = term if merged is None else merged + term
    o_ref[...] = x_ref[...] + jnp.dot(merged.astype(BF16), wo_ref[...], preferred_element_type=F32)


def _merge(x2d, oda, owa, oma, gates, wb, wo):
    n = x2d.shape[0]
    tm = _tile(n, 512)
    row = lambda w: pl.BlockSpec((tm, w), lambda i: (i, 0))
    return pl.pallas_call(
        _merge_kernel,
        grid=(n // tm,),
        in_specs=[row(D_MODEL), row(512), row(512), row(512), row(N_BRANCH * D_MODEL),
                  pl.BlockSpec((N_BRANCH, BRANCH_WIDTH, D_MODEL), lambda i: (0, 0, 0)),
                  pl.BlockSpec((D_MODEL, D_MODEL), lambda i: (0, 0))],
        out_specs=row(D_MODEL),
        out_shape=jax.ShapeDtypeStruct((n, D_MODEL), F32),
        compiler_params=_params(("parallel",)),
        name="merge",
    )(x2d, oda, owa, oma, gates, wb, wo)


def _top16_rows(s, row_ids, n_rows):
    out_iota = lax.broadcasted_iota(jnp.int32, (PEER_TOPK, s.shape[1]), 0)
    vals = jnp.zeros((PEER_TOPK, s.shape[1]), F32)
    idxs = jnp.zeros((PEER_TOPK, s.shape[1]), jnp.int32)
    for r in range(PEER_TOPK):
        m = jnp.max(s, axis=0, keepdims=True)
        sel = jnp.min(jnp.where(s == m, row_ids, n_rows), axis=0, keepdims=True)
        s = jnp.where(row_ids == sel, -jnp.inf, s)
        vals = jnp.where(out_iota == r, m, vals)
        idxs = jnp.where(out_iota == r, sel, idxs)
    return vals, idxs


def _peer_route_kernel(x_ref, fg_ref, wq_ref, keys_ref, xn_ref, wt_ref,
                       q_sc, g_sc, ei_sc, ej_sc, gt_sc, eit_sc, ejt_sc, w_sc, *, tm):
    xn = _rms(x_ref[...], fg_ref[...], D_MODEL).astype(BF16)
    xn_ref[...] = xn
    q = jnp.dot(xn, wq_ref[...], preferred_element_type=F32).astype(BF16)
    for hp in range(2 * PEER_HEADS):
        q_sc[hp] = q[:, hp * LANES:(hp + 1) * LANES]

    nt = (((1,), (1,)), ((), ()))
    key_ids = lax.broadcasted_iota(jnp.int32, (N_KEYS, tm), 0)
    cand_ids = lax.broadcasted_iota(jnp.int32, (PEER_TOPK * PEER_TOPK, tm), 0)

    def head_body(h, carry):
        sv, si = [], []
        for p in range(2):
            s = lax.dot_general(keys_ref[2 * h + p], q_sc[2 * h + p], nt,
                                preferred_element_type=F32)
            v, ix = _top16_rows(s, key_ids, N_KEYS)
            sv.append(v)
            si.append(ix)
        cand = jnp.concatenate([sv[0][a:a + 1, :] + sv[1] for a in range(PEER_TOPK)], axis=0)
        fv, fi = _top16_rows(cand, cand_ids, PEER_TOPK * PEER_TOPK)
        fa = fi // PEER_TOPK
        fb = fi % PEER_TOPK
        ei = jnp.zeros_like(fi)
        ej = jnp.zeros_like(fi)
        for a in range(PEER_TOPK):
            ei = jnp.where(fa == a, si[0][a:a + 1, :], ei)
            ej = jnp.where(fb == a, si[1][a:a + 1, :], ej)
        ex = jnp.exp(fv - jnp.max(fv, axis=0, keepdims=True))
        g = ex / jnp.sum(ex, axis=0, keepdims=True)
        off = pl.multiple_of(h * PEER_TOPK, PEER_TOPK)
        g_sc[pl.ds(off, PEER_TOPK), :] = g
        ei_sc[pl.ds(off, PEER_TOPK), :] = ei.astype(F32)
        ej_sc[pl.ds(off, PEER_TOPK), :] = ej.astype(F32)
        return carry

    lax.fori_loop(0, PEER_HEADS, head_body, 0)

    for c in range(tm // LANES):
        sl = slice(c * LANES, (c + 1) * LANES)
        gt_sc[sl, :] = g_sc[:, sl].T
        eit_sc[sl, :] = ei_sc[:, sl].T
        ejt_sc[sl, :] = ej_sc[:, sl].T

    sub = lax.broadcasted_iota(jnp.int32, (N_KEYS, LANES), 0).astype(F32)

    def tok_body(t, carry):
        g_r = gt_sc[pl.ds(t, 1), :]
        at = jnp.where(sub == eit_sc[pl.ds(t, 1), :], g_r, 0.0).astype(BF16)
        bt = jnp.where(sub == ejt_sc[pl.ds(t, 1), :], 1.0, 0.0).astype(BF16)
        w_t = lax.dot_general(at, bt, nt, preferred_element_type=F32)
        w_sc[pl.ds(t, N_KEYS, stride=tm), :] = w_t
        return carry

    lax.fori_loop(0, tm, tok_body, 0)

    for i in range(N_KEYS):
        wt_ref[i] = w_sc[i * tm:(i + 1) * tm, :].astype(BF16)


def _peer_route(x2d, fg, wq, keys):
    n = x2d.shape[0]
    tm = _tile(n, 128)
    kern = functools.partial(_peer_route_kernel, tm=tm)
    return pl.pallas_call(
        kern,
        grid=(n // tm,),
        in_specs=[
            pl.BlockSpec((tm, D_MODEL), lambda i: (i, 0)),
            pl.BlockSpec((1, D_MODEL), lambda i: (0, 0)),
            pl.BlockSpec((D_MODEL, 2 * PEER_HEADS * PEER_HALF), lambda i: (0, 0)),
            pl.BlockSpec((2 * PEER_HEADS, N_KEYS, PEER_HALF), lambda i: (0, 0, 0)),
        ],
        out_specs=[pl.BlockSpec((tm, D_MODEL), lambda i: (i, 0)),
                   pl.BlockSpec((N_KEYS, tm, N_KEYS), lambda i: (0, i, 0))],
        out_shape=[jax.ShapeDtypeStruct((n, D_MODEL), BF16),
                   jax.ShapeDtypeStruct((N_KEYS, n, N_KEYS), BF16)],
        scratch_shapes=[
            pltpu.VMEM((2 * PEER_HEADS, tm, PEER_HALF), BF16),
            pltpu.VMEM((PEER_HEADS * PEER_TOPK, tm), F32),
            pltpu.VMEM((PEER_HEADS * PEER_TOPK, tm), F32),
            pltpu.VMEM((PEER_HEADS * PEER_TOPK, tm), F32),
            pltpu.VMEM((tm, PEER_HEADS * PEER_TOPK), F32),
            pltpu.VMEM((tm, PEER_HEADS * PEER_TOPK), F32),
            pltpu.VMEM((tm, PEER_HEADS * PEER_TOPK), F32),
            pltpu.VMEM((N_KEYS * tm, N_KEYS), F32),
        ],
        compiler_params=_params(("parallel",)),
        name="peer_route",
    )(x2d, fg, wq, keys)


def _peer_dense_kernel(xn_ref, ut_ref, wt_ref, v_ref, x_ref, o_ref, acc_sc, *, ce):
    c = pl.program_id(1)

    @pl.when(c == 0)
    def _():
        acc_sc[...] = jnp.zeros(acc_sc.shape, F32)

    xn = xn_ref[...]
    sub = 2 * LANES
    for s in range(ce // sub):
        a = jnp.dot(xn, ut_ref[:, s * sub:(s + 1) * sub], preferred_element_type=F32)
        act = 0.5 * a * (1.0 + lax.erf(a * math.sqrt(0.5)))
        h = jnp.concatenate([wt_ref[2 * s].astype(F32) * act[:, :LANES],
                             wt_ref[2 * s + 1].astype(F32) * act[:, LANES:]], axis=1)
        acc_sc[...] += jnp.dot(h.astype(BF16), v_ref[s * sub:(s + 1) * sub, :], preferred_element_type=F32)

    @pl.when(c == pl.num_programs(1) - 1)
    def _():
        o_ref[...] = x_ref[...] + acc_sc[...]


def _peer_dense(xn, ut, wt, v, x2d):
    n = x2d.shape[0]
    tm = _tile(n, 512)
    ce = 1024
    kern = functools.partial(_peer_dense_kernel, ce=ce)
    return pl.pallas_call(
        kern,
        grid=(n // tm, N_EXPERTS // ce),
        in_specs=[
            pl.BlockSpec((tm, D_MODEL), lambda i, c: (i, 0)),
            pl.BlockSpec((D_MODEL, ce), lambda i, c: (0, c)),
            pl.BlockSpec((ce // N_KEYS, tm, N_KEYS), lambda i, c: (c, i, 0)),
            pl.BlockSpec((ce, D_MODEL), lambda i, c: (c, 0)),
            pl.BlockSpec((tm, D_MODEL), lambda i, c: (i, 0)),
        ],
        out_specs=pl.BlockSpec((tm, D_MODEL), lambda i, c: (i, 0)),
        out_shape=jax.ShapeDtypeStruct((n, D_MODEL), F32),
        scratch_shapes=[pltpu.VMEM((tm, D_MODEL), F32)],
        compiler_params=_params(("parallel", "arbitrary")),
        name="peer_dense",
    )(xn, ut, wt, v, x2d)


def _rope_tiles(seq):
    inv = 1.0 / (ROPE_THETA ** (jnp.arange(0, DA_DH, 2, dtype=F32) / DA_DH))
    ang = jnp.arange(seq, dtype=F32)[:, None] * inv[None, :]
    cos, sin = jnp.cos(ang), jnp.sin(ang)
    cos_t = jnp.tile(cos, (1, 4))
    sign = jnp.tile(jnp.concatenate([-jnp.ones((32,), F32), jnp.ones((32,), F32)]), 2)
    sin_t = jnp.tile(sin, (1, 4)) * sign[None, :]
    return cos_t, sin_t


def _prep_layer(l, mix_norm, w_in, da_q_norm, da_k_norm, da_lambda, da_out_norm, wa_q_norm, wa_k_norm,
                wa_sink, mem_norm, w_mem_kv, ma_q_norm, ma_k_norm, w_branch, w_out, ffn_norm,
                peer_wq, peer_keys, peer_u, peer_v):
    w = w_in[l]
    pad = jnp.zeros((D_MODEL, 256), F32)
    w_r = jnp.concatenate([
        w[:, DA_Q_OFF:DA_V_OFF],
        w[:, WA_Q_OFF:MA_Q_OFF], pad,
        w[:, MA_Q_OFF:GATE_OFF], w[:, DA_V_OFF:WA_Q_OFF],
        w[:, GATE_OFF:]], axis=1).astype(BF16)
    t2 = lambda g: jnp.tile(g.astype(F32), 2)
    head_gains = jnp.stack([
        t2(da_q_norm[l]) * DA_DH ** -0.5, t2(da_k_norm[l]),
        t2(wa_q_norm[l]) * WA_DH ** -0.5, t2(wa_k_norm[l]),
        ma_q_norm[l].astype(F32) * MA_DH ** -0.5,
        jnp.zeros((LANES,), F32), jnp.zeros((LANES,), F32), jnp.zeros((LANES,), F32)])
    return dict(
        mixg=mix_norm[l][None, :], w_r=w_r, head_gains=head_gains,
        da_lambda=da_lambda[l], og=da_out_norm[l][None, :],
        sink=jnp.broadcast_to(wa_sink[l].astype(F32)[:, None], (WA_HEADS, LANES)),
        mem_g=mem_norm[l][None, :], w_kv=w_mem_kv[l].astype(BF16), ma_kg=ma_k_norm[l][None, :],
        wb=w_branch[l].astype(BF16), wo=w_out[l].astype(BF16),
        ffn_g=ffn_norm[l][None, :], peer_wq=peer_wq[l].astype(BF16),
        keys=peer_keys[l].reshape(2 * PEER_HEADS, N_KEYS, PEER_HALF).astype(BF16),
        ut=peer_u[l].astype(BF16).T, v=peer_v[l].astype(BF16),
    )


def _trunk(x, mem, layers):
    nb, seq, _ = x.shape
    mlen = mem.shape[1]
    x2d = x.reshape(nb * seq, D_MODEL)
    mem2d = mem.reshape(nb * mlen, D_MODEL)
    cos_t, sin_t = _rope_tiles(seq)
    for l, p in enumerate(layers):
        lam_init = 0.8 - 0.6 * math.exp(-0.3 * l)
        dq, dk, dv, wq, wkd, wvd, mq, gates = _in_proj(x2d, seq, p["mixg"], p["w_r"], cos_t, sin_t,
                                                      p["head_gains"])
        oda = _diff_attn(dq, dk, dv, p["da_lambda"], p["og"], nb, seq, lam_init)
        owa = _window_attn(wq, wkd, wvd, p["sink"], nb, seq)
        mk, mv = _mem_kv(mem2d, mlen, p["mem_g"], p["w_kv"], p["ma_kg"])
        oma = _mem_attn(mq, mk, mv, nb, seq, mlen)
        x1 = _merge(x2d, oda, owa, oma, gates, p["wb"], p["wo"])
        xn, wt = _peer_route(x1, p["ffn_g"], p["peer_wq"], p["keys"])
        x2d = _peer_dense(xn, p["ut"], wt, p["v"], x1)
    return x2d.reshape(nb, seq, D_MODEL)


def kernel(x_prompt, x_sample, mem_prompt, mem_sample, mix_norm, w_in, da_q_norm, da_k_norm, da_lambda, da_out_norm, wa_q_norm, wa_k_norm, wa_sink, mem_norm, w_mem_kv, ma_q_norm, ma_k_norm, w_branch, w_out, ffn_norm, peer_wq, peer_keys, peer_u, peer_v):
    weights = (mix_norm, w_in, da_q_norm, da_k_norm, da_lambda, da_out_norm, wa_q_norm, wa_k_norm,
               wa_sink, mem_norm, w_mem_kv, ma_q_norm, ma_k_norm, w_branch, w_out, ffn_norm,
               peer_wq, peer_keys, peer_u, peer_v)
    layers = [_prep_layer(l, *weights) for l in range(w_in.shape[0])]
    y_prompt = _trunk(x_prompt, mem_prompt, layers)
    y_sample = _trunk(x_sample, mem_sample, layers)
    return (y_prompt, y_sample)
```

```python
import functools
import math

import jax
import jax.numpy as jnp
import numpy as np
from jax import lax
from jax.experimental import pallas as pl
from jax.experimental.pallas import tpu as pltpu

F32 = jnp.float32
BF16 = jnp.bfloat16

D_MODEL = 1024
NORM_EPS = 1e-6
NEG_INF = -1e30
ROPE_THETA = 10000.0
LANES = 128

DA_HEADS, DA_DH = 4, 64
WA_HEADS, WA_KV_HEADS, WA_DH, WINDOW = 8, 2, 64, 128
MA_HEADS, MA_DH = 4, 128
N_BRANCH, BRANCH_WIDTH = 3, 512
PEER_HEADS, N_KEYS, PEER_HALF, PEER_TOPK = 8, 128, 128, 16
N_EXPERTS = N_KEYS * N_KEYS

DA_Q_OFF = 0
DA_K_OFF = 512
DA_V_OFF = 1024
WA_Q_OFF = 1536
WA_K_OFF = 2048
WA_V_OFF = 2176
MA_Q_OFF = 2304
GATE_OFF = 2816
IN_CHUNK = 1024
N_IN_CHUNKS = 6

VMEM_LIMIT = 48 * 1024 * 1024
DIFF_TQ, DIFF_TK = 256, 1024


def _tile(n, pref):
    t = min(pref, n)
    t -= t % LANES
    while n % t:
        t -= LANES
    return t


def _params(sem):
    return pltpu.CompilerParams(dimension_semantics=sem, vmem_limit_bytes=VMEM_LIMIT)


def _lane_iota(shape):
    return lax.broadcasted_iota(jnp.int32, shape, len(shape) - 1)


def _rms(xf, gain_row, width):
    ms = jnp.sum(xf * xf, axis=-1, keepdims=True) * (1.0 / width)
    return xf * lax.rsqrt(ms + NORM_EPS) * gain_row


def _seg64_norm_rope(zt, gain_row, cosb, sinb):
    lane = _lane_iota(zt.shape)
    lo = lane < 64
    sq = zt * zt
    s0 = jnp.sum(jnp.where(lo, sq, 0.0), axis=-1, keepdims=True)
    s1 = jnp.sum(jnp.where(lo, 0.0, sq), axis=-1, keepdims=True)
    ms = jnp.where(lo, s0, s1) * (1.0 / 64)
    y = zt * lax.rsqrt(ms + NORM_EPS) * gain_row
    first = (lane % 64) < 32
    rot = jnp.where(first, pltpu.roll(y, 96, 1), pltpu.roll(y, 32, 1))
    return y * cosb + rot * sinb


def _in_proj_kernel(x_ref, mixg_ref, w_ref, cos_ref, sin_ref, hg_ref,
                    dq_ref, dk_ref, dv_ref, wq_ref, wkd_ref, wvd_ref, mq_ref, gate_ref, xn_sc):
    j = pl.program_id(1)

    @pl.when(j == 0)
    def _():
        xn_sc[...] = _rms(x_ref[...], mixg_ref[...], D_MODEL).astype(BF16)

    z = jnp.dot(xn_sc[...], w_ref[...], preferred_element_type=F32)
    cosb = cos_ref[...]
    sinb = sin_ref[...]

    @pl.when(j == 0)
    def _():
        for t in range(4):
            sl = slice(t * LANES, (t + 1) * LANES)
            dq_ref[:, sl] = _seg64_norm_rope(z[:, sl], hg_ref[0:1, :], cosb, sinb).astype(BF16)
            zs = z[:, 512 + t * LANES:512 + (t + 1) * LANES]
            dk_ref[:, sl] = _seg64_norm_rope(zs, hg_ref[1:2, :], cosb, sinb).astype(BF16)

    @pl.when(j == 1)
    def _():
        for t in range(4):
            sl = slice(t * LANES, (t + 1) * LANES)
            wq_ref[:, sl] = _seg64_norm_rope(z[:, sl], hg_ref[2:3, :], cosb, sinb).astype(BF16)
        k = _seg64_norm_rope(z[:, 512:640], hg_ref[3:4, :], cosb, sinb)
        v = z[:, 640:768]
        lo = _lane_iota(k.shape) < 64
        for src, dst in ((k, wkd_ref), (v, wvd_ref)):
            sw = pltpu.roll(src, 64, 1)
            dst[:, 0:LANES] = jnp.where(lo, src, sw).astype(BF16)
            dst[:, LANES:2 * LANES] = jnp.where(lo, sw, src).astype(BF16)

    @pl.when(j == 2)
    def _():
        for t in range(4):
            sl = slice(t * LANES, (t + 1) * LANES)
            mq_ref[:, sl] = _rms(z[:, sl], hg_ref[4:5, :], MA_DH).astype(BF16)
        ones = jnp.ones((z.shape[0], LANES), BF16)
        for t in range(DA_HEADS):
            dv_ref[:, 2 * t * LANES:(2 * t + 1) * LANES] = z[:, 512 + t * LANES:512 + (t + 1) * LANES].astype(BF16)
            dv_ref[:, (2 * t + 1) * LANES:(2 * t + 2) * LANES] = ones

    @pl.when(j >= 3)
    def _():
        gate_ref[...] = jax.nn.sigmoid(z)


def _in_proj(x2d, seq, mixg, w_r, cos_t, sin_t, head_gains):
    n = x2d.shape[0]
    tm = _tile(seq, 1024)
    nsb = seq // tm
    bf = lambda w: jax.ShapeDtypeStruct((n, w), BF16)
    row = lambda w: pl.BlockSpec((tm, w), lambda i, j: (i, 0))
    return pl.pallas_call(
        _in_proj_kernel,
        grid=(n // tm, N_IN_CHUNKS),
        in_specs=[
            pl.BlockSpec((tm, D_MODEL), lambda i, j: (i, 0)),
            pl.BlockSpec((1, D_MODEL), lambda i, j: (0, 0)),
            pl.BlockSpec((D_MODEL, IN_CHUNK), lambda i, j: (0, j)),
            pl.BlockSpec((tm, LANES), lambda i, j: (i % nsb, 0)),
            pl.BlockSpec((tm, LANES), lambda i, j: (i % nsb, 0)),
            pl.BlockSpec((8, LANES), lambda i, j: (0, 0)),
        ],
        out_specs=[row(512), row(512), row(1024), row(512), row(256), row(256), row(512),
                   pl.BlockSpec((tm, IN_CHUNK), lambda i, j: (i, jnp.maximum(j - 3, 0)))],
        out_shape=[bf(512), bf(512), bf(1024), bf(512), bf(256), bf(256), bf(512),
                   jax.ShapeDtypeStruct((n, N_BRANCH * D_MODEL), F32)],
        scratch_shapes=[pltpu.VMEM((tm, D_MODEL), BF16)],
        compiler_params=_params(("parallel", "arbitrary")),
        name="in_proj",
    )(x2d, mixg, w_r, cos_t, sin_t, head_gains)


def _diff_attn_kernel(q_ref, k_ref, v_ref, lam_ref, og_ref, o_ref, m_sc, acc_sc, s_sc, *, tk, lam_init):
    q = q_ref[...]
    lo = _lane_iota(q.shape) < 64
    zero = jnp.zeros_like(q)
    qs = (jnp.where(lo, q, zero), jnp.where(lo, zero, q))
    m_sc[...] = jnp.full(m_sc.shape, -jnp.inf, F32)
    acc_sc[...] = jnp.zeros(acc_sc.shape, F32)
    nt = (((1,), (1,)), ((), ()))
    n_chunks = k_ref.shape[0] // tk

    def scores(j, slot):
        off = pl.multiple_of(j * tk, tk)
        k = k_ref[pl.ds(off, tk), :]
        for c in range(2):
            s_sc[slot, c] = lax.dot_general(qs[c], k, nt, preferred_element_type=F32)

    def process(j, slot):
        off = pl.multiple_of(j * tk, tk)
        v = v_ref[pl.ds(off, tk), :]
        for c in range(2):
            s = s_sc[slot, c]
            m_old = m_sc[c]
            m_new = jnp.maximum(m_old, jnp.max(s, axis=-1, keepdims=True))
            p = jnp.exp2(s - m_new)
            alpha = jnp.exp2(m_old - m_new)
            acc_sc[c] = alpha * acc_sc[c] + jnp.dot(p.astype(BF16), v, preferred_element_type=F32)
            m_sc[c] = m_new

    scores(0, 0)
    if n_chunks == 1:
        process(0, 0)
    else:
        def pair(jj, carry):
            scores(2 * jj + 1, 1)
            process(2 * jj, 0)
            scores(2 * jj + 2, 0)
            process(2 * jj + 1, 1)
            return carry

        lax.fori_loop(0, n_chunks // 2 - 1, pair, 0)
        scores(n_chunks - 1, 1)
        process(n_chunks - 2, 0)
        process(n_chunks - 1, 1)

    lp = lam_ref[...]
    lam = (jnp.exp(jnp.sum(lp[0:1] * lp[1:2], axis=-1, keepdims=True))
           - jnp.exp(jnp.sum(lp[2:3] * lp[3:4], axis=-1, keepdims=True)) + lam_init)
    o = (acc_sc[0, :, :LANES] / acc_sc[0, :, LANES:]
         - lam * (acc_sc[1, :, :LANES] / acc_sc[1, :, LANES:]))
    o_ref[...] = (_rms(o, og_ref[...], 2 * DA_DH) * (1.0 - lam_init)).astype(BF16)


def _diff_attn(dq, dk, dv, da_lambda, og, nb, seq, lam_init):
    n = dq.shape[0]
    tq = _tile(seq, DIFF_TQ)
    tk = _tile(seq, DIFF_TK)
    nq = seq // tq
    assert seq // tk == 1 or (seq // tk) % 2 == 0, (seq, tk)
    kern = functools.partial(_diff_attn_kernel, tk=tk, lam_init=lam_init)
    return pl.pallas_call(
        kern,
        grid=(nb, DA_HEADS, nq),
        in_specs=[
            pl.BlockSpec((tq, LANES), lambda b, h, i: (b * nq + i, h)),
            pl.BlockSpec((seq, LANES), lambda b, h, i: (b, h)),
            pl.BlockSpec((seq, 2 * LANES), lambda b, h, i: (b, h)),
            pl.BlockSpec((4, DA_DH), lambda b, h, i: (0, 0)),
            pl.BlockSpec((1, LANES), lambda b, h, i: (0, 0)),
        ],
        out_specs=pl.BlockSpec((tq, LANES), lambda b, h, i: (b * nq + i, h)),
        out_shape=jax.ShapeDtypeStruct((n, BRANCH_WIDTH), BF16),
        scratch_shapes=[pltpu.VMEM((2, tq, 1), F32), pltpu.VMEM((2, tq, 2 * LANES), F32),
                        pltpu.VMEM((2, 2, tq, tk), F32)],
        compiler_params=_params(("parallel", "parallel", "arbitrary")),
        name="diff_attn",
    )(dq, dk, dv, da_lambda, og)


def _window_attn_kernel(q_ref, kd_ref, vd_ref, sink_ref, o_ref, *, tq, seq):
    i = pl.program_id(1)
    nsub = tq // WINDOW
    kw_len = 3 * WINDOW
    nt = (((1,), (1,)), ((), ()))
    for n in range(nsub):
        gb = i * nsub + n
        start = pl.multiple_of(jnp.clip((gb - 1) * WINDOW, 0, seq - kw_len), WINDOW)
        kw = kd_ref[pl.ds(start, kw_len), :]
        vw = vd_ref[pl.ds(start, kw_len), :]
        qpos = gb * WINDOW + lax.broadcasted_iota(jnp.int32, (WINDOW, kw_len), 0)
        kpos = start + lax.broadcasted_iota(jnp.int32, (WINDOW, kw_len), 1)
        valid = jnp.abs(qpos - kpos) <= WINDOW
        for pair in range(WA_HEADS // 2):
            g = (2 * pair) // (WA_HEADS // WA_KV_HEADS)
            qp = q_ref[n * WINDOW:(n + 1) * WINDOW, pair * LANES:(pair + 1) * LANES]
            lo = _lane_iota(qp.shape) < 64
            zero = jnp.zeros_like(qp)
            kg = kw[:, g * LANES:(g + 1) * LANES]
            vg = vw[:, g * LANES:(g + 1) * LANES]
            res = []
            for e in range(2):
                h = 2 * pair + e
                qh = jnp.where(lo, qp, zero) if e == 0 else jnp.where(lo, zero, qp)
                s = lax.dot_general(qh, kg, nt, preferred_element_type=F32)
                s = jnp.where(valid, s, NEG_INF)
                sk = sink_ref[h:h + 1, 0:1]
                m = jnp.maximum(jnp.max(s, axis=-1, keepdims=True), sk)
                ex = jnp.exp(s - m)
                p = ex / (jnp.sum(ex, axis=-1, keepdims=True) + jnp.exp(sk - m))
                res.append(jnp.dot(p.astype(BF16), vg, preferred_element_type=F32))
            o_ref[n * WINDOW:(n + 1) * WINDOW, pair * LANES:(pair + 1) * LANES] = (
                jnp.where(lo, res[0], res[1]).astype(BF16))


def _window_attn(wq, wkd, wvd, sink_b, nb, seq):
    n = wq.shape[0]
    tq = _tile(seq, 512)
    nq = seq // tq
    kern = functools.partial(_window_attn_kernel, tq=tq, seq=seq)
    return pl.pallas_call(
        kern,
        grid=(nb, nq),
        in_specs=[
            pl.BlockSpec((tq, 512), lambda b, i: (b * nq + i, 0)),
            pl.BlockSpec((seq, 256), lambda b, i: (b, 0)),
            pl.BlockSpec((seq, 256), lambda b, i: (b, 0)),
            pl.BlockSpec((8, LANES), lambda b, i: (0, 0)),
        ],
        out_specs=pl.BlockSpec((tq, 512), lambda b, i: (b * nq + i, 0)),
        out_shape=jax.ShapeDtypeStruct((n, BRANCH_WIDTH), BF16),
        compiler_params=_params(("parallel", "arbitrary")),
        name="window_attn",
    )(wq, wkd, wvd, sink_b)


def _mem_kv_kernel(mem_ref, mg_ref, w_ref, kg_ref, mk_ref, mv_ref):
    h = _rms(mem_ref[...], mg_ref[...], D_MODEL).astype(BF16)
    kv = jnp.dot(h, w_ref[...], preferred_element_type=F32)
    for t in range(MA_HEADS):
        sl = slice(t * LANES, (t + 1) * LANES)
        mk_ref[:, sl] = _rms(kv[:, sl], kg_ref[...], MA_DH).astype(BF16)
    mv_ref[...] = kv[:, 512:1024].astype(BF16)


def _mem_kv(mem2d, mlen, mg, w_kv, kg):
    n = mem2d.shape[0]
    return pl.pallas_call(
        _mem_kv_kernel,
        grid=(n // mlen,),
        in_specs=[
            pl.BlockSpec((mlen, D_MODEL), lambda b: (b, 0)),
            pl.BlockSpec((1, D_MODEL), lambda b: (0, 0)),
            pl.BlockSpec((D_MODEL, 2 * MA_HEADS * MA_DH), lambda b: (0, 0)),
            pl.BlockSpec((1, LANES), lambda b: (0, 0)),
        ],
        out_specs=[pl.BlockSpec((mlen, 512), lambda b: (b, 0)), pl.BlockSpec((mlen, 512), lambda b: (b, 0))],
        out_shape=[jax.ShapeDtypeStruct((n, 512), BF16), jax.ShapeDtypeStruct((n, 512), BF16)],
        compiler_params=_params(("parallel",)),
        name="mem_kv",
    )(mem2d, mg, w_kv, kg)


def _mem_attn_kernel(q_ref, mk_ref, mv_ref, o_ref):
    nt = (((1,), (1,)), ((), ()))
    for t in range(MA_HEADS):
        sl = slice(t * LANES, (t + 1) * LANES)
        s = lax.dot_general(q_ref[:, sl], mk_ref[:, sl], nt, preferred_element_type=F32)
        m = jnp.max(s, axis=-1, keepdims=True)
        ex = jnp.exp(s - m)
        p = ex / jnp.sum(ex, axis=-1, keepdims=True)
        o_ref[:, sl] = jnp.dot(p.astype(BF16), mv_ref[:, sl], preferred_element_type=F32).astype(BF16)


def _mem_attn(mq, mk, mv, nb, seq, mlen):
    n = mq.shape[0]
    tq = _tile(seq, 1024)
    nq = seq // tq
    return pl.pallas_call(
        _mem_attn_kernel,
        grid=(nb, nq),
        in_specs=[
            pl.BlockSpec((tq, 512), lambda b, i: (b * nq + i, 0)),
            pl.BlockSpec((mlen, 512), lambda b, i: (b, 0)),
            pl.BlockSpec((mlen, 512), lambda b, i: (b, 0)),
        ],
        out_specs=pl.BlockSpec((tq, 512), lambda b, i: (b * nq + i, 0)),
        out_shape=jax.ShapeDtypeStruct((n, BRANCH_WIDTH), BF16),
        compiler_params=_params(("parallel", "arbitrary")),
        name="mem_attn",
    )(mq, mk, mv)


def _merge_kernel(x_ref, oda_ref, owa_ref, oma_ref, gate_ref, wb_ref, wo_ref, o_ref):
    merged = None
    for nbr, br in enumerate((oda_ref, owa_ref, oma_ref)):
        proj = jnp.dot(br[...], wb_ref[nbr], preferred_element_type=F32)
        term = gate_ref[:, nbr * D_MODEL:(nbr + 1) * D_MODEL] * proj
        merged = term if merged is None else merged + term
    o_ref[...] = x_ref[...] + jnp.dot(merged.astype(BF16), wo_ref[...], preferred_element_type=F32)


def _merge(x2d, oda, owa, oma, gates, wb, wo):
    n = x2d.shape[0]
    tm = _tile(n, 512)
    row = lambda w: pl.BlockSpec((tm, w), lambda i: (i, 0))
    return pl.pallas_call(
        _merge_kernel,
        grid=(n // tm,),
        in_specs=[row(D_MODEL), row(512), row(512), row(512), row(N_BRANCH * D_MODEL),
                  pl.BlockSpec((N_BRANCH, BRANCH_WIDTH, D_MODEL), lambda i: (0, 0, 0)),
                  pl.BlockSpec((D_MODEL, D_MODEL), lambda i: (0, 0))],
        out_specs=row(D_MODEL),
        out_shape=jax.ShapeDtypeStruct((n, D_MODEL), F32),
        compiler_params=_params(("parallel",)),
        name="merge",
    )(x2d, oda, owa, oma, gates, wb, wo)


def _top16_rows(s, row_ids, n_rows):
    out_iota = lax.broadcasted_iota(jnp.int32, (PEER_TOPK, s.shape[1]), 0)
    vals = jnp.zeros((PEER_TOPK, s.shape[1]), F32)
    idxs = jnp.zeros((PEER_TOPK, s.shape[1]), F32)
    for r in range(PEER_TOPK):
        m = jnp.max(s, axis=0, keepdims=True)
        sel = jnp.min(jnp.where(s == m, row_ids, float(n_rows)), axis=0, keepdims=True)
        s = jnp.where(row_ids == sel, -jnp.inf, s)
        vals = jnp.where(out_iota == r, m, vals)
        idxs = jnp.where(out_iota == r, sel, idxs)
    return vals, idxs


def _peer_q_kernel(x_ref, fg_ref, wq_ref, xn_ref, q_ref):
    xn = _rms(x_ref[...], fg_ref[...], D_MODEL).astype(BF16)
    xn_ref[...] = xn
    q = jnp.dot(xn, wq_ref[...], preferred_element_type=F32).astype(BF16)
    for hp in range(2 * PEER_HEADS):
        q_ref[hp] = q[:, hp * LANES:(hp + 1) * LANES]


def _peer_q(x2d, fg, wq):
    n = x2d.shape[0]
    tm = _tile(n, 512)
    return pl.pallas_call(
        _peer_q_kernel,
        grid=(n // tm,),
        in_specs=[
            pl.BlockSpec((tm, D_MODEL), lambda i: (i, 0)),
            pl.BlockSpec((1, D_MODEL), lambda i: (0, 0)),
            pl.BlockSpec((D_MODEL, 2 * PEER_HEADS * PEER_HALF), lambda i: (0, 0)),
        ],
        out_specs=[pl.BlockSpec((tm, D_MODEL), lambda i: (i, 0)),
                   pl.BlockSpec((2 * PEER_HEADS, tm, PEER_HALF), lambda i: (0, i, 0))],
        out_shape=[jax.ShapeDtypeStruct((n, D_MODEL), BF16),
                   jax.ShapeDtypeStruct((2 * PEER_HEADS, n, PEER_HALF), BF16)],
        compiler_params=_params(("parallel",)),
        name="peer_q",
    )(x2d, fg, wq)


def _peer_route_kernel(q_ref, keys_ref, wt_ref,
                       g_sc, ei_sc, ej_sc, gt_sc, eit_sc, ejt_sc, w_sc, *, tm, pitch):
    nt = (((1,), (1,)), ((), ()))
    key_ids = lax.broadcasted_iota(jnp.int32, (N_KEYS, tm), 0).astype(F32)
    sub8 = lax.broadcasted_iota(jnp.int32, (8, tm), 0)
    cand_ids = jnp.concatenate(
        [sub8, sub8 + 8] + [sub8 + a * PEER_TOPK for a in range(1, 8)] + [(sub8 + 8) * PEER_TOPK],
        axis=0).astype(F32)

    def head_body(h, carry):
        sv, si = [], []
        for p in range(2):
            s = lax.dot_general(keys_ref[2 * h + p], q_ref[2 * h + p], nt,
                                preferred_element_type=F32)
            v, ix = _top16_rows(s, key_ids, N_KEYS)
            sv.append(v)
            si.append(ix)
        pieces = [sv[0][0:1, :] + sv[1][0:8, :], sv[0][0:1, :] + sv[1][8:16, :]]
        for a in range(1, 8):
            piece = sv[0][a:a + 1, :] + sv[1][0:8, :]
            nb_valid = PEER_TOPK // (a + 1)
            pieces.append(piece if nb_valid >= 8 else jnp.where(sub8 < nb_valid, piece, -jnp.inf))
        pieces.append(sv[0][8:16, :] + sv[1][0:1, :])
        cand = jnp.concatenate(pieces, axis=0)
        fv, fi = _top16_rows(cand, cand_ids, PEER_TOPK * PEER_TOPK)
        fi = fi.astype(jnp.int32)
        fa = fi >> 4
        fb = fi & (PEER_TOPK - 1)
        ei = jnp.zeros(fv.shape, F32)
        ej = jnp.zeros(fv.shape, F32)
        for a in range(PEER_TOPK):
            ei = jnp.where(fa == a, si[0][a:a + 1, :], ei)
            ej = jnp.where(fb == a, si[1][a:a + 1, :], ej)
        ex = jnp.exp(fv - jnp.max(fv, axis=0, keepdims=True))
        g = ex / jnp.sum(ex, axis=0, keepdims=True)
        off = pl.multiple_of(h * PEER_TOPK, PEER_TOPK)
        g_sc[pl.ds(off, PEER_TOPK), :] = g
        ei_sc[pl.ds(off, PEER_TOPK), :] = ei
        ej_sc[pl.ds(off, PEER_TOPK), :] = ej
        return carry

    lax.fori_loop(0, PEER_HEADS, head_body, 0, unroll=2)

    for c in range(tm // LANES):
        sl = slice(c * LANES, (c + 1) * LANES)
        gt_sc[sl, :] = g_sc[:, sl].T
        eit_sc[sl, :] = ei_sc[:, sl].T
        ejt_sc[sl, :] = ej_sc[:, sl].T

    sub = lax.broadcasted_iota(jnp.int32, (N_KEYS, LANES), 0).astype(F32)

    def tok_body(t, carry):
        g_r = gt_sc[pl.ds(t, 1), :]
        at = jnp.where(sub == eit_sc[pl.ds(t, 1), :], g_r, 0.0).astype(BF16)
        bt = jnp.where(sub == ejt_sc[pl.ds(t, 1), :], 1.0, 0.0).astype(BF16)
        w_t = lax.dot_general(at, bt, nt, preferred_element_type=F32)
        w_sc[pl.ds(t, N_KEYS, stride=pitch), :] = w_t
        return carry

    lax.fori_loop(0, tm, tok_body, 0, unroll=32)

    for i in range(N_KEYS):
        wt_ref[i] = w_sc[i * pitch:i * pitch + tm, :].astype(BF16)


def _peer_route(q3, keys):
    n = q3.shape[1]
    tm = _tile(n, 128)
    pitch = tm + 8 if (tm // 8) % 2 == 0 else tm
    kern = functools.partial(_peer_route_kernel, tm=tm, pitch=pitch)
    return pl.pallas_call(
        kern,
        grid=(n // tm,),
        in_specs=[
            pl.BlockSpec((2 * PEER_HEADS, tm, PEER_HALF), lambda i: (0, i, 0)),
            pl.BlockSpec((2 * PEER_HEADS, N_KEYS, PEER_HALF), lambda i: (0, 0, 0)),
        ],
        out_specs=pl.BlockSpec((N_KEYS, tm, N_KEYS), lambda i: (0, i, 0)),
        out_shape=jax.ShapeDtypeStruct((N_KEYS, n, N_KEYS), BF16),
        scratch_shapes=[
            pltpu.VMEM((PEER_HEADS * PEER_TOPK, tm), F32),
            pltpu.VMEM((PEER_HEADS * PEER_TOPK, tm), F32),
            pltpu.VMEM((PEER_HEADS * PEER_TOPK, tm), F32),
            pltpu.VMEM((tm, PEER_HEADS * PEER_TOPK), F32),
            pltpu.VMEM((tm, PEER_HEADS * PEER_TOPK), F32),
            pltpu.VMEM((tm, PEER_HEADS * PEER_TOPK), F32),
            pltpu.VMEM((N_KEYS * pitch, N_KEYS), F32),
        ],
        compiler_params=_params(("parallel",)),
        name="peer_route",
    )(q3, keys)


def _peer_dense_kernel(xn_ref, ut_ref, wt_ref, v_ref, x_ref, o_ref, acc_sc, *, ce):
    c = pl.program_id(1)

    @pl.when(c == 0)
    def _():
        acc_sc[...] = jnp.zeros(acc_sc.shape, F32)

    a = jnp.dot(xn_ref[...], ut_ref[...], preferred_element_type=F32)
    act = 0.5 * a * (1.0 + lax.erf(a * math.sqrt(0.5)))
    h = jnp.concatenate([(wt_ref[s].astype(F32) * act[:, s * LANES:(s + 1) * LANES]).astype(BF16)
                         for s in range(ce // LANES)], axis=1)
    acc_sc[...] += jnp.dot(h, v_ref[...], preferred_element_type=F32)

    @pl.when(c == pl.num_programs(1) - 1)
    def _():
        o_ref[...] = x_ref[...] + acc_sc[...]


def _peer_dense(xn, ut, wt, v, x2d):
    n = x2d.shape[0]
    tm = _tile(n, 512)
    ce = 1024
    kern = functools.partial(_peer_dense_kernel, ce=ce)
    return pl.pallas_call(
        kern,
        grid=(n // tm, N_EXPERTS // ce),
        in_specs=[
            pl.BlockSpec((tm, D_MODEL), lambda i, c: (i, 0)),
            pl.BlockSpec((D_MODEL, ce), lambda i, c: (0, c)),
            pl.BlockSpec((ce // N_KEYS, tm, N_KEYS), lambda i, c: (c, i, 0)),
            pl.BlockSpec((ce, D_MODEL), lambda i, c: (c, 0)),
            pl.BlockSpec((tm, D_MODEL), lambda i, c: (i, 0)),
        ],
        out_specs=pl.BlockSpec((tm, D_MODEL), lambda i, c: (i, 0)),
        out_shape=jax.ShapeDtypeStruct((n, D_MODEL), F32),
        scratch_shapes=[pltpu.VMEM((tm, D_MODEL), F32)],
        compiler_params=_params(("parallel", "arbitrary")),
        name="peer_dense",
    )(xn, ut, wt, v, x2d)


def _rope_tiles(seq):
    inv = 1.0 / (ROPE_THETA ** (jnp.arange(0, DA_DH, 2, dtype=F32) / DA_DH))
    ang = jnp.arange(seq, dtype=F32)[:, None] * inv[None, :]
    cos, sin = jnp.cos(ang), jnp.sin(ang)
    cos_t = jnp.tile(cos, (1, 4))
    sign = jnp.tile(jnp.concatenate([-jnp.ones((32,), F32), jnp.ones((32,), F32)]), 2)
    sin_t = jnp.tile(sin, (1, 4)) * sign[None, :]
    return cos_t, sin_t


def _prep_layer(l, mix_norm, w_in, da_q_norm, da_k_norm, da_lambda, da_out_norm, wa_q_norm, wa_k_norm,
                wa_sink, mem_norm, w_mem_kv, ma_q_norm, ma_k_norm, w_branch, w_out, ffn_norm,
                peer_wq, peer_keys, peer_u, peer_v):
    w = w_in[l]
    pad = jnp.zeros((D_MODEL, 256), F32)
    w_r = jnp.concatenate([
        w[:, DA_Q_OFF:DA_V_OFF],
        w[:, WA_Q_OFF:MA_Q_OFF], pad,
        w[:, MA_Q_OFF:GATE_OFF], w[:, DA_V_OFF:WA_Q_OFF],
        w[:, GATE_OFF:]], axis=1).astype(BF16)
    t2 = lambda g: jnp.tile(g.astype(F32), 2)
    head_gains = jnp.stack([
        t2(da_q_norm[l]) * (DA_DH ** -0.5 * math.log2(math.e)), t2(da_k_norm[l]),
        t2(wa_q_norm[l]) * WA_DH ** -0.5, t2(wa_k_norm[l]),
        ma_q_norm[l].astype(F32) * MA_DH ** -0.5,
        jnp.zeros((LANES,), F32), jnp.zeros((LANES,), F32), jnp.zeros((LANES,), F32)])
    return dict(
        mixg=mix_norm[l][None, :], w_r=w_r, head_gains=head_gains,
        da_lambda=da_lambda[l], og=da_out_norm[l][None, :],
        sink=jnp.broadcast_to(wa_sink[l].astype(F32)[:, None], (WA_HEADS, LANES)),
        mem_g=mem_norm[l][None, :], w_kv=w_mem_kv[l].astype(BF16), ma_kg=ma_k_norm[l][None, :],
        wb=w_branch[l].astype(BF16), wo=w_out[l].astype(BF16),
        ffn_g=ffn_norm[l][None, :], peer_wq=peer_wq[l].astype(BF16),
        keys=peer_keys[l].reshape(2 * PEER_HEADS, N_KEYS, PEER_HALF).astype(BF16),
        ut=peer_u[l].astype(BF16).T, v=peer_v[l].astype(BF16),
    )


def _trunk(x, mem, layers):
    nb, seq, _ = x.shape
    mlen = mem.shape[1]
    x2d = x.reshape(nb * seq, D_MODEL)
    mem2d = mem.reshape(nb * mlen, D_MODEL)
    cos_t, sin_t = _rope_tiles(seq)
    for l, p in enumerate(layers):
        lam_init = 0.8 - 0.6 * math.exp(-0.3 * l)
        dq, dk, dv, wq, wkd, wvd, mq, gates = _in_proj(x2d, seq, p["mixg"], p["w_r"], cos_t, sin_t,
                                                      p["head_gains"])
        oda = _diff_attn(dq, dk, dv, p["da_lambda"], p["og"], nb, seq, lam_init)
        owa = _window_attn(wq, wkd, wvd, p["sink"], nb, seq)
        mk, mv = _mem_kv(mem2d, mlen, p["mem_g"], p["w_kv"], p["ma_kg"])
        oma = _mem_attn(mq, mk, mv, nb, seq, mlen)
        x1 = _merge(x2d, oda, owa, oma, gates, p["wb"], p["wo"])
        xn, q3 = _peer_q(x1, p["ffn_g"], p["peer_wq"])
        wt = _peer_route(q3, p["keys"])
        x2d = _peer_dense(xn, p["ut"], wt, p["v"], x1)
    return x2d.reshape(nb, seq, D_MODEL)


def kernel(x_prompt, x_sample, mem_prompt, mem_sample, mix_norm, w_in, da_q_norm, da_k_norm, da_lambda, da_out_norm, wa_q_norm, wa_k_norm, wa_sink, mem_norm, w_mem_kv, ma_q_norm, ma_k_norm, w_branch, w_out, ffn_norm, peer_wq, peer_keys, peer_u, peer_v):
    weights = (mix_norm, w_in, da_q_norm, da_k_norm, da_lambda, da_out_norm, wa_q_norm, wa_k_norm,
               wa_sink, mem_norm, w_mem_kv, ma_q_norm, ma_k_norm, w_branch, w_out, ffn_norm,
               peer_wq, peer_keys, peer_u, peer_v)
    layers = [_prep_layer(l, *weights) for l in range(w_in.shape[0])]
    y_prompt = _trunk(x_prompt, mem_prompt, layers)
    y_sample = _trunk(x_sample, mem_sample, layers)
    return (y_prompt, y_sample)
```

```python
import functools
import math

import jax
import jax.numpy as jnp
import numpy as np
from jax import lax
from jax.experimental import pallas as pl
from jax.experimental.pallas import tpu as pltpu

F32 = jnp.float32
BF16 = jnp.bfloat16

D_MODEL = 1024
NORM_EPS = 1e-6
NEG_INF = -1e30
ROPE_THETA = 10000.0
LANES = 128

DA_HEADS, DA_DH = 4, 64
WA_HEADS, WA_KV_HEADS, WA_DH, WINDOW = 8, 2, 64, 128
MA_HEADS, MA_DH = 4, 128
N_BRANCH, BRANCH_WIDTH = 3, 512
PEER_HEADS, N_KEYS, PEER_HALF, PEER_TOPK = 8, 128, 128, 16
N_EXPERTS = N_KEYS * N_KEYS

DA_Q_OFF = 0
DA_K_OFF = 512
DA_V_OFF = 1024
WA_Q_OFF = 1536
WA_K_OFF = 2048
WA_V_OFF = 2176
MA_Q_OFF = 2304
GATE_OFF = 2816
IN_CHUNK = 1024
N_IN_CHUNKS = 6

VMEM_LIMIT = 48 * 1024 * 1024
DIFF_TQ, DIFF_TK = 256, 1024


def _tile(n, pref):
    t = min(pref, n)
    t -= t % LANES
    while n % t:
        t -= LANES
    return t


def _params(sem):
    return pltpu.CompilerParams(dimension_semantics=sem, vmem_limit_bytes=VMEM_LIMIT)


def _lane_iota(shape):
    return lax.broadcasted_iota(jnp.int32, shape, len(shape) - 1)


def _rms(xf, gain_row, width):
    ms = jnp.sum(xf * xf, axis=-1, keepdims=True) * (1.0 / width)
    return xf * lax.rsqrt(ms + NORM_EPS) * gain_row


def _seg64_norm_rope(zt, gain_row, cosb, sinb):
    lane = _lane_iota(zt.shape)
    lo = lane < 64
    sq = zt * zt
    s0 = jnp.sum(jnp.where(lo, sq, 0.0), axis=-1, keepdims=True)
    s1 = jnp.sum(jnp.where(lo, 0.0, sq), axis=-1, keepdims=True)
    ms = jnp.where(lo, s0, s1) * (1.0 / 64)
    y = zt * lax.rsqrt(ms + NORM_EPS) * gain_row
    first = (lane % 64) < 32
    rot = jnp.where(first, pltpu.roll(y, 96, 1), pltpu.roll(y, 32, 1))
    return y * cosb + rot * sinb


def _in_proj_kernel(x_ref, mixg_ref, w_ref, cos_ref, sin_ref, hg_ref,
                    dq_ref, dk_ref, dv_ref, wq_ref, wkd_ref, wvd_ref, mq_ref, gate_ref, xn_sc):
    j = pl.program_id(1)

    @pl.when(j == 0)
    def _():
        xn_sc[...] = _rms(x_ref[...], mixg_ref[...], D_MODEL).astype(BF16)

    z = jnp.dot(xn_sc[...], w_ref[...], preferred_element_type=F32)
    cosb = cos_ref[...]
    sinb = sin_ref[...]

    @pl.when(j == 0)
    def _():
        for t in range(4):
            sl = slice(t * LANES, (t + 1) * LANES)
            dq_ref[:, sl] = _seg64_norm_rope(z[:, sl], hg_ref[0:1, :], cosb, sinb).astype(BF16)
            zs = z[:, 512 + t * LANES:512 + (t + 1) * LANES]
            dk_ref[:, sl] = _seg64_norm_rope(zs, hg_ref[1:2, :], cosb, sinb).astype(BF16)

    @pl.when(j == 1)
    def _():
        for t in range(4):
            sl = slice(t * LANES, (t + 1) * LANES)
            wq_ref[:, sl] = _seg64_norm_rope(z[:, sl], hg_ref[2:3, :], cosb, sinb).astype(BF16)
        k = _seg64_norm_rope(z[:, 512:640], hg_ref[3:4, :], cosb, sinb)
        v = z[:, 640:768]
        lo = _lane_iota(k.shape) < 64
        ones = jnp.ones(k.shape, BF16)
        for src, dst, stride in ((k, wkd_ref, 1), (v, wvd_ref, 2)):
            sw = pltpu.roll(src, 64, 1)
            dst[:, 0:LANES] = jnp.where(lo, src, sw).astype(BF16)
            dst[:, stride * LANES:(stride + 1) * LANES] = jnp.where(lo, sw, src).astype(BF16)
        wvd_ref[:, LANES:2 * LANES] = ones
        wvd_ref[:, 3 * LANES:4 * LANES] = ones

    @pl.when(j == 2)
    def _():
        for t in range(4):
            sl = slice(t * LANES, (t + 1) * LANES)
            mq_ref[:, sl] = _rms(z[:, sl], hg_ref[4:5, :], MA_DH).astype(BF16)
        ones = jnp.ones((z.shape[0], LANES), BF16)
        for t in range(DA_HEADS):
            dv_ref[:, 2 * t * LANES:(2 * t + 1) * LANES] = z[:, 512 + t * LANES:512 + (t + 1) * LANES].astype(BF16)
            dv_ref[:, (2 * t + 1) * LANES:(2 * t + 2) * LANES] = ones

    @pl.when(j >= 3)
    def _():
        gate_ref[...] = jax.nn.sigmoid(z)


def _in_proj(x2d, seq, mixg, w_r, cos_t, sin_t, head_gains):
    n = x2d.shape[0]
    tm = _tile(seq, 1024)
    nsb = seq // tm
    bf = lambda w: jax.ShapeDtypeStruct((n, w), BF16)
    row = lambda w: pl.BlockSpec((tm, w), lambda i, j: (i, 0))
    return pl.pallas_call(
        _in_proj_kernel,
        grid=(n // tm, N_IN_CHUNKS),
        in_specs=[
            pl.BlockSpec((tm, D_MODEL), lambda i, j: (i, 0)),
            pl.BlockSpec((1, D_MODEL), lambda i, j: (0, 0)),
            pl.BlockSpec((D_MODEL, IN_CHUNK), lambda i, j: (0, j)),
            pl.BlockSpec((tm, LANES), lambda i, j: (i % nsb, 0)),
            pl.BlockSpec((tm, LANES), lambda i, j: (i % nsb, 0)),
            pl.BlockSpec((8, LANES), lambda i, j: (0, 0)),
        ],
        out_specs=[row(512), row(512), row(1024), row(512), row(256), row(512), row(512),
                   pl.BlockSpec((tm, IN_CHUNK), lambda i, j: (i, jnp.maximum(j - 3, 0)))],
        out_shape=[bf(512), bf(512), bf(1024), bf(512), bf(256), bf(512), bf(512),
                   jax.ShapeDtypeStruct((n, N_BRANCH * D_MODEL), F32)],
        scratch_shapes=[pltpu.VMEM((tm, D_MODEL), BF16)],
        compiler_params=_params(("parallel", "arbitrary")),
        name="in_proj",
    )(x2d, mixg, w_r, cos_t, sin_t, head_gains)


def _diff_attn_kernel(q_ref, k_ref, v_ref, lam_ref, og_ref, o_ref, m_sc, acc_sc, s_sc, *, tk, lam_init):
    q = q_ref[...]
    lo = _lane_iota(q.shape) < 64
    zero = jnp.zeros_like(q)
    qs = (jnp.where(lo, q, zero), jnp.where(lo, zero, q))
    m_sc[...] = jnp.full(m_sc.shape, -jnp.inf, F32)
    acc_sc[...] = jnp.zeros(acc_sc.shape, F32)
    nt = (((1,), (1,)), ((), ()))
    n_chunks = k_ref.shape[0] // tk

    def scores(j, slot):
        off = pl.multiple_of(j * tk, tk)
        k = k_ref[pl.ds(off, tk), :]
        for c in range(2):
            s_sc[slot, c] = lax.dot_general(qs[c], k, nt, preferred_element_type=F32)

    def process(j, slot):
        off = pl.multiple_of(j * tk, tk)
        v = v_ref[pl.ds(off, tk), :]
        for c in range(2):
            s = s_sc[slot, c]
            m_old = m_sc[c]
            m_new = jnp.maximum(m_old, jnp.max(s, axis=-1, keepdims=True))
            p = jnp.exp2(s - m_new)
            alpha = jnp.exp2(m_old - m_new)
            acc_sc[c] = alpha * acc_sc[c] + jnp.dot(p.astype(BF16), v, preferred_element_type=F32)
            m_sc[c] = m_new

    scores(0, 0)
    if n_chunks == 1:
        process(0, 0)
    else:
        def pair(jj, carry):
            scores(2 * jj + 1, 1)
            process(2 * jj, 0)
            scores(2 * jj + 2, 0)
            process(2 * jj + 1, 1)
            return carry

        lax.fori_loop(0, n_chunks // 2 - 1, pair, 0)
        scores(n_chunks - 1, 1)
        process(n_chunks - 2, 0)
        process(n_chunks - 1, 1)

    lp = lam_ref[...]
    lam = (jnp.exp(jnp.sum(lp[0:1] * lp[1:2], axis=-1, keepdims=True))
           - jnp.exp(jnp.sum(lp[2:3] * lp[3:4], axis=-1, keepdims=True)) + lam_init)
    o = (acc_sc[0, :, :LANES] / acc_sc[0, :, LANES:]
         - lam * (acc_sc[1, :, :LANES] / acc_sc[1, :, LANES:]))
    o_ref[...] = (_rms(o, og_ref[...], 2 * DA_DH) * (1.0 - lam_init)).astype(BF16)


def _diff_attn(dq, dk, dv, da_lambda, og, nb, seq, lam_init):
    n = dq.shape[0]
    tq = _tile(seq, DIFF_TQ)
    tk = _tile(seq, DIFF_TK)
    nq = seq // tq
    assert seq // tk == 1 or (seq // tk) % 2 == 0, (seq, tk)
    kern = functools.partial(_diff_attn_kernel, tk=tk, lam_init=lam_init)
    return pl.pallas_call(
        kern,
        grid=(nb, DA_HEADS, nq),
        in_specs=[
            pl.BlockSpec((tq, LANES), lambda b, h, i: (b * nq + i, h)),
            pl.BlockSpec((seq, LANES), lambda b, h, i: (b, h)),
            pl.BlockSpec((seq, 2 * LANES), lambda b, h, i: (b, h)),
            pl.BlockSpec((4, DA_DH), lambda b, h, i: (0, 0)),
            pl.BlockSpec((1, LANES), lambda b, h, i: (0, 0)),
        ],
        out_specs=pl.BlockSpec((tq, LANES), lambda b, h, i: (b * nq + i, h)),
        out_shape=jax.ShapeDtypeStruct((n, BRANCH_WIDTH), BF16),
        scratch_shapes=[pltpu.VMEM((2, tq, 1), F32), pltpu.VMEM((2, tq, 2 * LANES), F32),
                        pltpu.VMEM((2, 2, tq, tk), F32)],
        compiler_params=_params(("parallel", "parallel", "arbitrary")),
        name="diff_attn",
    )(dq, dk, dv, da_lambda, og)


def _window_attn_kernel(q_ref, kd_ref, vd_ref, sink_ref, o_ref, *, tq, seq):
    i = pl.program_id(1)
    nsub = tq // WINDOW
    kw_len = 3 * WINDOW
    nt = (((1,), (1,)), ((), ()))
    rep = WA_HEADS // WA_KV_HEADS
    lo = _lane_iota((WINDOW, LANES)) < 64
    log2e = math.log2(math.e)
    for n in range(nsub):
        gb = i * nsub + n
        start = pl.multiple_of(jnp.clip((gb - 1) * WINDOW, 0, seq - kw_len), WINDOW)
        qpos = gb * WINDOW + lax.broadcasted_iota(jnp.int32, (WINDOW, kw_len), 0)
        kpos = start + lax.broadcasted_iota(jnp.int32, (WINDOW, kw_len), 1)
        valid1 = jnp.abs(qpos - kpos) <= WINDOW
        valid = jnp.concatenate([valid1] * rep, axis=0)
        for g in range(WA_KV_HEADS):
            kg = kd_ref[pl.ds(start, kw_len), g * LANES:(g + 1) * LANES]
            vg = vd_ref[pl.ds(start, kw_len), 2 * g * LANES:2 * (g + 1) * LANES]
            tiles, sinks = [], []
            for e in range(rep):
                h = g * rep + e
                qp = q_ref[n * WINDOW:(n + 1) * WINDOW, (h // 2) * LANES:(h // 2 + 1) * LANES]
                zero = jnp.zeros_like(qp)
                tiles.append(jnp.where(lo, qp, zero) if h % 2 == 0 else jnp.where(lo, zero, qp))
                sinks.append(jnp.broadcast_to(sink_ref[h:h + 1, 0:1] * log2e, (WINDOW, 1)))
            qs = jnp.concatenate(tiles, axis=0)
            sk = jnp.concatenate(sinks, axis=0)
            s = lax.dot_general(qs, kg, nt, preferred_element_type=F32)
            s = jnp.where(valid, s, NEG_INF)
            m = jnp.maximum(jnp.max(s, axis=-1, keepdims=True), sk)
            r = jnp.dot(jnp.exp2(s - m).astype(BF16), vg, preferred_element_type=F32)
            o = r[:, :LANES] / (r[:, LANES:] + jnp.exp2(sk - m))
            for pr in range(rep // 2):
                col = (g * rep // 2 + pr) * LANES
                o_ref[n * WINDOW:(n + 1) * WINDOW, col:col + LANES] = jnp.where(
                    lo, o[2 * pr * WINDOW:(2 * pr + 1) * WINDOW], o[(2 * pr + 1) * WINDOW:(2 * pr + 2) * WINDOW]
                ).astype(BF16)


def _window_attn(wq, wkd, wvd, sink_b, nb, seq):
    n = wq.shape[0]
    tq = _tile(seq, 512)
    nq = seq // tq
    kern = functools.partial(_window_attn_kernel, tq=tq, seq=seq)
    return pl.pallas_call(
        kern,
        grid=(nb, nq),
        in_specs=[
            pl.BlockSpec((tq, 512), lambda b, i: (b * nq + i, 0)),
            pl.BlockSpec((seq, 256), lambda b, i: (b, 0)),
            pl.BlockSpec((seq, 512), lambda b, i: (b, 0)),
            pl.BlockSpec((8, LANES), lambda b, i: (0, 0)),
        ],
        out_specs=pl.BlockSpec((tq, 512), lambda b, i: (b * nq + i, 0)),
        out_shape=jax.ShapeDtypeStruct((n, BRANCH_WIDTH), BF16),
        compiler_params=_params(("parallel", "arbitrary")),
        name="window_attn",
    )(wq, wkd, wvd, sink_b)


def _mem_kv_kernel(mem_ref, mg_ref, w_ref, kg_ref, mk_ref, mv_ref):
    h = _rms(mem_ref[...], mg_ref[...], D_MODEL).astype(BF16)
    kv = jnp.dot(h, w_ref[...], preferred_element_type=F32)
    for t in range(MA_HEADS):
        sl = slice(t * LANES, (t + 1) * LANES)
        mk_ref[:, sl] = _rms(kv[:, sl], kg_ref[...], MA_DH).astype(BF16)
    mv_ref[...] = kv[:, 512:1024].astype(BF16)


def _mem_kv(mem2d, mlen, mg, w_kv, kg):
    n = mem2d.shape[0]
    return pl.pallas_call(
        _mem_kv_kernel,
        grid=(n // mlen,),
        in_specs=[
            pl.BlockSpec((mlen, D_MODEL), lambda b: (b, 0)),
            pl.BlockSpec((1, D_MODEL), lambda b: (0, 0)),
            pl.BlockSpec((D_MODEL, 2 * MA_HEADS * MA_DH), lambda b: (0, 0)),
            pl.BlockSpec((1, LANES), lambda b: (0, 0)),
        ],
        out_specs=[pl.BlockSpec((mlen, 512), lambda b: (b, 0)), pl.BlockSpec((mlen, 512), lambda b: (b, 0))],
        out_shape=[jax.ShapeDtypeStruct((n, 512), BF16), jax.ShapeDtypeStruct((n, 512), BF16)],
        compiler_params=_params(("parallel",)),
        name="mem_kv",
    )(mem2d, mg, w_kv, kg)


def _mem_attn_kernel(q_ref, mk_ref, mv_ref, o_ref):
    nt = (((1,), (1,)), ((), ()))
    for t in range(MA_HEADS):
        sl = slice(t * LANES, (t + 1) * LANES)
        s = lax.dot_general(q_ref[:, sl], mk_ref[:, sl], nt, preferred_element_type=F32)
        m = jnp.max(s, axis=-1, keepdims=True)
        ex = jnp.exp(s - m)
        p = ex / jnp.sum(ex, axis=-1, keepdims=True)
        o_ref[:, sl] = jnp.dot(p.astype(BF16), mv_ref[:, sl], preferred_element_type=F32).astype(BF16)


def _mem_attn(mq, mk, mv, nb, seq, mlen):
    n = mq.shape[0]
    tq = _tile(seq, 1024)
    nq = seq // tq
    return pl.pallas_call(
        _mem_attn_kernel,
        grid=(nb, nq),
        in_specs=[
            pl.BlockSpec((tq, 512), lambda b, i: (b * nq + i, 0)),
            pl.BlockSpec((mlen, 512), lambda b, i: (b, 0)),
            pl.BlockSpec((mlen, 512), lambda b, i: (b, 0)),
        ],
        out_specs=pl.BlockSpec((tq, 512), lambda b, i: (b * nq + i, 0)),
        out_shape=jax.ShapeDtypeStruct((n, BRANCH_WIDTH), BF16),
        compiler_params=_params(("parallel", "arbitrary")),
        name="mem_attn",
    )(mq, mk, mv)


def _merge_kernel(x_ref, oda_ref, owa_ref, oma_ref, gate_ref, wb_ref, wo_ref, o_ref):
    merged = None
    for nbr, br in enumerate((oda_ref, owa_ref, oma_ref)):
        proj = jnp.dot(br[...], wb_ref[nbr], preferred_element_type=F32)
        term = gate_ref[:, nbr * D_MODEL:(nbr + 1) * D_MODEL] * proj
        merged = term if merged is None else merged + term
    o_ref[...] = x_ref[...] + jnp.dot(merged.astype(BF16), wo_ref[...], preferred_element_type=F32)


def _merge(x2d, oda, owa, oma, gates, wb, wo):
    n = x2d.shape[0]
    tm = _tile(n, 512)
    row = lambda w: pl.BlockSpec((tm, w), lambda i: (i, 0))
    return pl.pallas_call(
        _merge_kernel,
        grid=(n // tm,),
        in_specs=[row(D_MODEL), row(512), row(512), row(512), row(N_BRANCH * D_MODEL),
                  pl.BlockSpec((N_BRANCH, BRANCH_WIDTH, D_MODEL), lambda i: (0, 0, 0)),
                  pl.BlockSpec((D_MODEL, D_MODEL), lambda i: (0, 0))],
        out_specs=row(D_MODEL),
        out_shape=jax.ShapeDtypeStruct((n, D_MODEL), F32),
        compiler_params=_params(("parallel",)),
        name="merge",
    )(x2d, oda, owa, oma, gates, wb, wo)


def _top16_lead(rows, ids):
    rows = list(rows)
    vals, idxs = [], []
    for _ in range(PEER_TOPK):
        level = list(zip(rows, ids))
        while len(level) > 1:
            nxt = []
            for j in range(0, len(level) - 1, 2):
                (va, ia), (vb, ib) = level[j], level[j + 1]
                take = va >= vb
                nxt.append((jnp.maximum(va, vb), jnp.where(take, ia, ib)))
            if len(level) % 2:
                nxt.append(level[-1])
            level = nxt
        m, sel = level[0]
        rows = [jnp.where(sel == i, -jnp.inf, r) for r, i in zip(rows, ids)]
        vals.append(m)
        idxs.append(sel)
    return vals, idxs


def _peer_q_kernel(x_ref, fg_ref, wq_ref, xn_ref, q_ref):
    xn = _rms(x_ref[...], fg_ref[...], D_MODEL).astype(BF16)
    xn_ref[...] = xn
    q = jnp.dot(xn, wq_ref[...], preferred_element_type=F32).astype(BF16)
    for hp in range(2 * PEER_HEADS):
        q_ref[hp] = q[:, hp * LANES:(hp + 1) * LANES]


def _peer_q(x2d, fg, wq):
    n = x2d.shape[0]
    tm = _tile(n, 512)
    return pl.pallas_call(
        _peer_q_kernel,
        grid=(n // tm,),
        in_specs=[
            pl.BlockSpec((tm, D_MODEL), lambda i: (i, 0)),
            pl.BlockSpec((1, D_MODEL), lambda i: (0, 0)),
            pl.BlockSpec((D_MODEL, 2 * PEER_HEADS * PEER_HALF), lambda i: (0, 0)),
        ],
        out_specs=[pl.BlockSpec((tm, D_MODEL), lambda i: (i, 0)),
                   pl.BlockSpec((2 * PEER_HEADS, tm, PEER_HALF), lambda i: (0, i, 0))],
        out_shape=[jax.ShapeDtypeStruct((n, D_MODEL), BF16),
                   jax.ShapeDtypeStruct((2 * PEER_HEADS, n, PEER_HALF), BF16)],
        compiler_params=_params(("parallel",)),
        name="peer_q",
    )(x2d, fg, wq)


_CAND_PAIRS = tuple((a, b) for a in range(PEER_TOPK) for b in range(PEER_TOPK) if (a + 1) * (b + 1) <= PEER_TOPK)


def _peer_topk_kernel(q_ref, keys_ref, g_ref, ei_ref, ej_ref, s0_sc, s1_sc, rg_sc, ri_sc, rj_sc, *, tm):
    nt = (((1,), (1,)), ((), ()))
    gt = tm // LANES
    key_ids = [float(k) for k in range(N_KEYS)]
    cand_ids = [float(a * PEER_TOPK + b) for a, b in _CAND_PAIRS]

    def head_body(h, carry):
        sv, si = [], []
        for p, s_sc in enumerate((s0_sc, s1_sc)):
            s = lax.dot_general(keys_ref[2 * h + p], q_ref[2 * h + p], nt,
                                preferred_element_type=F32)
            for tt in range(gt):
                s_sc[pl.ds(tt, N_KEYS, stride=gt), :] = s[:, tt * LANES:(tt + 1) * LANES]
            v, ix = _top16_lead([s_sc[k * gt:(k + 1) * gt, :] for k in range(N_KEYS)], key_ids)
            sv.append(v)
            si.append(ix)
        fv, fi = _top16_lead([sv[0][a] + sv[1][b] for a, b in _CAND_PAIRS], cand_ids)
        ex = [jnp.exp(v - fv[0]) for v in fv]
        den = ex[0]
        for e in ex[1:]:
            den = den + e
        for k in range(PEER_TOPK):
            fik = fi[k].astype(jnp.int32)
            fa = fik >> 4
            fb = fik & (PEER_TOPK - 1)
            ei = jnp.zeros((gt, LANES), F32)
            ej = jnp.zeros((gt, LANES), F32)
            for a in range(PEER_TOPK):
                ei = jnp.where(fa == a, si[0][a], ei)
                ej = jnp.where(fb == a, si[1][a], ej)
            off = pl.multiple_of((h * PEER_TOPK + k) * gt, gt)
            rg_sc[pl.ds(off, gt), :] = ex[k] / den
            ri_sc[pl.ds(off, gt), :] = ei
            rj_sc[pl.ds(off, gt), :] = ej
        return carry

    lax.fori_loop(0, PEER_HEADS, head_body, 0)

    for tt in range(gt):
        sl = slice(tt * LANES, (tt + 1) * LANES)
        g_ref[sl, :] = rg_sc[pl.ds(tt, N_KEYS, stride=gt), :].T
        ei_ref[sl, :] = ri_sc[pl.ds(tt, N_KEYS, stride=gt), :].T
        ej_ref[sl, :] = rj_sc[pl.ds(tt, N_KEYS, stride=gt), :].T


def _peer_topk(q3, keys):
    n = q3.shape[1]
    tm = _tile(n, 8 * LANES)
    slots = PEER_HEADS * PEER_TOPK
    out = jax.ShapeDtypeStruct((n, slots), F32)
    return pl.pallas_call(
        functools.partial(_peer_topk_kernel, tm=tm),
        grid=(n // tm,),
        in_specs=[
            pl.BlockSpec((2 * PEER_HEADS, tm, PEER_HALF), lambda i: (0, i, 0)),
            pl.BlockSpec((2 * PEER_HEADS, N_KEYS, PEER_HALF), lambda i: (0, 0, 0)),
        ],
        out_specs=[pl.BlockSpec((tm, slots), lambda i: (i, 0))] * 3,
        out_shape=[out, out, out],
        scratch_shapes=[pltpu.VMEM((N_KEYS * tm // LANES, LANES), F32)] * 2
        + [pltpu.VMEM((slots * tm // LANES, LANES), F32)] * 3,
        compiler_params=_params(("parallel",)),
        name="peer_topk",
    )(q3, keys)


def _peer_wbuild_kernel(g_ref, ei_ref, ej_ref, wt_ref, w_sc, *, tm, pitch):
    nt = (((1,), (1,)), ((), ()))
    sub = lax.broadcasted_iota(jnp.int32, (N_KEYS, LANES), 0).astype(F32)

    def tok_body(t, carry):
        g_r = g_ref[pl.ds(t, 1), :]
        at = jnp.where(sub == ei_ref[pl.ds(t, 1), :], g_r, 0.0).astype(BF16)
        bt = jnp.where(sub == ej_ref[pl.ds(t, 1), :], 1.0, 0.0).astype(BF16)
        w_t = lax.dot_general(at, bt, nt, preferred_element_type=F32)
        w_sc[pl.ds(t, N_KEYS, stride=pitch), :] = w_t
        return carry

    lax.fori_loop(0, tm, tok_body, 0, unroll=32)

    for i in range(N_KEYS):
        wt_ref[i] = w_sc[i * pitch:i * pitch + tm, :].astype(BF16)


def _peer_wbuild(g, ei, ej):
    n = g.shape[0]
    tm = _tile(n, 128)
    pitch = tm + 8 if (tm // 8) % 2 == 0 else tm
    row = pl.BlockSpec((tm, PEER_HEADS * PEER_TOPK), lambda i: (i, 0))
    return pl.pallas_call(
        functools.partial(_peer_wbuild_kernel, tm=tm, pitch=pitch),
        grid=(n // tm,),
        in_specs=[row, row, row],
        out_specs=pl.BlockSpec((N_KEYS, tm, N_KEYS), lambda i: (0, i, 0)),
        out_shape=jax.ShapeDtypeStruct((N_KEYS, n, N_KEYS), BF16),
        scratch_shapes=[pltpu.VMEM((N_KEYS * pitch, N_KEYS), F32)],
        compiler_params=_params(("parallel",)),
        name="peer_wbuild",
    )(g, ei, ej)


def _peer_dense_kernel(xn_ref, ut_ref, wt_ref, v_ref, x_ref, o_ref, acc_sc, *, ce):
    c = pl.program_id(1)

    @pl.when(c == 0)
    def _():
        acc_sc[...] = jnp.zeros(acc_sc.shape, F32)

    a = jnp.dot(xn_ref[...], ut_ref[...], preferred_element_type=F32)
    act = 0.5 * a * (1.0 + lax.erf(a * math.sqrt(0.5)))
    h = jnp.concatenate([(wt_ref[s].astype(F32) * act[:, s * LANES:(s + 1) * LANES]).astype(BF16)
                         for s in range(ce // LANES)], axis=1)
    acc_sc[...] += jnp.dot(h, v_ref[...], preferred_element_type=F32)

    @pl.when(c == pl.num_programs(1) - 1)
    def _():
        o_ref[...] = x_ref[...] + acc_sc[...]


def _peer_dense(xn, ut, wt, v, x2d):
    n = x2d.shape[0]
    tm = _tile(n, 512)
    ce = 1024
    kern = functools.partial(_peer_dense_kernel, ce=ce)
    return pl.pallas_call(
        kern,
        grid=(n // tm, N_EXPERTS // ce),
        in_specs=[
            pl.BlockSpec((tm, D_MODEL), lambda i, c: (i, 0)),
            pl.BlockSpec((D_MODEL, ce), lambda i, c: (0, c)),
            pl.BlockSpec((ce // N_KEYS, tm, N_KEYS), lambda i, c: (c, i, 0)),
            pl.BlockSpec((ce, D_MODEL), lambda i, c: (c, 0)),
            pl.BlockSpec((tm, D_MODEL), lambda i, c: (i, 0)),
        ],
        out_specs=pl.BlockSpec((tm, D_MODEL), lambda i, c: (i, 0)),
        out_shape=jax.ShapeDtypeStruct((n, D_MODEL), F32),
        scratch_shapes=[pltpu.VMEM((tm, D_MODEL), F32)],
        compiler_params=_params(("parallel", "arbitrary")),
        name="peer_dense",
    )(xn, ut, wt, v, x2d)


def _rope_tiles(seq):
    inv = 1.0 / (ROPE_THETA ** (jnp.arange(0, DA_DH, 2, dtype=F32) / DA_DH))
    ang = jnp.arange(seq, dtype=F32)[:, None] * inv[None, :]
    cos, sin = jnp.cos(ang), jnp.sin(ang)
    cos_t = jnp.tile(cos, (1, 4))
    sign = jnp.tile(jnp.concatenate([-jnp.ones((32,), F32), jnp.ones((32,), F32)]), 2)
    sin_t = jnp.tile(sin, (1, 4)) * sign[None, :]
    return cos_t, sin_t


def _prep_layer(l, mix_norm, w_in, da_q_norm, da_k_norm, da_lambda, da_out_norm, wa_q_norm, wa_k_norm,
                wa_sink, mem_norm, w_mem_kv, ma_q_norm, ma_k_norm, w_branch, w_out, ffn_norm,
                peer_wq, peer_keys, peer_u, peer_v):
    w = w_in[l]
    pad = jnp.zeros((D_MODEL, 256), F32)
    w_r = jnp.concatenate([
        w[:, DA_Q_OFF:DA_V_OFF],
        w[:, WA_Q_OFF:MA_Q_OFF], pad,
        w[:, MA_Q_OFF:GATE_OFF], w[:, DA_V_OFF:WA_Q_OFF],
        w[:, GATE_OFF:]], axis=1).astype(BF16)
    t2 = lambda g: jnp.tile(g.astype(F32), 2)
    head_gains = jnp.stack([
        t2(da_q_norm[l]) * (DA_DH ** -0.5 * math.log2(math.e)), t2(da_k_norm[l]),
        t2(wa_q_norm[l]) * (WA_DH ** -0.5 * math.log2(math.e)), t2(wa_k_norm[l]),
        ma_q_norm[l].astype(F32) * MA_DH ** -0.5,
        jnp.zeros((LANES,), F32), jnp.zeros((LANES,), F32), jnp.zeros((LANES,), F32)])
    return dict(
        mixg=mix_norm[l][None, :], w_r=w_r, head_gains=head_gains,
        da_lambda=da_lambda[l], og=da_out_norm[l][None, :],
        sink=jnp.broadcast_to(wa_sink[l].astype(F32)[:, None], (WA_HEADS, LANES)),
        mem_g=mem_norm[l][None, :], w_kv=w_mem_kv[l].astype(BF16), ma_kg=ma_k_norm[l][None, :],
        wb=w_branch[l].astype(BF16), wo=w_out[l].astype(BF16),
        ffn_g=ffn_norm[l][None, :], peer_wq=peer_wq[l].astype(BF16),
        keys=peer_keys[l].reshape(2 * PEER_HEADS, N_KEYS, PEER_HALF).astype(BF16),
        ut=peer_u[l].astype(BF16).T, v=peer_v[l].astype(BF16),
    )


def _trunk(x, mem, layers):
    nb, seq, _ = x.shape
    mlen = mem.shape[1]
    x2d = x.reshape(nb * seq, D_MODEL)
    mem2d = mem.reshape(nb * mlen, D_MODEL)
    cos_t, sin_t = _rope_tiles(seq)
    for l, p in enumerate(layers):
        lam_init = 0.8 - 0.6 * math.exp(-0.3 * l)
        dq, dk, dv, wq, wkd, wvd, mq, gates = _in_proj(x2d, seq, p["mixg"], p["w_r"], cos_t, sin_t,
                                                      p["head_gains"])
        oda = _diff_attn(dq, dk, dv, p["da_lambda"], p["og"], nb, seq, lam_init)
        owa = _window_attn(wq, wkd, wvd, p["sink"], nb, seq)
        mk, mv = _mem_kv(mem2d, mlen, p["mem_g"], p["w_kv"], p["ma_kg"])
        oma = _mem_attn(mq, mk, mv, nb, seq, mlen)
        x1 = _merge(x2d, oda, owa, oma, gates, p["wb"], p["wo"])
        xn, q3 = _peer_q(x1, p["ffn_g"], p["peer_wq"])
        wt = _peer_wbuild(*_peer_topk(q3, p["keys"]))
        x2d = _peer_dense(xn, p["ut"], wt, p["v"], x1)
    return x2d.reshape(nb, seq, D_MODEL)


def kernel(x_prompt, x_sample, mem_prompt, mem_sample, mix_norm, w_in, da_q_norm, da_k_norm, da_lambda, da_out_norm, wa_q_norm, wa_k_norm, wa_sink, mem_norm, w_mem_kv, ma_q_norm, ma_k_norm, w_branch, w_out, ffn_norm, peer_wq, peer_keys, peer_u, peer_v):
    weights = (mix_norm, w_in, da_q_norm, da_k_norm, da_lambda, da_out_norm, wa_q_norm, wa_k_norm,
               wa_sink, mem_norm, w_mem_kv, ma_q_norm, ma_k_norm, w_branch, w_out, ffn_norm,
               peer_wq, peer_keys, peer_u, peer_v)
    layers = [_prep_layer(l, *weights) for l in range(w_in.shape[0])]
    y_prompt = _trunk(x_prompt, mem_prompt, layers)
    y_sample = _trunk(x_sample, mem_sample, layers)
    return (y_prompt, y_sample)
```

```python
import functools
import math

import jax
import jax.numpy as jnp
import numpy as np
from jax import lax
from jax.experimental import pallas as pl
from jax.experimental.pallas import tpu as pltpu

F32 = jnp.float32
BF16 = jnp.bfloat16

D_MODEL = 1024
NORM_EPS = 1e-6
NEG_INF = -1e30
ROPE_THETA = 10000.0
LANES = 128

DA_HEADS, DA_DH = 4, 64
WA_HEADS, WA_KV_HEADS, WA_DH, WINDOW = 8, 2, 64, 128
MA_HEADS, MA_DH = 4, 128
N_BRANCH, BRANCH_WIDTH = 3, 512
PEER_HEADS, N_KEYS, PEER_HALF, PEER_TOPK = 8, 128, 128, 16
N_EXPERTS = N_KEYS * N_KEYS

DA_Q_OFF = 0
DA_K_OFF = 512
DA_V_OFF = 1024
WA_Q_OFF = 1536
WA_K_OFF = 2048
WA_V_OFF = 2176
MA_Q_OFF = 2304
GATE_OFF = 2816
IN_CHUNK = 1024
N_IN_CHUNKS = 6

VMEM_LIMIT = 48 * 1024 * 1024
DIFF_TQ, DIFF_TK = 512, 1024
DENSE_CE = 2048


def _tile(n, pref):
    t = min(pref, n)
    t -= t % LANES
    while n % t:
        t -= LANES
    return t


def _params(sem):
    return pltpu.CompilerParams(dimension_semantics=sem, vmem_limit_bytes=VMEM_LIMIT)


def _lane_iota(shape):
    return lax.broadcasted_iota(jnp.int32, shape, len(shape) - 1)


def _rms(xf, gain_row, width):
    ms = jnp.sum(xf * xf, axis=-1, keepdims=True) * (1.0 / width)
    return xf * lax.rsqrt(ms + NORM_EPS) * gain_row


def _seg64_norm_rope(zt, gain_row, cosb, sinb):
    lane = _lane_iota(zt.shape)
    lo = lane < 64
    sq = zt * zt
    s0 = jnp.sum(jnp.where(lo, sq, 0.0), axis=-1, keepdims=True)
    s1 = jnp.sum(jnp.where(lo, 0.0, sq), axis=-1, keepdims=True)
    ms = jnp.where(lo, s0, s1) * (1.0 / 64)
    y = zt * lax.rsqrt(ms + NORM_EPS) * gain_row
    first = (lane % 64) < 32
    rot = jnp.where(first, pltpu.roll(y, 96, 1), pltpu.roll(y, 32, 1))
    return y * cosb + rot * sinb


def _in_proj_kernel(x_ref, mixg_ref, w_ref, cos_ref, sin_ref, hg_ref,
                    dq_ref, dk_ref, dv_ref, wq_ref, wkd_ref, wvd_ref, mq_ref, gate_ref, xn_sc):
    j = pl.program_id(1)

    @pl.when(j == 0)
    def _():
        xn_sc[...] = _rms(x_ref[...], mixg_ref[...], D_MODEL).astype(BF16)

    z = jnp.dot(xn_sc[...], w_ref[...], preferred_element_type=F32)
    cosb = cos_ref[...]
    sinb = sin_ref[...]

    @pl.when(j == 0)
    def _():
        for t in range(4):
            sl = slice(t * LANES, (t + 1) * LANES)
            dq_ref[:, sl] = _seg64_norm_rope(z[:, sl], hg_ref[0:1, :], cosb, sinb).astype(BF16)
            zs = z[:, 512 + t * LANES:512 + (t + 1) * LANES]
            dk_ref[:, sl] = _seg64_norm_rope(zs, hg_ref[1:2, :], cosb, sinb).astype(BF16)

    @pl.when(j == 1)
    def _():
        for t in range(4):
            sl = slice(t * LANES, (t + 1) * LANES)
            wq_ref[:, sl] = _seg64_norm_rope(z[:, sl], hg_ref[2:3, :], cosb, sinb).astype(BF16)
        k = _seg64_norm_rope(z[:, 512:640], hg_ref[3:4, :], cosb, sinb)
        v = z[:, 640:768]
        lo = _lane_iota(k.shape) < 64
        ones = jnp.ones(k.shape, BF16)
        for src, dst, stride in ((k, wkd_ref, 1), (v, wvd_ref, 2)):
            sw = pltpu.roll(src, 64, 1)
            dst[:, 0:LANES] = jnp.where(lo, src, sw).astype(BF16)
            dst[:, stride * LANES:(stride + 1) * LANES] = jnp.where(lo, sw, src).astype(BF16)
        wvd_ref[:, LANES:2 * LANES] = ones
        wvd_ref[:, 3 * LANES:4 * LANES] = ones

    @pl.when(j == 2)
    def _():
        for t in range(4):
            sl = slice(t * LANES, (t + 1) * LANES)
            mq_ref[:, sl] = _rms(z[:, sl], hg_ref[4:5, :], MA_DH).astype(BF16)
        ones = jnp.ones((z.shape[0], LANES), BF16)
        for t in range(DA_HEADS):
            dv_ref[:, 2 * t * LANES:(2 * t + 1) * LANES] = z[:, 512 + t * LANES:512 + (t + 1) * LANES].astype(BF16)
            dv_ref[:, (2 * t + 1) * LANES:(2 * t + 2) * LANES] = ones

    @pl.when(j >= 3)
    def _():
        gate_ref[...] = jax.nn.sigmoid(z)


def _in_proj(x2d, seq, mixg, w_r, cos_t, sin_t, head_gains):
    n = x2d.shape[0]
    tm = _tile(seq, 1024)
    nsb = seq // tm
    bf = lambda w: jax.ShapeDtypeStruct((n, w), BF16)
    row = lambda w: pl.BlockSpec((tm, w), lambda i, j: (i, 0))
    return pl.pallas_call(
        _in_proj_kernel,
        grid=(n // tm, N_IN_CHUNKS),
        in_specs=[
            pl.BlockSpec((tm, D_MODEL), lambda i, j: (i, 0)),
            pl.BlockSpec((1, D_MODEL), lambda i, j: (0, 0)),
            pl.BlockSpec((D_MODEL, IN_CHUNK), lambda i, j: (0, j)),
            pl.BlockSpec((tm, LANES), lambda i, j: (i % nsb, 0)),
            pl.BlockSpec((tm, LANES), lambda i, j: (i % nsb, 0)),
            pl.BlockSpec((8, LANES), lambda i, j: (0, 0)),
        ],
        out_specs=[row(512), row(512), row(1024), row(512), row(256), row(512), row(512),
                   pl.BlockSpec((tm, IN_CHUNK), lambda i, j: (i, jnp.maximum(j - 3, 0)))],
        out_shape=[bf(512), bf(512), bf(1024), bf(512), bf(256), bf(512), bf(512),
                   jax.ShapeDtypeStruct((n, N_BRANCH * D_MODEL), F32)],
        scratch_shapes=[pltpu.VMEM((tm, D_MODEL), BF16)],
        compiler_params=_params(("parallel", "arbitrary")),
        name="in_proj",
    )(x2d, mixg, w_r, cos_t, sin_t, head_gains)


def _diff_attn_kernel(q_ref, k_ref, v_ref, lam_ref, og_ref, o_ref, m_sc, acc_sc, s_sc, *, tk, lam_init):
    q = q_ref[...]
    lo = _lane_iota(q.shape) < 64
    zero = jnp.zeros_like(q)
    qs = (jnp.where(lo, q, zero), jnp.where(lo, zero, q))
    m_sc[...] = jnp.full(m_sc.shape, -jnp.inf, F32)
    acc_sc[...] = jnp.zeros(acc_sc.shape, F32)
    nt = (((1,), (1,)), ((), ()))
    n_chunks = k_ref.shape[0] // tk

    def scores(j, slot):
        off = pl.multiple_of(j * tk, tk)
        k = k_ref[pl.ds(off, tk), :]
        for c in range(2):
            s_sc[slot, c] = lax.dot_general(qs[c], k, nt, preferred_element_type=F32)

    def process(j, slot):
        off = pl.multiple_of(j * tk, tk)
        v = v_ref[pl.ds(off, tk), :]
        for c in range(2):
            s = s_sc[slot, c]
            m_old = m_sc[c]
            m_new = jnp.maximum(m_old, jnp.max(s, axis=-1, keepdims=True))
            p = jnp.exp2(s - m_new)
            alpha = jnp.exp2(m_old - m_new)
            acc_sc[c] = alpha * acc_sc[c] + jnp.dot(p.astype(BF16), v, preferred_element_type=F32)
            m_sc[c] = m_new

    scores(0, 0)
    if n_chunks == 1:
        process(0, 0)
    else:
        def pair(jj, carry):
            scores(2 * jj + 1, 1)
            process(2 * jj, 0)
            scores(2 * jj + 2, 0)
            process(2 * jj + 1, 1)
            return carry

        lax.fori_loop(0, n_chunks // 2 - 1, pair, 0)
        scores(n_chunks - 1, 1)
        process(n_chunks - 2, 0)
        process(n_chunks - 1, 1)

    lp = lam_ref[...]
    lam = (jnp.exp(jnp.sum(lp[0:1] * lp[1:2], axis=-1, keepdims=True))
           - jnp.exp(jnp.sum(lp[2:3] * lp[3:4], axis=-1, keepdims=True)) + lam_init)
    o = (acc_sc[0, :, :LANES] / acc_sc[0, :, LANES:]
         - lam * (acc_sc[1, :, :LANES] / acc_sc[1, :, LANES:]))
    o_ref[...] = (_rms(o, og_ref[...], 2 * DA_DH) * (1.0 - lam_init)).astype(BF16)


def _diff_attn(dq, dk, dv, da_lambda, og, nb, seq, lam_init):
    n = dq.shape[0]
    tq = _tile(seq, DIFF_TQ)
    tk = _tile(seq, DIFF_TK)
    nq = seq // tq
    assert seq // tk == 1 or (seq // tk) % 2 == 0, (seq, tk)
    kern = functools.partial(_diff_attn_kernel, tk=tk, lam_init=lam_init)
    return pl.pallas_call(
        kern,
        grid=(nb, DA_HEADS, nq),
        in_specs=[
            pl.BlockSpec((tq, LANES), lambda b, h, i: (b * nq + i, h)),
            pl.BlockSpec((seq, LANES), lambda b, h, i: (b, h)),
            pl.BlockSpec((seq, 2 * LANES), lambda b, h, i: (b, h)),
            pl.BlockSpec((4, DA_DH), lambda b, h, i: (0, 0)),
            pl.BlockSpec((1, LANES), lambda b, h, i: (0, 0)),
        ],
        out_specs=pl.BlockSpec((tq, LANES), lambda b, h, i: (b * nq + i, h)),
        out_shape=jax.ShapeDtypeStruct((n, BRANCH_WIDTH), BF16),
        scratch_shapes=[pltpu.VMEM((2, tq, 1), F32), pltpu.VMEM((2, tq, 2 * LANES), F32),
                        pltpu.VMEM((2, 2, tq, tk), F32)],
        compiler_params=_params(("parallel", "parallel", "arbitrary")),
        name="diff_attn",
    )(dq, dk, dv, da_lambda, og)


def _window_attn_kernel(q_ref, kd_ref, vd_ref, sink_ref, o_ref, *, tq, seq):
    i = pl.program_id(1)
    nsub = tq // WINDOW
    kw_len = 3 * WINDOW
    nt = (((1,), (1,)), ((), ()))
    rep = WA_HEADS // WA_KV_HEADS
    lo = _lane_iota((WINDOW, LANES)) < 64
    log2e = math.log2(math.e)
    for n in range(nsub):
        gb = i * nsub + n
        start = pl.multiple_of(jnp.clip((gb - 1) * WINDOW, 0, seq - kw_len), WINDOW)
        qpos = gb * WINDOW + lax.broadcasted_iota(jnp.int32, (WINDOW, kw_len), 0)
        kpos = start + lax.broadcasted_iota(jnp.int32, (WINDOW, kw_len), 1)
        valid1 = jnp.abs(qpos - kpos) <= WINDOW
        valid = jnp.concatenate([valid1] * rep, axis=0)
        for g in range(WA_KV_HEADS):
            kg = kd_ref[pl.ds(start, kw_len), g * LANES:(g + 1) * LANES]
            vg = vd_ref[pl.ds(start, kw_len), 2 * g * LANES:2 * (g + 1) * LANES]
            tiles, sinks = [], []
            for e in range(rep):
                h = g * rep + e
                qp = q_ref[n * WINDOW:(n + 1) * WINDOW, (h // 2) * LANES:(h // 2 + 1) * LANES]
                zero = jnp.zeros_like(qp)
                tiles.append(jnp.where(lo, qp, zero) if h % 2 == 0 else jnp.where(lo, zero, qp))
                sinks.append(jnp.broadcast_to(sink_ref[h:h + 1, 0:1] * log2e, (WINDOW, 1)))
            qs = jnp.concatenate(tiles, axis=0)
            sk = jnp.concatenate(sinks, axis=0)
            s = lax.dot_general(qs, kg, nt, preferred_element_type=F32)
            s = jnp.where(valid, s, NEG_INF)
            m = jnp.maximum(jnp.max(s, axis=-1, keepdims=True), sk)
            r = jnp.dot(jnp.exp2(s - m).astype(BF16), vg, preferred_element_type=F32)
            o = r[:, :LANES] / (r[:, LANES:] + jnp.exp2(sk - m))
            for pr in range(rep // 2):
                col = (g * rep // 2 + pr) * LANES
                o_ref[n * WINDOW:(n + 1) * WINDOW, col:col + LANES] = jnp.where(
                    lo, o[2 * pr * WINDOW:(2 * pr + 1) * WINDOW], o[(2 * pr + 1) * WINDOW:(2 * pr + 2) * WINDOW]
                ).astype(BF16)


def _window_attn(wq, wkd, wvd, sink_b, nb, seq):
    n = wq.shape[0]
    tq = _tile(seq, 512)
    nq = seq // tq
    kern = functools.partial(_window_attn_kernel, tq=tq, seq=seq)
    return pl.pallas_call(
        kern,
        grid=(nb, nq),
        in_specs=[
            pl.BlockSpec((tq, 512), lambda b, i: (b * nq + i, 0)),
            pl.BlockSpec((seq, 256), lambda b, i: (b, 0)),
            pl.BlockSpec((seq, 512), lambda b, i: (b, 0)),
            pl.BlockSpec((8, LANES), lambda b, i: (0, 0)),
        ],
        out_specs=pl.BlockSpec((tq, 512), lambda b, i: (b * nq + i, 0)),
        out_shape=jax.ShapeDtypeStruct((n, BRANCH_WIDTH), BF16),
        compiler_params=_params(("parallel", "arbitrary")),
        name="window_attn",
    )(wq, wkd, wvd, sink_b)


def _mem_kv_kernel(mem_ref, mg_ref, w_ref, kg_ref, mk_ref, mv_ref):
    h = _rms(mem_ref[...], mg_ref[...], D_MODEL).astype(BF16)
    kv = jnp.dot(h, w_ref[...], preferred_element_type=F32)
    for t in range(MA_HEADS):
        sl = slice(t * LANES, (t + 1) * LANES)
        mk_ref[:, sl] = _rms(kv[:, sl], kg_ref[...], MA_DH).astype(BF16)
    mv_ref[...] = kv[:, 512:1024].astype(BF16)


def _mem_kv(mem2d, mlen, mg, w_kv, kg):
    n = mem2d.shape[0]
    return pl.pallas_call(
        _mem_kv_kernel,
        grid=(n // mlen,),
        in_specs=[
            pl.BlockSpec((mlen, D_MODEL), lambda b: (b, 0)),
            pl.BlockSpec((1, D_MODEL), lambda b: (0, 0)),
            pl.BlockSpec((D_MODEL, 2 * MA_HEADS * MA_DH), lambda b: (0, 0)),
            pl.BlockSpec((1, LANES), lambda b: (0, 0)),
        ],
        out_specs=[pl.BlockSpec((mlen, 512), lambda b: (b, 0)), pl.BlockSpec((mlen, 512), lambda b: (b, 0))],
        out_shape=[jax.ShapeDtypeStruct((n, 512), BF16), jax.ShapeDtypeStruct((n, 512), BF16)],
        compiler_params=_params(("parallel",)),
        name="mem_kv",
    )(mem2d, mg, w_kv, kg)


def _mem_attn_kernel(q_ref, mk_ref, mv_ref, o_ref):
    nt = (((1,), (1,)), ((), ()))
    for t in range(MA_HEADS):
        sl = slice(t * LANES, (t + 1) * LANES)
        s = lax.dot_general(q_ref[:, sl], mk_ref[:, sl], nt, preferred_element_type=F32)
        m = jnp.max(s, axis=-1, keepdims=True)
        ex = jnp.exp(s - m)
        p = ex / jnp.sum(ex, axis=-1, keepdims=True)
        o_ref[:, sl] = jnp.dot(p.astype(BF16), mv_ref[:, sl], preferred_element_type=F32).astype(BF16)


def _mem_attn(mq, mk, mv, nb, seq, mlen):
    n = mq.shape[0]
    tq = _tile(seq, 1024)
    nq = seq // tq
    return pl.pallas_call(
        _mem_attn_kernel,
        grid=(nb, nq),
        in_specs=[
            pl.BlockSpec((tq, 512), lambda b, i: (b * nq + i, 0)),
            pl.BlockSpec((mlen, 512), lambda b, i: (b, 0)),
            pl.BlockSpec((mlen, 512), lambda b, i: (b, 0)),
        ],
        out_specs=pl.BlockSpec((tq, 512), lambda b, i: (b * nq + i, 0)),
        out_shape=jax.ShapeDtypeStruct((n, BRANCH_WIDTH), BF16),
        compiler_params=_params(("parallel", "arbitrary")),
        name="mem_attn",
    )(mq, mk, mv)


def _merge_kernel(x_ref, oda_ref, owa_ref, oma_ref, gate_ref, wb_ref, wo_ref, o_ref):
    merged = None
    for nbr, br in enumerate((oda_ref, owa_ref, oma_ref)):
        proj = jnp.dot(br[...], wb_ref[nbr], preferred_element_type=F32)
        term = gate_ref[:, nbr * D_MODEL:(nbr + 1) * D_MODEL] * proj
        merged = term if merged is None else merged + term
    o_ref[...] = x_ref[...] + jnp.dot(merged.astype(BF16), wo_ref[...], preferred_element_type=F32)


def _merge(x2d, oda, owa, oma, gates, wb, wo):
    n = x2d.shape[0]
    tm = _tile(n, 512)
    row = lambda w: pl.BlockSpec((tm, w), lambda i: (i, 0))
    return pl.pallas_call(
        _merge_kernel,
        grid=(n // tm,),
        in_specs=[row(D_MODEL), row(512), row(512), row(512), row(N_BRANCH * D_MODEL),
                  pl.BlockSpec((N_BRANCH, BRANCH_WIDTH, D_MODEL), lambda i: (0, 0, 0)),
                  pl.BlockSpec((D_MODEL, D_MODEL), lambda i: (0, 0))],
        out_specs=row(D_MODEL),
        out_shape=jax.ShapeDtypeStruct((n, D_MODEL), F32),
        compiler_params=_params(("parallel",)),
        name="merge",
    )(x2d, oda, owa, oma, gates, wb, wo)


def _top16_lead(rows, ids):
    rows = list(rows)
    vals, idxs = [], []
    for _ in range(PEER_TOPK):
        level = list(zip(rows, ids))
        while len(level) > 1:
            nxt = []
            for j in range(0, len(level) - 1, 2):
                (va, ia), (vb, ib) = level[j], level[j + 1]
                take = va >= vb
                nxt.append((jnp.maximum(va, vb), jnp.where(take, ia, ib)))
            if len(level) % 2:
                nxt.append(level[-1])
            level = nxt
        m, sel = level[0]
        rows = [jnp.where(sel == i, -jnp.inf, r) for r, i in zip(rows, ids)]
        vals.append(m)
        idxs.append(sel)
    return vals, idxs


def _peer_q_kernel(x_ref, fg_ref, wq_ref, xn_ref, q_ref):
    xn = _rms(x_ref[...], fg_ref[...], D_MODEL).astype(BF16)
    xn_ref[...] = xn
    q = jnp.dot(xn, wq_ref[...], preferred_element_type=F32).astype(BF16)
    for hp in range(2 * PEER_HEADS):
        q_ref[hp] = q[:, hp * LANES:(hp + 1) * LANES]


def _peer_q(x2d, fg, wq):
    n = x2d.shape[0]
    tm = _tile(n, 512)
    return pl.pallas_call(
        _peer_q_kernel,
        grid=(n // tm,),
        in_specs=[
            pl.BlockSpec((tm, D_MODEL), lambda i: (i, 0)),
            pl.BlockSpec((1, D_MODEL), lambda i: (0, 0)),
            pl.BlockSpec((D_MODEL, 2 * PEER_HEADS * PEER_HALF), lambda i: (0, 0)),
        ],
        out_specs=[pl.BlockSpec((tm, D_MODEL), lambda i: (i, 0)),
                   pl.BlockSpec((2 * PEER_HEADS, tm, PEER_HALF), lambda i: (0, i, 0))],
        out_shape=[jax.ShapeDtypeStruct((n, D_MODEL), BF16),
                   jax.ShapeDtypeStruct((2 * PEER_HEADS, n, PEER_HALF), BF16)],
        compiler_params=_params(("parallel",)),
        name="peer_q",
    )(x2d, fg, wq)


_CAND_PAIRS = tuple((a, b) for a in range(PEER_TOPK) for b in range(PEER_TOPK) if (a + 1) * (b + 1) <= PEER_TOPK)


def _peer_topk_kernel(q_ref, keys_ref, g_ref, ei_ref, ej_ref, s0_sc, s1_sc, rg_sc, ri_sc, rj_sc, *, tm):
    nt = (((1,), (1,)), ((), ()))
    gt = tm // LANES
    key_ids = [float(k) for k in range(N_KEYS)]
    cand_ids = [float(a * PEER_TOPK + b) for a, b in _CAND_PAIRS]

    def head_body(h, carry):
        sv, si = [], []
        for p, s_sc in enumerate((s0_sc, s1_sc)):
            s = lax.dot_general(keys_ref[2 * h + p], q_ref[2 * h + p], nt,
                                preferred_element_type=F32)
            for tt in range(gt):
                s_sc[pl.ds(tt, N_KEYS, stride=gt), :] = s[:, tt * LANES:(tt + 1) * LANES]
            v, ix = _top16_lead([s_sc[k * gt:(k + 1) * gt, :] for k in range(N_KEYS)], key_ids)
            sv.append(v)
            si.append(ix)
        fv, fi = _top16_lead([sv[0][a] + sv[1][b] for a, b in _CAND_PAIRS], cand_ids)
        ex = [jnp.exp(v - fv[0]) for v in fv]
        den = ex[0]
        for e in ex[1:]:
            den = den + e
        for k in range(PEER_TOPK):
            fik = fi[k].astype(jnp.int32)
            fa = fik >> 4
            fb = fik & (PEER_TOPK - 1)
            ei = jnp.zeros((gt, LANES), F32)
            ej = jnp.zeros((gt, LANES), F32)
            for a in range(PEER_TOPK):
                ei = jnp.where(fa == a, si[0][a], ei)
                ej = jnp.where(fb == a, si[1][a], ej)
            off = pl.multiple_of((h * PEER_TOPK + k) * gt, gt)
            rg_sc[pl.ds(off, gt), :] = ex[k] / den
            ri_sc[pl.ds(off, gt), :] = ei
            rj_sc[pl.ds(off, gt), :] = ej
        return carry

    lax.fori_loop(0, PEER_HEADS, head_body, 0)

    for tt in range(gt):
        sl = slice(tt * LANES, (tt + 1) * LANES)
        g_ref[sl, :] = rg_sc[pl.ds(tt, N_KEYS, stride=gt), :].T
        ei_ref[sl, :] = ri_sc[pl.ds(tt, N_KEYS, stride=gt), :].T
        ej_ref[sl, :] = rj_sc[pl.ds(tt, N_KEYS, stride=gt), :].T


def _peer_topk(q3, keys):
    n = q3.shape[1]
    tm = _tile(n, 8 * LANES)
    slots = PEER_HEADS * PEER_TOPK
    out = jax.ShapeDtypeStruct((n, slots), F32)
    return pl.pallas_call(
        functools.partial(_peer_topk_kernel, tm=tm),
        grid=(n // tm,),
        in_specs=[
            pl.BlockSpec((2 * PEER_HEADS, tm, PEER_HALF), lambda i: (0, i, 0)),
            pl.BlockSpec((2 * PEER_HEADS, N_KEYS, PEER_HALF), lambda i: (0, 0, 0)),
        ],
        out_specs=[pl.BlockSpec((tm, slots), lambda i: (i, 0))] * 3,
        out_shape=[out, out, out],
        scratch_shapes=[pltpu.VMEM((N_KEYS * tm // LANES, LANES), F32)] * 2
        + [pltpu.VMEM((slots * tm // LANES, LANES), F32)] * 3,
        compiler_params=_params(("parallel",)),
        name="peer_topk",
    )(q3, keys)


def _peer_wbuild_kernel(g_ref, ei_ref, ej_ref, wt_ref, w_sc, *, tm, pitch):
    nt = (((1,), (1,)), ((), ()))
    sub = lax.broadcasted_iota(jnp.int32, (N_KEYS, LANES), 0).astype(F32)

    def tok_body(t, carry):
        g_r = g_ref[pl.ds(t, 1), :]
        at = jnp.where(sub == ei_ref[pl.ds(t, 1), :], g_r, 0.0).astype(BF16)
        bt = jnp.where(sub == ej_ref[pl.ds(t, 1), :], 1.0, 0.0).astype(BF16)
        w_t = lax.dot_general(at, bt, nt, preferred_element_type=F32)
        w_sc[pl.ds(t, N_KEYS, stride=pitch), :] = w_t
        return carry

    lax.fori_loop(0, tm, tok_body, 0, unroll=True)

    for i in range(N_KEYS):
        wt_ref[i] = w_sc[i * pitch:i * pitch + tm, :].astype(BF16)


def _peer_wbuild(g, ei, ej):
    n = g.shape[0]
    tm = _tile(n, 128)
    pitch = tm + 4
    row = pl.BlockSpec((tm, PEER_HEADS * PEER_TOPK), lambda i: (i, 0))
    return pl.pallas_call(
        functools.partial(_peer_wbuild_kernel, tm=tm, pitch=pitch),
        grid=(n // tm,),
        in_specs=[row, row, row],
        out_specs=pl.BlockSpec((N_KEYS, tm, N_KEYS), lambda i: (0, i, 0)),
        out_shape=jax.ShapeDtypeStruct((N_KEYS, n, N_KEYS), BF16),
        scratch_shapes=[pltpu.VMEM((N_KEYS * pitch, N_KEYS), F32)],
        compiler_params=_params(("parallel",)),
        name="peer_wbuild",
    )(g, ei, ej)


def _peer_dense_kernel(xn_ref, ut_ref, wt_ref, v_ref, x_ref, o_ref, acc_sc, *, ce):
    c = pl.program_id(1)

    @pl.when(c == 0)
    def _():
        acc_sc[...] = jnp.zeros(acc_sc.shape, F32)

    a = jnp.dot(xn_ref[...], ut_ref[...], preferred_element_type=F32)
    act = 0.5 * a * (1.0 + lax.erf(a * math.sqrt(0.5)))
    h = jnp.concatenate([(wt_ref[s].astype(F32) * act[:, s * LANES:(s + 1) * LANES]).astype(BF16)
                         for s in range(ce // LANES)], axis=1)
    acc_sc[...] += jnp.dot(h, v_ref[...], preferred_element_type=F32)

    @pl.when(c == pl.num_programs(1) - 1)
    def _():
        o_ref[...] = x_ref[...] + acc_sc[...]


def _peer_dense(xn, ut, wt, v, x2d):
    n = x2d.shape[0]
    tm = _tile(n, 512)
    ce = DENSE_CE
    kern = functools.partial(_peer_dense_kernel, ce=ce)
    return pl.pallas_call(
        kern,
        grid=(n // tm, N_EXPERTS // ce),
        in_specs=[
            pl.BlockSpec((tm, D_MODEL), lambda i, c: (i, 0)),
            pl.BlockSpec((D_MODEL, ce), lambda i, c: (0, c)),
            pl.BlockSpec((ce // N_KEYS, tm, N_KEYS), lambda i, c: (c, i, 0)),
            pl.BlockSpec((ce, D_MODEL), lambda i, c: (c, 0)),
            pl.BlockSpec((tm, D_MODEL), lambda i, c: (i, 0)),
        ],
        out_specs=pl.BlockSpec((tm, D_MODEL), lambda i, c: (i, 0)),
        out_shape=jax.ShapeDtypeStruct((n, D_MODEL), F32),
        scratch_shapes=[pltpu.VMEM((tm, D_MODEL), F32)],
        compiler_params=_params(("parallel", "arbitrary")),
        name="peer_dense",
    )(xn, ut, wt, v, x2d)


def _rope_tiles(seq):
    inv = 1.0 / (ROPE_THETA ** (jnp.arange(0, DA_DH, 2, dtype=F32) / DA_DH))
    ang = jnp.arange(seq, dtype=F32)[:, None] * inv[None, :]
    cos, sin = jnp.cos(ang), jnp.sin(ang)
    cos_t = jnp.tile(cos, (1, 4))
    sign = jnp.tile(jnp.concatenate([-jnp.ones((32,), F32), jnp.ones((32,), F32)]), 2)
    sin_t = jnp.tile(sin, (1, 4)) * sign[None, :]
    return cos_t, sin_t


def _prep_layer(l, mix_norm, w_in, da_q_norm, da_k_norm, da_lambda, da_out_norm, wa_q_norm, wa_k_norm,
                wa_sink, mem_norm, w_mem_kv, ma_q_norm, ma_k_norm, w_branch, w_out, ffn_norm,
                peer_wq, peer_keys, peer_u, peer_v):
    w = w_in[l]
    pad = jnp.zeros((D_MODEL, 256), F32)
    w_r = jnp.concatenate([
        w[:, DA_Q_OFF:DA_V_OFF],
        w[:, WA_Q_OFF:MA_Q_OFF], pad,
        w[:, MA_Q_OFF:GATE_OFF], w[:, DA_V_OFF:WA_Q_OFF],
        w[:, GATE_OFF:]], axis=1).astype(BF16)
    t2 = lambda g: jnp.tile(g.astype(F32), 2)
    head_gains = jnp.stack([
        t2(da_q_norm[l]) * (DA_DH ** -0.5 * math.log2(math.e)), t2(da_k_norm[l]),
        t2(wa_q_norm[l]) * (WA_DH ** -0.5 * math.log2(math.e)), t2(wa_k_norm[l]),
        ma_q_norm[l].astype(F32) * MA_DH ** -0.5,
        jnp.zeros((LANES,), F32), jnp.zeros((LANES,), F32), jnp.zeros((LANES,), F32)])
    return dict(
        mixg=mix_norm[l][None, :], w_r=w_r, head_gains=head_gains,
        da_lambda=da_lambda[l], og=da_out_norm[l][None, :],
        sink=jnp.broadcast_to(wa_sink[l].astype(F32)[:, None], (WA_HEADS, LANES)),
        mem_g=mem_norm[l][None, :], w_kv=w_mem_kv[l].astype(BF16), ma_kg=ma_k_norm[l][None, :],
        wb=w_branch[l].astype(BF16), wo=w_out[l].astype(BF16),
        ffn_g=ffn_norm[l][None, :], peer_wq=peer_wq[l].astype(BF16),
        keys=peer_keys[l].reshape(2 * PEER_HEADS, N_KEYS, PEER_HALF).astype(BF16),
        ut=peer_u[l].astype(BF16).T, v=peer_v[l].astype(BF16),
    )


def _trunk(x, mem, layers):
    nb, seq, _ = x.shape
    mlen = mem.shape[1]
    x2d = x.reshape(nb * seq, D_MODEL)
    mem2d = mem.reshape(nb * mlen, D_MODEL)
    cos_t, sin_t = _rope_tiles(seq)
    for l, p in enumerate(layers):
        lam_init = 0.8 - 0.6 * math.exp(-0.3 * l)
        dq, dk, dv, wq, wkd, wvd, mq, gates = _in_proj(x2d, seq, p["mixg"], p["w_r"], cos_t, sin_t,
                                                      p["head_gains"])
        oda = _diff_attn(dq, dk, dv, p["da_lambda"], p["og"], nb, seq, lam_init)
        owa = _window_attn(wq, wkd, wvd, p["sink"], nb, seq)
        mk, mv = _mem_kv(mem2d, mlen, p["mem_g"], p["w_kv"], p["ma_kg"])
        oma = _mem_attn(mq, mk, mv, nb, seq, mlen)
        x1 = _merge(x2d, oda, owa, oma, gates, p["wb"], p["wo"])
        xn, q3 = _peer_q(x1, p["ffn_g"], p["peer_wq"])
        wt = _peer_wbuild(*_peer_topk(q3, p["keys"]))
        x2d = _peer_dense(xn, p["ut"], wt, p["v"], x1)
    return x2d.reshape(nb, seq, D_MODEL)


def kernel(x_prompt, x_sample, mem_prompt, mem_sample, mix_norm, w_in, da_q_norm, da_k_norm, da_lambda, da_out_norm, wa_q_norm, wa_k_norm, wa_sink, mem_norm, w_mem_kv, ma_q_norm, ma_k_norm, w_branch, w_out, ffn_norm, peer_wq, peer_keys, peer_u, peer_v):
    weights = (mix_norm, w_in, da_q_norm, da_k_norm, da_lambda, da_out_norm, wa_q_norm, wa_k_norm,
               wa_sink, mem_norm, w_mem_kv, ma_q_norm, ma_k_norm, w_branch, w_out, ffn_norm,
               peer_wq, peer_keys, peer_u, peer_v)
    layers = [_prep_layer(l, *weights) for l in range(w_in.shape[0])]
    y_prompt = _trunk(x_prompt, mem_prompt, layers)
    y_sample = _trunk(x_sample, mem_sample, layers)
    return (y_prompt, y_sample)
```

```python
import functools
import math

import jax
import jax.numpy as jnp
import numpy as np
from jax import lax
from jax.experimental import pallas as pl
from jax.experimental.pallas import tpu as pltpu

F32 = jnp.float32
BF16 = jnp.bfloat16

D_MODEL = 1024
NORM_EPS = 1e-6
NEG_INF = -1e30
ROPE_THETA = 10000.0
LANES = 128

DA_HEADS, DA_DH = 4, 64
WA_HEADS, WA_KV_HEADS, WA_DH, WINDOW = 8, 2, 64, 128
MA_HEADS, MA_DH = 4, 128
N_BRANCH, BRANCH_WIDTH = 3, 512
PEER_HEADS, N_KEYS, PEER_HALF, PEER_TOPK = 8, 128, 128, 16
N_EXPERTS = N_KEYS * N_KEYS

DA_Q_OFF = 0
DA_K_OFF = 512
DA_V_OFF = 1024
WA_Q_OFF = 1536
WA_K_OFF = 2048
WA_V_OFF = 2176
MA_Q_OFF = 2304
GATE_OFF = 2816
IN_CHUNK = 1024
N_IN_CHUNKS = 6

VMEM_LIMIT = 48 * 1024 * 1024
DIFF_TQ, DIFF_TK = 512, 1024
DENSE_CE = 2048
DV_ROWS = 144


def _tile(n, pref):
    t = min(pref, n)
    t -= t % LANES
    while n % t:
        t -= LANES
    return t


def _params(sem):
    return pltpu.CompilerParams(dimension_semantics=sem, vmem_limit_bytes=VMEM_LIMIT)


def _lane_iota(shape):
    return lax.broadcasted_iota(jnp.int32, shape, len(shape) - 1)


def _rms(xf, gain_row, width):
    ms = jnp.sum(xf * xf, axis=-1, keepdims=True) * (1.0 / width)
    return xf * lax.rsqrt(ms + NORM_EPS) * gain_row


def _seg64_norm_rope(zt, gain_row, cosb, sinb):
    lane = _lane_iota(zt.shape)
    lo = lane < 64
    sq = zt * zt
    s0 = jnp.sum(jnp.where(lo, sq, 0.0), axis=-1, keepdims=True)
    s1 = jnp.sum(jnp.where(lo, 0.0, sq), axis=-1, keepdims=True)
    ms = jnp.where(lo, s0, s1) * (1.0 / 64)
    y = zt * lax.rsqrt(ms + NORM_EPS) * gain_row
    first = (lane % 64) < 32
    rot = jnp.where(first, pltpu.roll(y, 96, 1), pltpu.roll(y, 32, 1))
    return y * cosb + rot * sinb


def _in_proj_kernel(x_ref, mixg_ref, w_ref, cos_ref, sin_ref, hg_ref,
                    dq_ref, dk_ref, dv_ref, wq_ref, wkd_ref, wvd_ref, mq_ref, gate_ref, xn_sc):
    j = pl.program_id(1)

    @pl.when(j == 0)
    def _():
        xn_sc[...] = _rms(x_ref[...], mixg_ref[...], D_MODEL).astype(BF16)

    z = jnp.dot(xn_sc[...], w_ref[...], preferred_element_type=F32)
    cosb = cos_ref[...]
    sinb = sin_ref[...]

    @pl.when(j == 0)
    def _():
        for t in range(4):
            sl = slice(t * LANES, (t + 1) * LANES)
            dq_ref[:, sl] = _seg64_norm_rope(z[:, sl], hg_ref[0:1, :], cosb, sinb).astype(BF16)
            zs = z[:, 512 + t * LANES:512 + (t + 1) * LANES]
            dk_ref[:, sl] = _seg64_norm_rope(zs, hg_ref[1:2, :], cosb, sinb).astype(BF16)

    @pl.when(j == 1)
    def _():
        for t in range(4):
            sl = slice(t * LANES, (t + 1) * LANES)
            wq_ref[:, sl] = _seg64_norm_rope(z[:, sl], hg_ref[2:3, :], cosb, sinb).astype(BF16)
        k = _seg64_norm_rope(z[:, 512:640], hg_ref[3:4, :], cosb, sinb)
        v = z[:, 640:768]
        lo = _lane_iota(k.shape) < 64
        ones = jnp.ones(k.shape, BF16)
        for src, dst, stride in ((k, wkd_ref, 1), (v, wvd_ref, 2)):
            sw = pltpu.roll(src, 64, 1)
            dst[:, 0:LANES] = jnp.where(lo, src, sw).astype(BF16)
            dst[:, stride * LANES:(stride + 1) * LANES] = jnp.where(lo, sw, src).astype(BF16)
        wvd_ref[:, LANES:2 * LANES] = ones
        wvd_ref[:, 3 * LANES:4 * LANES] = ones

    @pl.when(j == 2)
    def _():
        for t in range(4):
            sl = slice(t * LANES, (t + 1) * LANES)
            mq_ref[:, sl] = _rms(z[:, sl], hg_ref[4:5, :], MA_DH).astype(BF16)
        ones = jnp.ones((DV_ROWS - LANES, z.shape[0]), BF16)
        for t in range(DA_HEADS):
            vt = z[:, 512 + t * LANES:512 + (t + 1) * LANES].T
            dv_ref[t * DV_ROWS:t * DV_ROWS + LANES, :] = vt.astype(BF16)
            dv_ref[t * DV_ROWS + LANES:(t + 1) * DV_ROWS, :] = ones

    @pl.when(j >= 3)
    def _():
        gate_ref[...] = jax.nn.sigmoid(z)


def _in_proj(x2d, seq, mixg, w_r, cos_t, sin_t, head_gains):
    n = x2d.shape[0]
    tm = _tile(seq, 1024)
    nsb = seq // tm
    bf = lambda w: jax.ShapeDtypeStruct((n, w), BF16)
    row = lambda w: pl.BlockSpec((tm, w), lambda i, j: (i, 0))
    return pl.pallas_call(
        _in_proj_kernel,
        grid=(n // tm, N_IN_CHUNKS),
        in_specs=[
            pl.BlockSpec((tm, D_MODEL), lambda i, j: (i, 0)),
            pl.BlockSpec((1, D_MODEL), lambda i, j: (0, 0)),
            pl.BlockSpec((D_MODEL, IN_CHUNK), lambda i, j: (0, j)),
            pl.BlockSpec((tm, LANES), lambda i, j: (i % nsb, 0)),
            pl.BlockSpec((tm, LANES), lambda i, j: (i % nsb, 0)),
            pl.BlockSpec((8, LANES), lambda i, j: (0, 0)),
        ],
        out_specs=[row(512), row(512),
                   pl.BlockSpec((DA_HEADS * DV_ROWS, tm), lambda i, j: (i // nsb, i % nsb)),
                   row(512), row(256), row(512), row(512),
                   pl.BlockSpec((tm, IN_CHUNK), lambda i, j: (i, jnp.maximum(j - 3, 0)))],
        out_shape=[bf(512), bf(512), jax.ShapeDtypeStruct((n // seq * DA_HEADS * DV_ROWS, seq), BF16),
                   bf(512), bf(256), bf(512), bf(512),
                   jax.ShapeDtypeStruct((n, N_BRANCH * D_MODEL), F32)],
        scratch_shapes=[pltpu.VMEM((tm, D_MODEL), BF16)],
        compiler_params=_params(("parallel", "arbitrary")),
        name="in_proj",
    )(x2d, mixg, w_r, cos_t, sin_t, head_gains)


def _diff_attn_kernel(q_ref, k_ref, v_ref, lam_ref, og_ref, o_ref, m_sc, acc_sc, s_sc, *, tk, lam_init):
    q = q_ref[...]
    lo = _lane_iota(q.shape) < 64
    zero = jnp.zeros_like(q)
    qs = (jnp.where(lo, q, zero), jnp.where(lo, zero, q))
    m_sc[...] = jnp.full(m_sc.shape, -jnp.inf, F32)
    acc_sc[...] = jnp.zeros(acc_sc.shape, F32)
    nt = (((1,), (1,)), ((), ()))
    n_chunks = k_ref.shape[0] // tk

    def scores(j, slot):
        off = pl.multiple_of(j * tk, tk)
        k = k_ref[pl.ds(off, tk), :]
        for c in range(2):
            s_sc[slot, c] = lax.dot_general(k, qs[c], nt, preferred_element_type=F32)

    def process(j, slot):
        off = pl.multiple_of(j * tk, tk)
        vt = v_ref[:, pl.ds(off, tk)]
        for c in range(2):
            s = s_sc[slot, c]
            m_old = m_sc[c]
            m_new = jnp.maximum(m_old, jnp.max(s, axis=0, keepdims=True))
            p = jnp.exp2(s - m_new)
            alpha = jnp.exp2(m_old - m_new)
            acc_sc[c] = alpha * acc_sc[c] + jnp.dot(vt, p.astype(BF16), preferred_element_type=F32)
            m_sc[c] = m_new

    scores(0, 0)
    if n_chunks == 1:
        process(0, 0)
    else:
        def pair(jj, carry):
            scores(2 * jj + 1, 1)
            process(2 * jj, 0)
            scores(2 * jj + 2, 0)
            process(2 * jj + 1, 1)
            return carry

        lax.fori_loop(0, n_chunks // 2 - 1, pair, 0)
        scores(n_chunks - 1, 1)
        process(n_chunks - 2, 0)
        process(n_chunks - 1, 1)

    lp = lam_ref[...]
    lam = (jnp.exp(jnp.sum(lp[0:1] * lp[1:2], axis=-1, keepdims=True))
           - jnp.exp(jnp.sum(lp[2:3] * lp[3:4], axis=-1, keepdims=True)) + lam_init)
    ot = (acc_sc[0, :LANES, :] / acc_sc[0, LANES:LANES + 1, :]
          - lam * (acc_sc[1, :LANES, :] / acc_sc[1, LANES:LANES + 1, :]))
    o_ref[...] = (_rms(ot.T, og_ref[...], 2 * DA_DH) * (1.0 - lam_init)).astype(BF16)


def _diff_attn(dq, dk, dv, da_lambda, og, nb, seq, lam_init):
    n = dq.shape[0]
    tq = _tile(seq, DIFF_TQ)
    tk = _tile(seq, DIFF_TK)
    nq = seq // tq
    assert seq // tk == 1 or (seq // tk) % 2 == 0, (seq, tk)
    kern = functools.partial(_diff_attn_kernel, tk=tk, lam_init=lam_init)
    return pl.pallas_call(
        kern,
        grid=(nb, DA_HEADS, nq),
        in_specs=[
            pl.BlockSpec((tq, LANES), lambda b, h, i: (b * nq + i, h)),
            pl.BlockSpec((seq, LANES), lambda b, h, i: (b, h)),
            pl.BlockSpec((DV_ROWS, seq), lambda b, h, i: (b * DA_HEADS + h, 0)),
            pl.BlockSpec((4, DA_DH), lambda b, h, i: (0, 0)),
            pl.BlockSpec((1, LANES), lambda b, h, i: (0, 0)),
        ],
        out_specs=pl.BlockSpec((tq, LANES), lambda b, h, i: (b * nq + i, h)),
        out_shape=jax.ShapeDtypeStruct((n, BRANCH_WIDTH), BF16),
        scratch_shapes=[pltpu.VMEM((2, 1, tq), F32), pltpu.VMEM((2, DV_ROWS, tq), F32),
                        pltpu.VMEM((2, 2, tk, tq), F32)],
        compiler_params=_params(("parallel", "parallel", "arbitrary")),
        name="diff_attn",
    )(dq, dk, dv, da_lambda, og)


def _window_attn_kernel(q_ref, kd_ref, vd_ref, sink_ref, o_ref, *, tq, seq):
    i = pl.program_id(1)
    nsub = tq // WINDOW
    kw_len = 3 * WINDOW
    nt = (((1,), (1,)), ((), ()))
    rep = WA_HEADS // WA_KV_HEADS
    lo = _lane_iota((WINDOW, LANES)) < 64
    log2e = math.log2(math.e)
    for n in range(nsub):
        gb = i * nsub + n
        start = pl.multiple_of(jnp.clip((gb - 1) * WINDOW, 0, seq - kw_len), WINDOW)
        qpos = gb * WINDOW + lax.broadcasted_iota(jnp.int32, (WINDOW, kw_len), 0)
        kpos = start + lax.broadcasted_iota(jnp.int32, (WINDOW, kw_len), 1)
        valid1 = jnp.abs(qpos - kpos) <= WINDOW
        valid = jnp.concatenate([valid1] * rep, axis=0)
        for g in range(WA_KV_HEADS):
            kg = kd_ref[pl.ds(start, kw_len), g * LANES:(g + 1) * LANES]
            vg = vd_ref[pl.ds(start, kw_len), 2 * g * LANES:2 * (g + 1) * LANES]
            tiles, sinks = [], []
            for e in range(rep):
                h = g * rep + e
                qp = q_ref[n * WINDOW:(n + 1) * WINDOW, (h // 2) * LANES:(h // 2 + 1) * LANES]
                zero = jnp.zeros_like(qp)
                tiles.append(jnp.where(lo, qp, zero) if h % 2 == 0 else jnp.where(lo, zero, qp))
                sinks.append(jnp.broadcast_to(sink_ref[h:h + 1, 0:1] * log2e, (WINDOW, 1)))
            qs = jnp.concatenate(tiles, axis=0)
            sk = jnp.concatenate(sinks, axis=0)
            s = lax.dot_general(qs, kg, nt, preferred_element_type=F32)
            s = jnp.where(valid, s, NEG_INF)
            m = jnp.maximum(jnp.max(s, axis=-1, keepdims=True), sk)
            r = jnp.dot(jnp.exp2(s - m).astype(BF16), vg, preferred_element_type=F32)
            o = r[:, :LANES] / (r[:, LANES:] + jnp.exp2(sk - m))
            for pr in range(rep // 2):
                col = (g * rep // 2 + pr) * LANES
                o_ref[n * WINDOW:(n + 1) * WINDOW, col:col + LANES] = jnp.where(
                    lo, o[2 * pr * WINDOW:(2 * pr + 1) * WINDOW], o[(2 * pr + 1) * WINDOW:(2 * pr + 2) * WINDOW]
                ).astype(BF16)


def _window_attn(wq, wkd, wvd, sink_b, nb, seq):
    n = wq.shape[0]
    tq = _tile(seq, 512)
    nq = seq // tq
    kern = functools.partial(_window_attn_kernel, tq=tq, seq=seq)
    return pl.pallas_call(
        kern,
        grid=(nb, nq),
        in_specs=[
            pl.BlockSpec((tq, 512), lambda b, i: (b * nq + i, 0)),
            pl.BlockSpec((seq, 256), lambda b, i: (b, 0)),
            pl.BlockSpec((seq, 512), lambda b, i: (b, 0)),
            pl.BlockSpec((8, LANES), lambda b, i: (0, 0)),
        ],
        out_specs=pl.BlockSpec((tq, 512), lambda b, i: (b * nq + i, 0)),
        out_shape=jax.ShapeDtypeStruct((n, BRANCH_WIDTH), BF16),
        compiler_params=_params(("parallel", "arbitrary")),
        name="window_attn",
    )(wq, wkd, wvd, sink_b)


def _mem_kv_kernel(mem_ref, mg_ref, w_ref, kg_ref, mk_ref, mv_ref):
    h = _rms(mem_ref[...], mg_ref[...], D_MODEL).astype(BF16)
    kv = jnp.dot(h, w_ref[...], preferred_element_type=F32)
    for t in range(MA_HEADS):
        sl = slice(t * LANES, (t + 1) * LANES)
        mk_ref[:, sl] = _rms(kv[:, sl], kg_ref[...], MA_DH).astype(BF16)
    mv_ref[...] = kv[:, 512:1024].astype(BF16)


def _mem_kv(mem2d, mlen, mg, w_kv, kg):
    n = mem2d.shape[0]
    return pl.pallas_call(
        _mem_kv_kernel,
        grid=(n // mlen,),
        in_specs=[
            pl.BlockSpec((mlen, D_MODEL), lambda b: (b, 0)),
            pl.BlockSpec((1, D_MODEL), lambda b: (0, 0)),
            pl.BlockSpec((D_MODEL, 2 * MA_HEADS * MA_DH), lambda b: (0, 0)),
            pl.BlockSpec((1, LANES), lambda b: (0, 0)),
        ],
        out_specs=[pl.BlockSpec((mlen, 512), lambda b: (b, 0)), pl.BlockSpec((mlen, 512), lambda b: (b, 0))],
        out_shape=[jax.ShapeDtypeStruct((n, 512), BF16), jax.ShapeDtypeStruct((n, 512), BF16)],
        compiler_params=_params(("parallel",)),
        name="mem_kv",
    )(mem2d, mg, w_kv, kg)


def _mem_attn_kernel(q_ref, mk_ref, mv_ref, o_ref):
    nt = (((1,), (1,)), ((), ()))
    for t in range(MA_HEADS):
        sl = slice(t * LANES, (t + 1) * LANES)
        s = lax.dot_general(q_ref[:, sl], mk_ref[:, sl], nt, preferred_element_type=F32)
        m = jnp.max(s, axis=-1, keepdims=True)
        ex = jnp.exp(s - m)
        p = ex / jnp.sum(ex, axis=-1, keepdims=True)
        o_ref[:, sl] = jnp.dot(p.astype(BF16), mv_ref[:, sl], preferred_element_type=F32).astype(BF16)


def _mem_attn(mq, mk, mv, nb, seq, mlen):
    n = mq.shape[0]
    tq = _tile(seq, 1024)
    nq = seq // tq
    return pl.pallas_call(
        _mem_attn_kernel,
        grid=(nb, nq),
        in_specs=[
            pl.BlockSpec((tq, 512), lambda b, i: (b * nq + i, 0)),
            pl.BlockSpec((mlen, 512), lambda b, i: (b, 0)),
            pl.BlockSpec((mlen, 512), lambda b, i: (b, 0)),
        ],
        out_specs=pl.BlockSpec((tq, 512), lambda b, i: (b * nq + i, 0)),
        out_shape=jax.ShapeDtypeStruct((n, BRANCH_WIDTH), BF16),
        compiler_params=_params(("parallel", "arbitrary")),
        name="mem_attn",
    )(mq, mk, mv)


def _merge_kernel(x_ref, oda_ref, owa_ref, oma_ref, gate_ref, wb_ref, wo_ref, o_ref):
    merged = None
    for nbr, br in enumerate((oda_ref, owa_ref, oma_ref)):
        proj = jnp.dot(br[...], wb_ref[nbr], preferred_element_type=F32)
        term = gate_ref[:, nbr * D_MODEL:(nbr + 1) * D_MODEL] * proj
        merged = term if merged is None else merged + term
    o_ref[...] = x_ref[...] + jnp.dot(merged.astype(BF16), wo_ref[...], preferred_element_type=F32)


def _merge(x2d, oda, owa, oma, gates, wb, wo):
    n = x2d.shape[0]
    tm = _tile(n, 512)
    row = lambda w: pl.BlockSpec((tm, w), lambda i: (i, 0))
    return pl.pallas_call(
        _merge_kernel,
        grid=(n // tm,),
        in_specs=[row(D_MODEL), row(512), row(512), row(512), row(N_BRANCH * D_MODEL),
                  pl.BlockSpec((N_BRANCH, BRANCH_WIDTH, D_MODEL), lambda i: (0, 0, 0)),
                  pl.BlockSpec((D_MODEL, D_MODEL), lambda i: (0, 0))],
        out_specs=row(D_MODEL),
        out_shape=jax.ShapeDtypeStruct((n, D_MODEL), F32),
        compiler_params=_params(("parallel",)),
        name="merge",
    )(x2d, oda, owa, oma, gates, wb, wo)


def _top16_lead(rows, ids):
    rows = list(rows)
    vals, idxs = [], []
    for _ in range(PEER_TOPK):
        level = list(zip(rows, ids))
        while len(level) > 1:
            nxt = []
            for j in range(0, len(level) - 1, 2):
                (va, ia), (vb, ib) = level[j], level[j + 1]
                take = va >= vb
                nxt.append((jnp.maximum(va, vb), jnp.where(take, ia, ib)))
            if len(level) % 2:
                nxt.append(level[-1])
            level = nxt
        m, sel = level[0]
        rows = [jnp.where(sel == i, -jnp.inf, r) for r, i in zip(rows, ids)]
        vals.append(m)
        idxs.append(sel)
    return vals, idxs


def _peer_q_kernel(x_ref, fg_ref, wq_ref, xn_ref, q_ref):
    xn = _rms(x_ref[...], fg_ref[...], D_MODEL).astype(BF16)
    xn_ref[...] = xn
    q = jnp.dot(xn, wq_ref[...], preferred_element_type=F32).astype(BF16)
    for hp in range(2 * PEER_HEADS):
        q_ref[hp] = q[:, hp * LANES:(hp + 1) * LANES]


def _peer_q(x2d, fg, wq):
    n = x2d.shape[0]
    tm = _tile(n, 512)
    return pl.pallas_call(
        _peer_q_kernel,
        grid=(n // tm,),
        in_specs=[
            pl.BlockSpec((tm, D_MODEL), lambda i: (i, 0)),
            pl.BlockSpec((1, D_MODEL), lambda i: (0, 0)),
            pl.BlockSpec((D_MODEL, 2 * PEER_HEADS * PEER_HALF), lambda i: (0, 0)),
        ],
        out_specs=[pl.BlockSpec((tm, D_MODEL), lambda i: (i, 0)),
                   pl.BlockSpec((2 * PEER_HEADS, tm, PEER_HALF), lambda i: (0, i, 0))],
        out_shape=[jax.ShapeDtypeStruct((n, D_MODEL), BF16),
                   jax.ShapeDtypeStruct((2 * PEER_HEADS, n, PEER_HALF), BF16)],
        compiler_params=_params(("parallel",)),
        name="peer_q",
    )(x2d, fg, wq)


_CAND_PAIRS = tuple((a, b) for a in range(PEER_TOPK) for b in range(PEER_TOPK) if (a + 1) * (b + 1) <= PEER_TOPK)


def _peer_topk_kernel(q_ref, keys_ref, g_ref, ei_ref, ej_ref, s0_sc, s1_sc, rg_sc, ri_sc, rj_sc, *, tm):
    nt = (((1,), (1,)), ((), ()))
    gt = tm // LANES
    key_ids = [float(k) for k in range(N_KEYS)]
    cand_ids = [float(a * PEER_TOPK + b) for a, b in _CAND_PAIRS]

    def head_body(h, carry):
        sv, si = [], []
        for p, s_sc in enumerate((s0_sc, s1_sc)):
            s = lax.dot_general(keys_ref[2 * h + p], q_ref[2 * h + p], nt,
                                preferred_element_type=F32)
            for tt in range(gt):
                s_sc[pl.ds(tt, N_KEYS, stride=gt), :] = s[:, tt * LANES:(tt + 1) * LANES]
            v, ix = _top16_lead([s_sc[k * gt:(k + 1) * gt, :] for k in range(N_KEYS)], key_ids)
            sv.append(v)
            si.append(ix)
        fv, fi = _top16_lead([sv[0][a] + sv[1][b] for a, b in _CAND_PAIRS], cand_ids)
        ex = [jnp.exp(v - fv[0]) for v in fv]
        den = ex[0]
        for e in ex[1:]:
            den = den + e
        for k in range(PEER_TOPK):
            fik = fi[k].astype(jnp.int32)
            fa = fik >> 4
            fb = fik & (PEER_TOPK - 1)
            ei = jnp.zeros((gt, LANES), F32)
            ej = jnp.zeros((gt, LANES), F32)
            for a in range(PEER_TOPK):
                ei = jnp.where(fa == a, si[0][a], ei)
                ej = jnp.where(fb == a, si[1][a], ej)
            off = pl.multiple_of((h * PEER_TOPK + k) * gt, gt)
            rg_sc[pl.ds(off, gt), :] = ex[k] / den
            ri_sc[pl.ds(off, gt), :] = ei
            rj_sc[pl.ds(off, gt), :] = ej
        return carry

    lax.fori_loop(0, PEER_HEADS, head_body, 0)

    for tt in range(gt):
        sl = slice(tt * LANES, (tt + 1) * LANES)
        g_ref[sl, :] = rg_sc[pl.ds(tt, N_KEYS, stride=gt), :].T
        ei_ref[sl, :] = ri_sc[pl.ds(tt, N_KEYS, stride=gt), :].T
        ej_ref[sl, :] = rj_sc[pl.ds(tt, N_KEYS, stride=gt), :].T


def _peer_topk(q3, keys):
    n = q3.shape[1]
    tm = _tile(n, 8 * LANES)
    slots = PEER_HEADS * PEER_TOPK
    out = jax.ShapeDtypeStruct((n, slots), F32)
    return pl.pallas_call(
        functools.partial(_peer_topk_kernel, tm=tm),
        grid=(n // tm,),
        in_specs=[
            pl.BlockSpec((2 * PEER_HEADS, tm, PEER_HALF), lambda i: (0, i, 0)),
            pl.BlockSpec((2 * PEER_HEADS, N_KEYS, PEER_HALF), lambda i: (0, 0, 0)),
        ],
        out_specs=[pl.BlockSpec((tm, slots), lambda i: (i, 0))] * 3,
        out_shape=[out, out, out],
        scratch_shapes=[pltpu.VMEM((N_KEYS * tm // LANES, LANES), F32)] * 2
        + [pltpu.VMEM((slots * tm // LANES, LANES), F32)] * 3,
        compiler_params=_params(("parallel",)),
        name="peer_topk",
    )(q3, keys)


def _peer_wbuild_kernel(g_ref, ei_ref, ej_ref, wt_ref, w_sc, *, tm, pitch):
    nt = (((1,), (1,)), ((), ()))
    sub = lax.broadcasted_iota(jnp.int32, (N_KEYS, LANES), 0).astype(F32)

    def tok_body(t, carry):
        g_r = g_ref[pl.ds(t, 1), :]
        at = jnp.where(sub == ei_ref[pl.ds(t, 1), :], g_r, 0.0).astype(BF16)
        bt = jnp.where(sub == ej_ref[pl.ds(t, 1), :], 1.0, 0.0).astype(BF16)
        w_t = lax.dot_general(at, bt, nt, preferred_element_type=F32)
        w_sc[pl.ds(t, N_KEYS, stride=pitch), :] = w_t
        return carry

    lax.fori_loop(0, tm, tok_body, 0, unroll=True)

    for i in range(N_KEYS):
        wt_ref[i] = w_sc[i * pitch:i * pitch + tm, :].astype(BF16)


def _peer_wbuild(g, ei, ej):
    n = g.shape[0]
    tm = _tile(n, 128)
    pitch = tm + 4
    row = pl.BlockSpec((tm, PEER_HEADS * PEER_TOPK), lambda i: (i, 0))
    return pl.pallas_call(
        functools.partial(_peer_wbuild_kernel, tm=tm, pitch=pitch),
        grid=(n // tm,),
        in_specs=[row, row, row],
        out_specs=pl.BlockSpec((N_KEYS, tm, N_KEYS), lambda i: (0, i, 0)),
        out_shape=jax.ShapeDtypeStruct((N_KEYS, n, N_KEYS), BF16),
        scratch_shapes=[pltpu.VMEM((N_KEYS * pitch, N_KEYS), F32)],
        compiler_params=_params(("parallel",)),
        name="peer_wbuild",
    )(g, ei, ej)


def _peer_dense_kernel(xn_ref, ut_ref, wt_ref, v_ref, x_ref, o_ref, acc_sc, *, ce):
    c = pl.program_id(1)

    @pl.when(c == 0)
    def _():
        acc_sc[...] = jnp.zeros(acc_sc.shape, F32)

    a = jnp.dot(xn_ref[...], ut_ref[...], preferred_element_type=F32)
    act = 0.5 * a * (1.0 + lax.erf(a * math.sqrt(0.5)))
    h = jnp.concatenate([(wt_ref[s].astype(F32) * act[:, s * LANES:(s + 1) * LANES]).astype(BF16)
                         for s in range(ce // LANES)], axis=1)
    acc_sc[...] += jnp.dot(h, v_ref[...], preferred_element_type=F32)

    @pl.when(c == pl.num_programs(1) - 1)
    def _():
        o_ref[...] = x_ref[...] + acc_sc[...]


def _peer_dense(xn, ut, wt, v, x2d):
    n = x2d.shape[0]
    tm = _tile(n, 512)
    ce = DENSE_CE
    kern = functools.partial(_peer_dense_kernel, ce=ce)
    return pl.pallas_call(
        kern,
        grid=(n // tm, N_EXPERTS // ce),
        in_specs=[
            pl.BlockSpec((tm, D_MODEL), lambda i, c: (i, 0)),
            pl.BlockSpec((D_MODEL, ce), lambda i, c: (0, c)),
            pl.BlockSpec((ce // N_KEYS, tm, N_KEYS), lambda i, c: (c, i, 0)),
            pl.BlockSpec((ce, D_MODEL), lambda i, c: (c, 0)),
            pl.BlockSpec((tm, D_MODEL), lambda i, c: (i, 0)),
        ],
        out_specs=pl.BlockSpec((tm, D_MODEL), lambda i, c: (i, 0)),
        out_shape=jax.ShapeDtypeStruct((n, D_MODEL), F32),
        scratch_shapes=[pltpu.VMEM((tm, D_MODEL), F32)],
        compiler_params=_params(("parallel", "arbitrary")),
        name="peer_dense",
    )(xn, ut, wt, v, x2d)


def _rope_tiles(seq):
    inv = 1.0 / (ROPE_THETA ** (jnp.arange(0, DA_DH, 2, dtype=F32) / DA_DH))
    ang = jnp.arange(seq, dtype=F32)[:, None] * inv[None, :]
    cos, sin = jnp.cos(ang), jnp.sin(ang)
    cos_t = jnp.tile(cos, (1, 4))
    sign = jnp.tile(jnp.concatenate([-jnp.ones((32,), F32), jnp.ones((32,), F32)]), 2)
    sin_t = jnp.tile(sin, (1, 4)) * sign[None, :]
    return cos_t, sin_t


def _prep_layer(l, mix_norm, w_in, da_q_norm, da_k_norm, da_lambda, da_out_norm, wa_q_norm, wa_k_norm,
                wa_sink, mem_norm, w_mem_kv, ma_q_norm, ma_k_norm, w_branch, w_out, ffn_norm,
                peer_wq, peer_keys, peer_u, peer_v):
    w = w_in[l]
    pad = jnp.zeros((D_MODEL, 256), F32)
    w_r = jnp.concatenate([
        w[:, DA_Q_OFF:DA_V_OFF],
        w[:, WA_Q_OFF:MA_Q_OFF], pad,
        w[:, MA_Q_OFF:GATE_OFF], w[:, DA_V_OFF:WA_Q_OFF],
        w[:, GATE_OFF:]], axis=1).astype(BF16)
    t2 = lambda g: jnp.tile(g.astype(F32), 2)
    head_gains = jnp.stack([
        t2(da_q_norm[l]) * (DA_DH ** -0.5 * math.log2(math.e)), t2(da_k_norm[l]),
        t2(wa_q_norm[l]) * (WA_DH ** -0.5 * math.log2(math.e)), t2(wa_k_norm[l]),
        ma_q_norm[l].astype(F32) * MA_DH ** -0.5,
        jnp.zeros((LANES,), F32), jnp.zeros((LANES,), F32), jnp.zeros((LANES,), F32)])
    return dict(
        mixg=mix_norm[l][None, :], w_r=w_r, head_gains=head_gains,
        da_lambda=da_lambda[l], og=da_out_norm[l][None, :],
        sink=jnp.broadcast_to(wa_sink[l].astype(F32)[:, None], (WA_HEADS, LANES)),
        mem_g=mem_norm[l][None, :], w_kv=w_mem_kv[l].astype(BF16), ma_kg=ma_k_norm[l][None, :],
        wb=w_branch[l].astype(BF16), wo=w_out[l].astype(BF16),
        ffn_g=ffn_norm[l][None, :], peer_wq=peer_wq[l].astype(BF16),
        keys=peer_keys[l].reshape(2 * PEER_HEADS, N_KEYS, PEER_HALF).astype(BF16),
        ut=peer_u[l].astype(BF16).T, v=peer_v[l].astype(BF16),
    )


def _trunk(x, mem, layers):
    nb, seq, _ = x.shape
    mlen = mem.shape[1]
    x2d = x.reshape(nb * seq, D_MODEL)
    mem2d = mem.reshape(nb * mlen, D_MODEL)
    cos_t, sin_t = _rope_tiles(seq)
    for l, p in enumerate(layers):
        lam_init = 0.8 - 0.6 * math.exp(-0.3 * l)
        dq, dk, dv, wq, wkd, wvd, mq, gates = _in_proj(x2d, seq, p["mixg"], p["w_r"], cos_t, sin_t,
                                                      p["head_gains"])
        oda = _diff_attn(dq, dk, dv, p["da_lambda"], p["og"], nb, seq, lam_init)
        owa = _window_attn(wq, wkd, wvd, p["sink"], nb, seq)
        mk, mv = _mem_kv(mem2d, mlen, p["mem_g"], p["w_kv"], p["ma_kg"])
        oma = _mem_attn(mq, mk, mv, nb, seq, mlen)
        x1 = _merge(x2d, oda, owa, oma, gates, p["wb"], p["wo"])
        xn, q3 = _peer_q(x1, p["ffn_g"], p["peer_wq"])
        wt = _peer_wbuild(*_peer_topk(q3, p["keys"]))
        x2d = _peer_dense(xn, p["ut"], wt, p["v"], x1)
    return x2d.reshape(nb, seq, D_MODEL)


def kernel(x_prompt, x_sample, mem_prompt, mem_sample, mix_norm, w_in, da_q_norm, da_k_norm, da_lambda, da_out_norm, wa_q_norm, wa_k_norm, wa_sink, mem_norm, w_mem_kv, ma_q_norm, ma_k_norm, w_branch, w_out, ffn_norm, peer_wq, peer_keys, peer_u, peer_v):
    weights = (mix_norm, w_in, da_q_norm, da_k_norm, da_lambda, da_out_norm, wa_q_norm, wa_k_norm,
               wa_sink, mem_norm, w_mem_kv, ma_q_norm, ma_k_norm, w_branch, w_out, ffn_norm,
               peer_wq, peer_keys, peer_u, peer_v)
    layers = [_prep_layer(l, *weights) for l in range(w_in.shape[0])]
    y_prompt = _trunk(x_prompt, mem_prompt, layers)
    y_sample = _trunk(x_sample, mem_sample, layers)
    return (y_prompt, y_sample)
```

```python
import functools
import math

import jax
import jax.numpy as jnp
import numpy as np
from jax import lax
from jax.experimental import pallas as pl
from jax.experimental.pallas import tpu as pltpu

F32 = jnp.float32
BF16 = jnp.bfloat16

D_MODEL = 1024
NORM_EPS = 1e-6
NEG_INF = -1e30
ROPE_THETA = 10000.0
LANES = 128

DA_HEADS, DA_DH = 4, 64
WA_HEADS, WA_KV_HEADS, WA_DH, WINDOW = 8, 2, 64, 128
MA_HEADS, MA_DH = 4, 128
N_BRANCH, BRANCH_WIDTH = 3, 512
PEER_HEADS, N_KEYS, PEER_HALF, PEER_TOPK = 8, 128, 128, 16
N_EXPERTS = N_KEYS * N_KEYS

DA_Q_OFF = 0
DA_K_OFF = 512
DA_V_OFF = 1024
WA_Q_OFF = 1536
WA_K_OFF = 2048
WA_V_OFF = 2176
MA_Q_OFF = 2304
GATE_OFF = 2816
IN_CHUNK = 1024
N_IN_CHUNKS = 6

VMEM_LIMIT = 48 * 1024 * 1024
DIFF_TQ, DIFF_TK = 512, 2048
DENSE_CE = 2048
DV_ROWS = 144


def _tile(n, pref):
    t = min(pref, n)
    t -= t % LANES
    while n % t:
        t -= LANES
    return t


def _params(sem):
    return pltpu.CompilerParams(dimension_semantics=sem, vmem_limit_bytes=VMEM_LIMIT)


def _lane_iota(shape):
    return lax.broadcasted_iota(jnp.int32, shape, len(shape) - 1)


def _rms(xf, gain_row, width):
    ms = jnp.sum(xf * xf, axis=-1, keepdims=True) * (1.0 / width)
    return xf * lax.rsqrt(ms + NORM_EPS) * gain_row


def _seg64_norm_rope(zt, gain_row, cosb, sinb):
    lane = _lane_iota(zt.shape)
    lo = lane < 64
    sq = zt * zt
    s0 = jnp.sum(jnp.where(lo, sq, 0.0), axis=-1, keepdims=True)
    s1 = jnp.sum(jnp.where(lo, 0.0, sq), axis=-1, keepdims=True)
    ms = jnp.where(lo, s0, s1) * (1.0 / 64)
    y = zt * lax.rsqrt(ms + NORM_EPS) * gain_row
    first = (lane % 64) < 32
    rot = jnp.where(first, pltpu.roll(y, 96, 1), pltpu.roll(y, 32, 1))
    return y * cosb + rot * sinb


def _in_proj_kernel(x_ref, mixg_ref, w_ref, cos_ref, sin_ref, hg_ref,
                    dq_ref, dk_ref, dv_ref, wq_ref, wkd_ref, wvd_ref, mq_ref, gate_ref):
    xn = _rms(x_ref[...], mixg_ref[...], D_MODEL).astype(BF16)
    cosb = cos_ref[...]
    sinb = sin_ref[...]

    def chunk(c):
        return jnp.dot(xn, w_ref[:, c * IN_CHUNK:(c + 1) * IN_CHUNK], preferred_element_type=F32)

    z = chunk(0)
    for t in range(4):
        sl = slice(t * LANES, (t + 1) * LANES)
        dq_ref[:, sl] = _seg64_norm_rope(z[:, sl], hg_ref[0:1, :], cosb, sinb).astype(BF16)
        zs = z[:, 512 + t * LANES:512 + (t + 1) * LANES]
        dk_ref[:, sl] = _seg64_norm_rope(zs, hg_ref[1:2, :], cosb, sinb).astype(BF16)

    z = chunk(1)
    for t in range(4):
        sl = slice(t * LANES, (t + 1) * LANES)
        wq_ref[:, sl] = _seg64_norm_rope(z[:, sl], hg_ref[2:3, :], cosb, sinb).astype(BF16)
    k = _seg64_norm_rope(z[:, 512:640], hg_ref[3:4, :], cosb, sinb)
    v = z[:, 640:768]
    lo = _lane_iota(k.shape) < 64
    ones = jnp.ones(k.shape, BF16)
    for src, dst, stride in ((k, wkd_ref, 1), (v, wvd_ref, 2)):
        sw = pltpu.roll(src, 64, 1)
        dst[:, 0:LANES] = jnp.where(lo, src, sw).astype(BF16)
        dst[:, stride * LANES:(stride + 1) * LANES] = jnp.where(lo, sw, src).astype(BF16)
    wvd_ref[:, LANES:2 * LANES] = ones
    wvd_ref[:, 3 * LANES:4 * LANES] = ones

    z = chunk(2)
    for t in range(4):
        sl = slice(t * LANES, (t + 1) * LANES)
        mq_ref[:, sl] = _rms(z[:, sl], hg_ref[4:5, :], MA_DH).astype(BF16)
    ones = jnp.ones((DV_ROWS - LANES, z.shape[0]), BF16)
    for t in range(DA_HEADS):
        vt = z[:, 512 + t * LANES:512 + (t + 1) * LANES].T
        dv_ref[t * DV_ROWS:t * DV_ROWS + LANES, :] = vt.astype(BF16)
        dv_ref[t * DV_ROWS + LANES:(t + 1) * DV_ROWS, :] = ones

    for c in range(3, N_IN_CHUNKS):
        gate_ref[:, (c - 3) * IN_CHUNK:(c - 2) * IN_CHUNK] = jax.nn.sigmoid(chunk(c))


def _in_proj(x2d, seq, mixg, w_r, cos_t, sin_t, head_gains):
    n = x2d.shape[0]
    tm = _tile(seq, 512)
    nsb = seq // tm
    bf = lambda w: jax.ShapeDtypeStruct((n, w), BF16)
    row = lambda w: pl.BlockSpec((tm, w), lambda i: (i, 0))
    return pl.pallas_call(
        _in_proj_kernel,
        grid=(n // tm,),
        in_specs=[
            pl.BlockSpec((tm, D_MODEL), lambda i: (i, 0)),
            pl.BlockSpec((1, D_MODEL), lambda i: (0, 0)),
            pl.BlockSpec((D_MODEL, N_IN_CHUNKS * IN_CHUNK), lambda i: (0, 0), pipeline_mode=pl.Buffered(1)),
            pl.BlockSpec((tm, LANES), lambda i: (i % nsb, 0)),
            pl.BlockSpec((tm, LANES), lambda i: (i % nsb, 0)),
            pl.BlockSpec((8, LANES), lambda i: (0, 0)),
        ],
        out_specs=[row(512), row(512),
                   pl.BlockSpec((DA_HEADS * DV_ROWS, tm), lambda i: (i // nsb, i % nsb)),
                   row(512), row(256), row(512), row(512), row(N_BRANCH * D_MODEL)],
        out_shape=[bf(512), bf(512), jax.ShapeDtypeStruct((n // seq * DA_HEADS * DV_ROWS, seq), BF16),
                   bf(512), bf(256), bf(512), bf(512),
                   jax.ShapeDtypeStruct((n, N_BRANCH * D_MODEL), F32)],
        compiler_params=_params(("parallel",)),
        name="in_proj",
    )(x2d, mixg, w_r, cos_t, sin_t, head_gains)


def _diff_attn_kernel(q_ref, k_ref, v_ref, lam_ref, og_ref, o_ref, m_sc, acc_sc, s_sc, *, tk, lam_init):
    q = q_ref[...]
    lo = _lane_iota(q.shape) < 64
    zero = jnp.zeros_like(q)
    qs = (jnp.where(lo, q, zero), jnp.where(lo, zero, q))
    m_sc[...] = jnp.full(m_sc.shape, -jnp.inf, F32)
    acc_sc[...] = jnp.zeros(acc_sc.shape, F32)
    nt = (((1,), (1,)), ((), ()))
    n_chunks = k_ref.shape[0] // tk

    def scores(j, slot):
        off = pl.multiple_of(j * tk, tk)
        k = k_ref[pl.ds(off, tk), :]
        for c in range(2):
            s_sc[slot, c] = lax.dot_general(k, qs[c], nt, preferred_element_type=F32)

    def process(j, slot):
        off = pl.multiple_of(j * tk, tk)
        vt = v_ref[:, pl.ds(off, tk)]
        for c in range(2):
            s = s_sc[slot, c]
            m_old = m_sc[c]
            m_new = jnp.maximum(m_old, jnp.max(s, axis=0, keepdims=True))
            p = jnp.exp2(s - m_new)
            alpha = jnp.exp2(m_old - m_new)
            acc_sc[c] = alpha * acc_sc[c] + jnp.dot(vt, p.astype(BF16), preferred_element_type=F32)
            m_sc[c] = m_new

    scores(0, 0)
    if n_chunks == 1:
        process(0, 0)
    else:
        def pair(jj, carry):
            scores(2 * jj + 1, 1)
            process(2 * jj, 0)
            scores(2 * jj + 2, 0)
            process(2 * jj + 1, 1)
            return carry

        lax.fori_loop(0, n_chunks // 2 - 1, pair, 0)
        scores(n_chunks - 1, 1)
        process(n_chunks - 2, 0)
        process(n_chunks - 1, 1)

    lp = lam_ref[...]
    lam = (jnp.exp(jnp.sum(lp[0:1] * lp[1:2], axis=-1, keepdims=True))
           - jnp.exp(jnp.sum(lp[2:3] * lp[3:4], axis=-1, keepdims=True)) + lam_init)
    ot = (acc_sc[0, :LANES, :] / acc_sc[0, LANES:LANES + 1, :]
          - lam * (acc_sc[1, :LANES, :] / acc_sc[1, LANES:LANES + 1, :]))
    o_ref[...] = (_rms(ot.T, og_ref[...], 2 * DA_DH) * (1.0 - lam_init)).astype(BF16)


def _diff_attn(dq, dk, dv, da_lambda, og, nb, seq, lam_init):
    n = dq.shape[0]
    tq = _tile(seq, DIFF_TQ)
    tk = _tile(seq, DIFF_TK)
    nq = seq // tq
    assert seq // tk == 1 or (seq // tk) % 2 == 0, (seq, tk)
    kern = functools.partial(_diff_attn_kernel, tk=tk, lam_init=lam_init)
    return pl.pallas_call(
        kern,
        grid=(nb, DA_HEADS, nq),
        in_specs=[
            pl.BlockSpec((tq, LANES), lambda b, h, i: (b * nq + i, h)),
            pl.BlockSpec((seq, LANES), lambda b, h, i: (b, h)),
            pl.BlockSpec((DV_ROWS, seq), lambda b, h, i: (b * DA_HEADS + h, 0)),
            pl.BlockSpec((4, DA_DH), lambda b, h, i: (0, 0)),
            pl.BlockSpec((1, LANES), lambda b, h, i: (0, 0)),
        ],
        out_specs=pl.BlockSpec((tq, LANES), lambda b, h, i: (b * nq + i, h)),
        out_shape=jax.ShapeDtypeStruct((n, BRANCH_WIDTH), BF16),
        scratch_shapes=[pltpu.VMEM((2, 1, tq), F32), pltpu.VMEM((2, DV_ROWS, tq), F32),
                        pltpu.VMEM((2, 2, tk, tq), F32)],
        compiler_params=_params(("parallel", "parallel", "arbitrary")),
        name="diff_attn",
    )(dq, dk, dv, da_lambda, og)


def _window_attn_kernel(q_ref, kd_ref, vd_ref, sink_ref, o_ref, *, tq, seq):
    i = pl.program_id(1)
    nsub = tq // WINDOW
    kw_len = 3 * WINDOW
    nt = (((1,), (1,)), ((), ()))
    rep = WA_HEADS // WA_KV_HEADS
    lo = _lane_iota((WINDOW, LANES)) < 64
    log2e = math.log2(math.e)
    for n in range(nsub):
        gb = i * nsub + n
        start = pl.multiple_of(jnp.clip((gb - 1) * WINDOW, 0, seq - kw_len), WINDOW)
        qpos = gb * WINDOW + lax.broadcasted_iota(jnp.int32, (WINDOW, kw_len), 0)
        kpos = start + lax.broadcasted_iota(jnp.int32, (WINDOW, kw_len), 1)
        valid1 = jnp.abs(qpos - kpos) <= WINDOW
        valid = jnp.concatenate([valid1] * rep, axis=0)
        for g in range(WA_KV_HEADS):
            kg = kd_ref[pl.ds(start, kw_len), g * LANES:(g + 1) * LANES]
            vg = vd_ref[pl.ds(start, kw_len), 2 * g * LANES:2 * (g + 1) * LANES]
            tiles, sinks = [], []
            for e in range(rep):
                h = g * rep + e
                qp = q_ref[n * WINDOW:(n + 1) * WINDOW, (h // 2) * LANES:(h // 2 + 1) * LANES]
                zero = jnp.zeros_like(qp)
                tiles.append(jnp.where(lo, qp, zero) if h % 2 == 0 else jnp.where(lo, zero, qp))
                sinks.append(jnp.broadcast_to(sink_ref[h:h + 1, 0:1] * log2e, (WINDOW, 1)))
            qs = jnp.concatenate(tiles, axis=0)
            sk = jnp.concatenate(sinks, axis=0)
            s = lax.dot_general(qs, kg, nt, preferred_element_type=F32)
            s = jnp.where(valid, s, NEG_INF)
            m = jnp.maximum(jnp.max(s, axis=-1, keepdims=True), sk)
            r = jnp.dot(jnp.exp2(s - m).astype(BF16), vg, preferred_element_type=F32)
            o = r[:, :LANES] / (r[:, LANES:] + jnp.exp2(sk - m))
            for pr in range(rep // 2):
                col = (g * rep // 2 + pr) * LANES
                o_ref[n * WINDOW:(n + 1) * WINDOW, col:col + LANES] = jnp.where(
                    lo, o[2 * pr * WINDOW:(2 * pr + 1) * WINDOW], o[(2 * pr + 1) * WINDOW:(2 * pr + 2) * WINDOW]
                ).astype(BF16)


def _window_attn(wq, wkd, wvd, sink_b, nb, seq):
    n = wq.shape[0]
    tq = _tile(seq, 512)
    nq = seq // tq
    kern = functools.partial(_window_attn_kernel, tq=tq, seq=seq)
    return pl.pallas_call(
        kern,
        grid=(nb, nq),
        in_specs=[
            pl.BlockSpec((tq, 512), lambda b, i: (b * nq + i, 0)),
            pl.BlockSpec((seq, 256), lambda b, i: (b, 0)),
            pl.BlockSpec((seq, 512), lambda b, i: (b, 0)),
            pl.BlockSpec((8, LANES), lambda b, i: (0, 0)),
        ],
        out_specs=pl.BlockSpec((tq, 512), lambda b, i: (b * nq + i, 0)),
        out_shape=jax.ShapeDtypeStruct((n, BRANCH_WIDTH), BF16),
        compiler_params=_params(("parallel", "arbitrary")),
        name="window_attn",
    )(wq, wkd, wvd, sink_b)


def _mem_kv_kernel(mem_ref, mg_ref, w_ref, kg_ref, mk_ref, mv_ref):
    h = _rms(mem_ref[...], mg_ref[...], D_MODEL).astype(BF16)
    kv = jnp.dot(h, w_ref[...], preferred_element_type=F32)
    for t in range(MA_HEADS):
        sl = slice(t * LANES, (t + 1) * LANES)
        mk_ref[:, sl] = _rms(kv[:, sl], kg_ref[...], MA_DH).astype(BF16)
    mv_ref[...] = kv[:, 512:1024].astype(BF16)


def _mem_kv(mem2d, mlen, mg, w_kv, kg):
    n = mem2d.shape[0]
    return pl.pallas_call(
        _mem_kv_kernel,
        grid=(n // mlen,),
        in_specs=[
            pl.BlockSpec((mlen, D_MODEL), lambda b: (b, 0)),
            pl.BlockSpec((1, D_MODEL), lambda b: (0, 0)),
            pl.BlockSpec((D_MODEL, 2 * MA_HEADS * MA_DH), lambda b: (0, 0)),
            pl.BlockSpec((1, LANES), lambda b: (0, 0)),
        ],
        out_specs=[pl.BlockSpec((mlen, 512), lambda b: (b, 0)), pl.BlockSpec((mlen, 512), lambda b: (b, 0))],
        out_shape=[jax.ShapeDtypeStruct((n, 512), BF16), jax.ShapeDtypeStruct((n, 512), BF16)],
        compiler_params=_params(("parallel",)),
        name="mem_kv",
    )(mem2d, mg, w_kv, kg)


def _mem_attn_kernel(q_ref, mk_ref, mv_ref, o_ref):
    nt = (((1,), (1,)), ((), ()))
    for t in range(MA_HEADS):
        sl = slice(t * LANES, (t + 1) * LANES)
        s = lax.dot_general(q_ref[:, sl], mk_ref[:, sl], nt, preferred_element_type=F32)
        m = jnp.max(s, axis=-1, keepdims=True)
        ex = jnp.exp(s - m)
        p = ex / jnp.sum(ex, axis=-1, keepdims=True)
        o_ref[:, sl] = jnp.dot(p.astype(BF16), mv_ref[:, sl], preferred_element_type=F32).astype(BF16)


def _mem_attn(mq, mk, mv, nb, seq, mlen):
    n = mq.shape[0]
    tq = _tile(seq, 1024)
    nq = seq // tq
    return pl.pallas_call(
        _mem_attn_kernel,
        grid=(nb, nq),
        in_specs=[
            pl.BlockSpec((tq, 512), lambda b, i: (b * nq + i, 0)),
            pl.BlockSpec((mlen, 512), lambda b, i: (b, 0)),
            pl.BlockSpec((mlen, 512), lambda b, i: (b, 0)),
        ],
        out_specs=pl.BlockSpec((tq, 512), lambda b, i: (b * nq + i, 0)),
        out_shape=jax.ShapeDtypeStruct((n, BRANCH_WIDTH), BF16),
        compiler_params=_params(("parallel", "arbitrary")),
        name="mem_attn",
    )(mq, mk, mv)


def _merge_kernel(x_ref, oda_ref, owa_ref, oma_ref, gate_ref, wb_ref, wo_ref, o_ref):
    merged = None
    for nbr, br in enumerate((oda_ref, owa_ref, oma_ref)):
        proj = jnp.dot(br[...], wb_ref[nbr], preferred_element_type=F32)
        term = gate_ref[:, nbr * D_MODEL:(nbr + 1) * D_MODEL] * proj
        merged = term if merged is None else merged + term
    o_ref[...] = x_ref[...] + jnp.dot(merged.astype(BF16), wo_ref[...], preferred_element_type=F32)


def _merge(x2d, oda, owa, oma, gates, wb, wo):
    n = x2d.shape[0]
    tm = _tile(n, 512)
    row = lambda w: pl.BlockSpec((tm, w), lambda i: (i, 0))
    return pl.pallas_call(
        _merge_kernel,
        grid=(n // tm,),
        in_specs=[row(D_MODEL), row(512), row(512), row(512), row(N_BRANCH * D_MODEL),
                  pl.BlockSpec((N_BRANCH, BRANCH_WIDTH, D_MODEL), lambda i: (0, 0, 0)),
                  pl.BlockSpec((D_MODEL, D_MODEL), lambda i: (0, 0))],
        out_specs=row(D_MODEL),
        out_shape=jax.ShapeDtypeStruct((n, D_MODEL), F32),
        compiler_params=_params(("parallel",)),
        name="merge",
    )(x2d, oda, owa, oma, gates, wb, wo)


def _top16_lead(rows, ids):
    rows = list(rows)
    vals, idxs = [], []
    for _ in range(PEER_TOPK):
        level = list(zip(rows, ids))
        while len(level) > 1:
            nxt = []
            for j in range(0, len(level) - 1, 2):
                (va, ia), (vb, ib) = level[j], level[j + 1]
                take = va >= vb
                nxt.append((jnp.maximum(va, vb), jnp.where(take, ia, ib)))
            if len(level) % 2:
                nxt.append(level[-1])
            level = nxt
        m, sel = level[0]
        rows = [jnp.where(sel == i, -jnp.inf, r) for r, i in zip(rows, ids)]
        vals.append(m)
        idxs.append(sel)
    return vals, idxs


def _peer_q_kernel(x_ref, fg_ref, wq_ref, xn_ref, q_ref):
    xn = _rms(x_ref[...], fg_ref[...], D_MODEL).astype(BF16)
    xn_ref[...] = xn
    q = jnp.dot(xn, wq_ref[...], preferred_element_type=F32).astype(BF16)
    for hp in range(2 * PEER_HEADS):
        q_ref[hp] = q[:, hp * LANES:(hp + 1) * LANES]


def _peer_q(x2d, fg, wq):
    n = x2d.shape[0]
    tm = _tile(n, 512)
    return pl.pallas_call(
        _peer_q_kernel,
        grid=(n // tm,),
        in_specs=[
            pl.BlockSpec((tm, D_MODEL), lambda i: (i, 0)),
            pl.BlockSpec((1, D_MODEL), lambda i: (0, 0)),
            pl.BlockSpec((D_MODEL, 2 * PEER_HEADS * PEER_HALF), lambda i: (0, 0)),
        ],
        out_specs=[pl.BlockSpec((tm, D_MODEL), lambda i: (i, 0)),
                   pl.BlockSpec((2 * PEER_HEADS, tm, PEER_HALF), lambda i: (0, i, 0))],
        out_shape=[jax.ShapeDtypeStruct((n, D_MODEL), BF16),
                   jax.ShapeDtypeStruct((2 * PEER_HEADS, n, PEER_HALF), BF16)],
        compiler_params=_params(("parallel",)),
        name="peer_q",
    )(x2d, fg, wq)


_CAND_PAIRS = tuple((a, b) for a in range(PEER_TOPK) for b in range(PEER_TOPK) if (a + 1) * (b + 1) <= PEER_TOPK)


def _peer_topk_kernel(q_ref, keys_ref, g_ref, ei_ref, ej_ref, s0_sc, s1_sc, rg_sc, ri_sc, rj_sc, *, tm):
    nt = (((1,), (1,)), ((), ()))
    gt = tm // LANES
    key_ids = [float(k) for k in range(N_KEYS)]
    cand_ids = [float(a * PEER_TOPK + b) for a, b in _CAND_PAIRS]

    def head_body(h, carry):
        sv, si = [], []
        for p, s_sc in enumerate((s0_sc, s1_sc)):
            s = lax.dot_general(keys_ref[2 * h + p], q_ref[2 * h + p], nt,
                                preferred_element_type=F32)
            for tt in range(gt):
                s_sc[pl.ds(tt, N_KEYS, stride=gt), :] = s[:, tt * LANES:(tt + 1) * LANES]
            v, ix = _top16_lead([s_sc[k * gt:(k + 1) * gt, :] for k in range(N_KEYS)], key_ids)
            sv.append(v)
            si.append(ix)
        fv, fi = _top16_lead([sv[0][a] + sv[1][b] for a, b in _CAND_PAIRS], cand_ids)
        ex = [jnp.exp(v - fv[0]) for v in fv]
        den = ex[0]
        for e in ex[1:]:
            den = den + e
        for k in range(PEER_TOPK):
            fik = fi[k].astype(jnp.int32)
            fa = fik >> 4
            fb = fik & (PEER_TOPK - 1)
            ei = jnp.zeros((gt, LANES), F32)
            ej = jnp.zeros((gt, LANES), F32)
            for a in range(PEER_TOPK):
                ei = jnp.where(fa == a, si[0][a], ei)
                ej = jnp.where(fb == a, si[1][a], ej)
            off = pl.multiple_of((h * PEER_TOPK + k) * gt, gt)
            rg_sc[pl.ds(off, gt), :] = ex[k] / den
            ri_sc[pl.ds(off, gt), :] = ei
            rj_sc[pl.ds(off, gt), :] = ej
        return carry

    lax.fori_loop(0, PEER_HEADS, head_body, 0)

    for tt in range(gt):
        sl = slice(tt * LANES, (tt + 1) * LANES)
        g_ref[sl, :] = rg_sc[pl.ds(tt, N_KEYS, stride=gt), :].T
        ei_ref[sl, :] = ri_sc[pl.ds(tt, N_KEYS, stride=gt), :].T
        ej_ref[sl, :] = rj_sc[pl.ds(tt, N_KEYS, stride=gt), :].T


def _peer_topk(q3, keys):
    n = q3.shape[1]
    tm = _tile(n, 8 * LANES)
    slots = PEER_HEADS * PEER_TOPK
    out = jax.ShapeDtypeStruct((n, slots), F32)
    return pl.pallas_call(
        functools.partial(_peer_topk_kernel, tm=tm),
        grid=(n // tm,),
        in_specs=[
            pl.BlockSpec((2 * PEER_HEADS, tm, PEER_HALF), lambda i: (0, i, 0)),
            pl.BlockSpec((2 * PEER_HEADS, N_KEYS, PEER_HALF), lambda i: (0, 0, 0)),
        ],
        out_specs=[pl.BlockSpec((tm, slots), lambda i: (i, 0))] * 3,
        out_shape=[out, out, out],
        scratch_shapes=[pltpu.VMEM((N_KEYS * tm // LANES, LANES), F32)] * 2
        + [pltpu.VMEM((slots * tm // LANES, LANES), F32)] * 3,
        compiler_params=_params(("parallel",)),
        name="peer_topk",
    )(q3, keys)


def _peer_wbuild_kernel(g_ref, ei_ref, ej_ref, wt_ref, w_sc, *, tm, pitch):
    nt = (((1,), (1,)), ((), ()))
    sub = lax.broadcasted_iota(jnp.int32, (N_KEYS, LANES), 0).astype(F32)

    def tok_body(t, carry):
        g_r = g_ref[pl.ds(t, 1), :]
        at = jnp.where(sub == ei_ref[pl.ds(t, 1), :], g_r, 0.0).astype(BF16)
        bt = jnp.where(sub == ej_ref[pl.ds(t, 1), :], 1.0, 0.0).astype(BF16)
        w_t = lax.dot_general(at, bt, nt, preferred_element_type=F32)
        w_sc[pl.ds(t, N_KEYS, stride=pitch), :] = w_t
        return carry

    lax.fori_loop(0, tm, tok_body, 0, unroll=True)

    for i in range(N_KEYS):
        wt_ref[i] = w_sc[i * pitch:i * pitch + tm, :].astype(BF16)


def _peer_wbuild(g, ei, ej):
    n = g.shape[0]
    tm = _tile(n, 128)
    pitch = tm + 4
    row = pl.BlockSpec((tm, PEER_HEADS * PEER_TOPK), lambda i: (i, 0))
    return pl.pallas_call(
        functools.partial(_peer_wbuild_kernel, tm=tm, pitch=pitch),
        grid=(n // tm,),
        in_specs=[row, row, row],
        out_specs=pl.BlockSpec((N_KEYS, tm, N_KEYS), lambda i: (0, i, 0)),
        out_shape=jax.ShapeDtypeStruct((N_KEYS, n, N_KEYS), BF16),
        scratch_shapes=[pltpu.VMEM((N_KEYS * pitch, N_KEYS), F32)],
        compiler_params=_params(("parallel",)),
        name="peer_wbuild",
    )(g, ei, ej)


def _peer_dense_kernel(xn_ref, ut_ref, wt_ref, v_ref, x_ref, o_ref, acc_sc, *, ce):
    c = pl.program_id(1)

    @pl.when(c == 0)
    def _():
        acc_sc[...] = jnp.zeros(acc_sc.shape, F32)

    a = jnp.dot(xn_ref[...], ut_ref[...], preferred_element_type=F32)
    act = 0.5 * a * (1.0 + lax.erf(a * math.sqrt(0.5)))
    h = jnp.concatenate([(wt_ref[s].astype(F32) * act[:, s * LANES:(s + 1) * LANES]).astype(BF16)
                         for s in range(ce // LANES)], axis=1)
    acc_sc[...] += jnp.dot(h, v_ref[...], preferred_element_type=F32)

    @pl.when(c == pl.num_programs(1) - 1)
    def _():
        o_ref[...] = x_ref[...] + acc_sc[...]


def _peer_dense(xn, ut, wt, v, x2d):
    n = x2d.shape[0]
    tm = _tile(n, 512)
    ce = DENSE_CE
    kern = functools.partial(_peer_dense_kernel, ce=ce)
    return pl.pallas_call(
        kern,
        grid=(n // tm, N_EXPERTS // ce),
        in_specs=[
            pl.BlockSpec((tm, D_MODEL), lambda i, c: (i, 0)),
            pl.BlockSpec((D_MODEL, ce), lambda i, c: (0, c)),
            pl.BlockSpec((ce // N_KEYS, tm, N_KEYS), lambda i, c: (c, i, 0)),
            pl.BlockSpec((ce, D_MODEL), lambda i, c: (c, 0)),
            pl.BlockSpec((tm, D_MODEL), lambda i, c: (i, 0)),
        ],
        out_specs=pl.BlockSpec((tm, D_MODEL), lambda i, c: (i, 0)),
        out_shape=jax.ShapeDtypeStruct((n, D_MODEL), F32),
        scratch_shapes=[pltpu.VMEM((tm, D_MODEL), F32)],
        compiler_params=_params(("parallel", "arbitrary")),
        name="peer_dense",
    )(xn, ut, wt, v, x2d)


def _rope_tiles(seq):
    inv = 1.0 / (ROPE_THETA ** (jnp.arange(0, DA_DH, 2, dtype=F32) / DA_DH))
    ang = jnp.arange(seq, dtype=F32)[:, None] * inv[None, :]
    cos, sin = jnp.cos(ang), jnp.sin(ang)
    cos_t = jnp.tile(cos, (1, 4))
    sign = jnp.tile(jnp.concatenate([-jnp.ones((32,), F32), jnp.ones((32,), F32)]), 2)
    sin_t = jnp.tile(sin, (1, 4)) * sign[None, :]
    return cos_t, sin_t


def _prep_layer(l, mix_norm, w_in, da_q_norm, da_k_norm, da_lambda, da_out_norm, wa_q_norm, wa_k_norm,
                wa_sink, mem_norm, w_mem_kv, ma_q_norm, ma_k_norm, w_branch, w_out, ffn_norm,
                peer_wq, peer_keys, peer_u, peer_v):
    w = w_in[l]
    pad = jnp.zeros((D_MODEL, 256), F32)
    w_r = jnp.concatenate([
        w[:, DA_Q_OFF:DA_V_OFF],
        w[:, WA_Q_OFF:MA_Q_OFF], pad,
        w[:, MA_Q_OFF:GATE_OFF], w[:, DA_V_OFF:WA_Q_OFF],
        w[:, GATE_OFF:]], axis=1).astype(BF16)
    t2 = lambda g: jnp.tile(g.astype(F32), 2)
    head_gains = jnp.stack([
        t2(da_q_norm[l]) * (DA_DH ** -0.5 * math.log2(math.e)), t2(da_k_norm[l]),
        t2(wa_q_norm[l]) * (WA_DH ** -0.5 * math.log2(math.e)), t2(wa_k_norm[l]),
        ma_q_norm[l].astype(F32) * MA_DH ** -0.5,
        jnp.zeros((LANES,), F32), jnp.zeros((LANES,), F32), jnp.zeros((LANES,), F32)])
    return dict(
        mixg=mix_norm[l][None, :], w_r=w_r, head_gains=head_gains,
        da_lambda=da_lambda[l], og=da_out_norm[l][None, :],
        sink=jnp.broadcast_to(wa_sink[l].astype(F32)[:, None], (WA_HEADS, LANES)),
        mem_g=mem_norm[l][None, :], w_kv=w_mem_kv[l].astype(BF16), ma_kg=ma_k_norm[l][None, :],
        wb=w_branch[l].astype(BF16), wo=w_out[l].astype(BF16),
        ffn_g=ffn_norm[l][None, :], peer_wq=peer_wq[l].astype(BF16),
        keys=peer_keys[l].reshape(2 * PEER_HEADS, N_KEYS, PEER_HALF).astype(BF16),
        ut=peer_u[l].astype(BF16).T, v=peer_v[l].astype(BF16),
    )


def _trunk(x, mem, layers):
    nb, seq, _ = x.shape
    mlen = mem.shape[1]
    x2d = x.reshape(nb * seq, D_MODEL)
    mem2d = mem.reshape(nb * mlen, D_MODEL)
    cos_t, sin_t = _rope_tiles(seq)
    for l, p in enumerate(layers):
        lam_init = 0.8 - 0.6 * math.exp(-0.3 * l)
        dq, dk, dv, wq, wkd, wvd, mq, gates = _in_proj(x2d, seq, p["mixg"], p["w_r"], cos_t, sin_t,
                                                      p["head_gains"])
        oda = _diff_attn(dq, dk, dv, p["da_lambda"], p["og"], nb, seq, lam_init)
        owa = _window_attn(wq, wkd, wvd, p["sink"], nb, seq)
        mk, mv = _mem_kv(mem2d, mlen, p["mem_g"], p["w_kv"], p["ma_kg"])
        oma = _mem_attn(mq, mk, mv, nb, seq, mlen)
        x1 = _merge(x2d, oda, owa, oma, gates, p["wb"], p["wo"])
        xn, q3 = _peer_q(x1, p["ffn_g"], p["peer_wq"])
        wt = _peer_wbuild(*_peer_topk(q3, p["keys"]))
        x2d = _peer_dense(xn, p["ut"], wt, p["v"], x1)
    return x2d.reshape(nb, seq, D_MODEL)


def kernel(x_prompt, x_sample, mem_prompt, mem_sample, mix_norm, w_in, da_q_norm, da_k_norm, da_lambda, da_out_norm, wa_q_norm, wa_k_norm, wa_sink, mem_norm, w_mem_kv, ma_q_norm, ma_k_norm, w_branch, w_out, ffn_norm, peer_wq, peer_keys, peer_u, peer_v):
    weights = (mix_norm, w_in, da_q_norm, da_k_norm, da_lambda, da_out_norm, wa_q_norm, wa_k_norm,
               wa_sink, mem_norm, w_mem_kv, ma_q_norm, ma_k_norm, w_branch, w_out, ffn_norm,
               peer_wq, peer_keys, peer_u, peer_v)
    layers = [_prep_layer(l, *weights) for l in range(w_in.shape[0])]
    y_prompt = _trunk(x_prompt, mem_prompt, layers)
    y_sample = _trunk(x_sample, mem_sample, layers)
    return (y_prompt, y_sample)
```

```python
import functools
import math

import jax
import jax.numpy as jnp
import numpy as np
from jax import lax
from jax.experimental import pallas as pl
from jax.experimental.pallas import tpu as pltpu

F32 = jnp.float32
BF16 = jnp.bfloat16

D_MODEL = 1024
NORM_EPS = 1e-6
NEG_INF = -1e30
ROPE_THETA = 10000.0
LANES = 128

DA_HEADS, DA_DH = 4, 64
WA_HEADS, WA_KV_HEADS, WA_DH, WINDOW = 8, 2, 64, 128
MA_HEADS, MA_DH = 4, 128
N_BRANCH, BRANCH_WIDTH = 3, 512
PEER_HEADS, N_KEYS, PEER_HALF, PEER_TOPK = 8, 128, 128, 16
N_EXPERTS = N_KEYS * N_KEYS

DA_Q_OFF = 0
DA_K_OFF = 512
DA_V_OFF = 1024
WA_Q_OFF = 1536
WA_K_OFF = 2048
WA_V_OFF = 2176
MA_Q_OFF = 2304
GATE_OFF = 2816
IN_CHUNK = 1024
N_IN_CHUNKS = 6

VMEM_LIMIT = 48 * 1024 * 1024
DIFF_TQ, DIFF_TK = 512, 2048
DENSE_CE = 2048
DV_ROWS = 144


def _tile(n, pref):
    t = min(pref, n)
    t -= t % LANES
    while n % t:
        t -= LANES
    return t


def _params(sem):
    return pltpu.CompilerParams(dimension_semantics=sem, vmem_limit_bytes=VMEM_LIMIT)


def _lane_iota(shape):
    return lax.broadcasted_iota(jnp.int32, shape, len(shape) - 1)


def _rms(xf, gain_row, width):
    ms = jnp.sum(xf * xf, axis=-1, keepdims=True) * (1.0 / width)
    return xf * lax.rsqrt(ms + NORM_EPS) * gain_row


def _seg64_norm_rope(zt, gain_row, cosb, sinb):
    lane = _lane_iota(zt.shape)
    lo = lane < 64
    sq = zt * zt
    s0 = jnp.sum(jnp.where(lo, sq, 0.0), axis=-1, keepdims=True)
    s1 = jnp.sum(jnp.where(lo, 0.0, sq), axis=-1, keepdims=True)
    ms = jnp.where(lo, s0, s1) * (1.0 / 64)
    y = zt * lax.rsqrt(ms + NORM_EPS) * gain_row
    first = (lane % 64) < 32
    rot = jnp.where(first, pltpu.roll(y, 96, 1), pltpu.roll(y, 32, 1))
    return y * cosb + rot * sinb


def _in_proj_kernel(x_ref, mixg_ref, w_ref, cos_ref, sin_ref, hg_ref,
                    dq_ref, dk_ref, dv_ref, wq_ref, wkd_ref, wvd_ref, mq_ref, gate_ref):
    xn = _rms(x_ref[...], mixg_ref[...], D_MODEL).astype(BF16)
    cosb = cos_ref[...]
    sinb = sin_ref[...]

    def chunk(c):
        return jnp.dot(xn, w_ref[:, c * IN_CHUNK:(c + 1) * IN_CHUNK], preferred_element_type=F32)

    z = chunk(0)
    for t in range(4):
        sl = slice(t * LANES, (t + 1) * LANES)
        dq_ref[:, sl] = _seg64_norm_rope(z[:, sl], hg_ref[0:1, :], cosb, sinb).astype(BF16)
        zs = z[:, 512 + t * LANES:512 + (t + 1) * LANES]
        dk_ref[:, sl] = _seg64_norm_rope(zs, hg_ref[1:2, :], cosb, sinb).astype(BF16)

    z = chunk(1)
    for t in range(4):
        sl = slice(t * LANES, (t + 1) * LANES)
        wq_ref[:, sl] = _seg64_norm_rope(z[:, sl], hg_ref[2:3, :], cosb, sinb).astype(BF16)
    k = _seg64_norm_rope(z[:, 512:640], hg_ref[3:4, :], cosb, sinb)
    v = z[:, 640:768]
    lo = _lane_iota(k.shape) < 64
    ones = jnp.ones(k.shape, BF16)
    for src, dst, stride in ((k, wkd_ref, 1), (v, wvd_ref, 2)):
        sw = pltpu.roll(src, 64, 1)
        dst[:, 0:LANES] = jnp.where(lo, src, sw).astype(BF16)
        dst[:, stride * LANES:(stride + 1) * LANES] = jnp.where(lo, sw, src).astype(BF16)
    wvd_ref[:, LANES:2 * LANES] = ones
    wvd_ref[:, 3 * LANES:4 * LANES] = ones

    z = chunk(2)
    for t in range(4):
        sl = slice(t * LANES, (t + 1) * LANES)
        mq_ref[:, sl] = _rms(z[:, sl], hg_ref[4:5, :], MA_DH).astype(BF16)
    ones = jnp.ones((DV_ROWS - LANES, z.shape[0]), BF16)
    for t in range(DA_HEADS):
        vt = z[:, 512 + t * LANES:512 + (t + 1) * LANES].T
        dv_ref[t * DV_ROWS:t * DV_ROWS + LANES, :] = vt.astype(BF16)
        dv_ref[t * DV_ROWS + LANES:(t + 1) * DV_ROWS, :] = ones

    for c in range(3, N_IN_CHUNKS):
        gate_ref[:, (c - 3) * IN_CHUNK:(c - 2) * IN_CHUNK] = jax.nn.sigmoid(chunk(c))


def _in_proj(x2d, seq, mixg, w_r, cos_t, sin_t, head_gains):
    n = x2d.shape[0]
    tm = _tile(seq, 512)
    nsb = seq // tm
    bf = lambda w: jax.ShapeDtypeStruct((n, w), BF16)
    row = lambda w: pl.BlockSpec((tm, w), lambda i: (i, 0))
    return pl.pallas_call(
        _in_proj_kernel,
        grid=(n // tm,),
        in_specs=[
            pl.BlockSpec((tm, D_MODEL), lambda i: (i, 0)),
            pl.BlockSpec((1, D_MODEL), lambda i: (0, 0)),
            pl.BlockSpec((D_MODEL, N_IN_CHUNKS * IN_CHUNK), lambda i: (0, 0), pipeline_mode=pl.Buffered(1)),
            pl.BlockSpec((tm, LANES), lambda i: (i % nsb, 0)),
            pl.BlockSpec((tm, LANES), lambda i: (i % nsb, 0)),
            pl.BlockSpec((8, LANES), lambda i: (0, 0)),
        ],
        out_specs=[row(512), row(512),
                   pl.BlockSpec((DA_HEADS * DV_ROWS, tm), lambda i: (i // nsb, i % nsb)),
                   row(512), row(256), row(512), row(512), row(N_BRANCH * D_MODEL)],
        out_shape=[bf(512), bf(512), jax.ShapeDtypeStruct((n // seq * DA_HEADS * DV_ROWS, seq), BF16),
                   bf(512), bf(256), bf(512), bf(512),
                   jax.ShapeDtypeStruct((n, N_BRANCH * D_MODEL), F32)],
        compiler_params=_params(("parallel",)),
        name="in_proj",
    )(x2d, mixg, w_r, cos_t, sin_t, head_gains)


def _diff_attn_kernel(q_ref, qn_ref, k_ref, v_ref, lam_ref, og_ref, o_ref, m_sc, acc_sc, s_sc, cmax_sc, *, tk,
                      lam_init):
    def split(q):
        lo = _lane_iota(q.shape) < 64
        zero = jnp.zeros_like(q)
        return jnp.where(lo, q, zero), jnp.where(lo, zero, q)

    qs = split(q_ref[...])
    m_sc[...] = jnp.full(m_sc.shape, -jnp.inf, F32)
    acc_sc[...] = jnp.zeros(acc_sc.shape, F32)
    nt = (((1,), (1,)), ((), ()))
    n_chunks = k_ref.shape[0] // tk

    def scores(j, slot, qs=qs):
        off = pl.multiple_of(j * tk, tk)
        k = k_ref[pl.ds(off, tk), :]
        for c in range(2):
            s = lax.dot_general(k, qs[c], nt, preferred_element_type=F32)
            s_sc[slot, c] = s
            cmax_sc[slot, c] = jnp.max(s, axis=0, keepdims=True)

    def process(j, slot):
        off = pl.multiple_of(j * tk, tk)
        vt = v_ref[:, pl.ds(off, tk)]
        for c in range(2):
            s = s_sc[slot, c]
            m_old = m_sc[c]
            m_new = jnp.maximum(m_old, cmax_sc[slot, c])
            p = jnp.exp2(s - m_new)
            alpha = jnp.exp2(m_old - m_new)
            acc_sc[c] = alpha * acc_sc[c] + jnp.dot(vt, p.astype(BF16), preferred_element_type=F32)
            m_sc[c] = m_new

    if n_chunks == 1:
        scores(0, 0)
        process(0, 0)
    else:
        @pl.when(pl.program_id(2) == 0)
        def _():
            scores(0, 0)

        def pair(jj, carry):
            scores(2 * jj + 1, 1)
            process(2 * jj, 0)
            scores(2 * jj + 2, 0)
            process(2 * jj + 1, 1)
            return carry

        lax.fori_loop(0, n_chunks // 2 - 1, pair, 0)
        scores(n_chunks - 1, 1)
        process(n_chunks - 2, 0)
        scores(0, 0, split(qn_ref[...]))
        process(n_chunks - 1, 1)

    lp = lam_ref[...]
    lam = (jnp.exp(jnp.sum(lp[0:1] * lp[1:2], axis=-1, keepdims=True))
           - jnp.exp(jnp.sum(lp[2:3] * lp[3:4], axis=-1, keepdims=True)) + lam_init)
    ot = (acc_sc[0, :LANES, :] / acc_sc[0, LANES:LANES + 1, :]
          - lam * (acc_sc[1, :LANES, :] / acc_sc[1, LANES:LANES + 1, :]))
    o_ref[...] = (_rms(ot.T, og_ref[...], 2 * DA_DH) * (1.0 - lam_init)).astype(BF16)


def _diff_attn(dq, dk, dv, da_lambda, og, nb, seq, lam_init):
    n = dq.shape[0]
    tq = _tile(seq, DIFF_TQ)
    tk = _tile(seq, DIFF_TK)
    nq = seq // tq
    assert seq // tk == 1 or (seq // tk) % 2 == 0, (seq, tk)
    kern = functools.partial(_diff_attn_kernel, tk=tk, lam_init=lam_init)
    return pl.pallas_call(
        kern,
        grid=(nb, DA_HEADS, nq),
        in_specs=[
            pl.BlockSpec((tq, LANES), lambda b, h, i: (b * nq + i, h)),
            pl.BlockSpec((tq, LANES), lambda b, h, i: (b * nq + jnp.minimum(i + 1, nq - 1), h)),
            pl.BlockSpec((seq, LANES), lambda b, h, i: (b, h)),
            pl.BlockSpec((DV_ROWS, seq), lambda b, h, i: (b * DA_HEADS + h, 0)),
            pl.BlockSpec((4, DA_DH), lambda b, h, i: (0, 0)),
            pl.BlockSpec((1, LANES), lambda b, h, i: (0, 0)),
        ],
        out_specs=pl.BlockSpec((tq, LANES), lambda b, h, i: (b * nq + i, h)),
        out_shape=jax.ShapeDtypeStruct((n, BRANCH_WIDTH), BF16),
        scratch_shapes=[pltpu.VMEM((2, 1, tq), F32), pltpu.VMEM((2, DV_ROWS, tq), F32),
                        pltpu.VMEM((2, 2, tk, tq), F32), pltpu.VMEM((2, 2, 1, tq), F32)],
        compiler_params=_params(("parallel", "parallel", "arbitrary")),
        name="diff_attn",
    )(dq, dq, dk, dv, da_lambda, og)


def _window_attn_kernel(q_ref, kd_ref, vd_ref, sink_ref, o_ref, *, tq, seq):
    i = pl.program_id(1)
    nsub = tq // WINDOW
    kw_len = 3 * WINDOW
    nt = (((1,), (1,)), ((), ()))
    rep = WA_HEADS // WA_KV_HEADS
    lo = _lane_iota((WINDOW, LANES)) < 64
    log2e = math.log2(math.e)
    for n in range(nsub):
        gb = i * nsub + n
        start = pl.multiple_of(jnp.clip((gb - 1) * WINDOW, 0, seq - kw_len), WINDOW)
        qpos = gb * WINDOW + lax.broadcasted_iota(jnp.int32, (WINDOW, kw_len), 0)
        kpos = start + lax.broadcasted_iota(jnp.int32, (WINDOW, kw_len), 1)
        valid1 = jnp.abs(qpos - kpos) <= WINDOW
        valid = jnp.concatenate([valid1] * rep, axis=0)
        for g in range(WA_KV_HEADS):
            kg = kd_ref[pl.ds(start, kw_len), g * LANES:(g + 1) * LANES]
            vg = vd_ref[pl.ds(start, kw_len), 2 * g * LANES:2 * (g + 1) * LANES]
            tiles, sinks = [], []
            for e in range(rep):
                h = g * rep + e
                qp = q_ref[n * WINDOW:(n + 1) * WINDOW, (h // 2) * LANES:(h // 2 + 1) * LANES]
                zero = jnp.zeros_like(qp)
                tiles.append(jnp.where(lo, qp, zero) if h % 2 == 0 else jnp.where(lo, zero, qp))
                sinks.append(jnp.broadcast_to(sink_ref[h:h + 1, 0:1] * log2e, (WINDOW, 1)))
            qs = jnp.concatenate(tiles, axis=0)
            sk = jnp.concatenate(sinks, axis=0)
            s = lax.dot_general(qs, kg, nt, preferred_element_type=F32)
            s = jnp.where(valid, s, NEG_INF)
            m = jnp.maximum(jnp.max(s, axis=-1, keepdims=True), sk)
            r = jnp.dot(jnp.exp2(s - m).astype(BF16), vg, preferred_element_type=F32)
            o = r[:, :LANES] / (r[:, LANES:] + jnp.exp2(sk - m))
            for pr in range(rep // 2):
                col = (g * rep // 2 + pr) * LANES
                o_ref[n * WINDOW:(n + 1) * WINDOW, col:col + LANES] = jnp.where(
                    lo, o[2 * pr * WINDOW:(2 * pr + 1) * WINDOW], o[(2 * pr + 1) * WINDOW:(2 * pr + 2) * WINDOW]
                ).astype(BF16)


def _window_attn(wq, wkd, wvd, sink_b, nb, seq):
    n = wq.shape[0]
    tq = _tile(seq, 512)
    nq = seq // tq
    kern = functools.partial(_window_attn_kernel, tq=tq, seq=seq)
    return pl.pallas_call(
        kern,
        grid=(nb, nq),
        in_specs=[
            pl.BlockSpec((tq, 512), lambda b, i: (b * nq + i, 0)),
            pl.BlockSpec((seq, 256), lambda b, i: (b, 0)),
            pl.BlockSpec((seq, 512), lambda b, i: (b, 0)),
            pl.BlockSpec((8, LANES), lambda b, i: (0, 0)),
        ],
        out_specs=pl.BlockSpec((tq, 512), lambda b, i: (b * nq + i, 0)),
        out_shape=jax.ShapeDtypeStruct((n, BRANCH_WIDTH), BF16),
        compiler_params=_params(("parallel", "arbitrary")),
        name="window_attn",
    )(wq, wkd, wvd, sink_b)


def _mem_kv_kernel(mem_ref, mg_ref, w_ref, kg_ref, mk_ref, mv_ref):
    h = _rms(mem_ref[...], mg_ref[...], D_MODEL).astype(BF16)
    kv = jnp.dot(h, w_ref[...], preferred_element_type=F32)
    for t in range(MA_HEADS):
        sl = slice(t * LANES, (t + 1) * LANES)
        mk_ref[:, sl] = _rms(kv[:, sl], kg_ref[...], MA_DH).astype(BF16)
    mv_ref[...] = kv[:, 512:1024].astype(BF16)


def _mem_kv(mem2d, mlen, mg, w_kv, kg):
    n = mem2d.shape[0]
    return pl.pallas_call(
        _mem_kv_kernel,
        grid=(n // mlen,),
        in_specs=[
            pl.BlockSpec((mlen, D_MODEL), lambda b: (b, 0)),
            pl.BlockSpec((1, D_MODEL), lambda b: (0, 0)),
            pl.BlockSpec((D_MODEL, 2 * MA_HEADS * MA_DH), lambda b: (0, 0)),
            pl.BlockSpec((1, LANES), lambda b: (0, 0)),
        ],
        out_specs=[pl.BlockSpec((mlen, 512), lambda b: (b, 0)), pl.BlockSpec((mlen, 512), lambda b: (b, 0))],
        out_shape=[jax.ShapeDtypeStruct((n, 512), BF16), jax.ShapeDtypeStruct((n, 512), BF16)],
        compiler_params=_params(("parallel",)),
        name="mem_kv",
    )(mem2d, mg, w_kv, kg)


def _mem_attn_kernel(q_ref, mk_ref, mv_ref, o_ref):
    nt = (((1,), (1,)), ((), ()))
    for t in range(MA_HEADS):
        sl = slice(t * LANES, (t + 1) * LANES)
        s = lax.dot_general(q_ref[:, sl], mk_ref[:, sl], nt, preferred_element_type=F32)
        m = jnp.max(s, axis=-1, keepdims=True)
        ex = jnp.exp(s - m)
        p = ex / jnp.sum(ex, axis=-1, keepdims=True)
        o_ref[:, sl] = jnp.dot(p.astype(BF16), mv_ref[:, sl], preferred_element_type=F32).astype(BF16)


def _mem_attn(mq, mk, mv, nb, seq, mlen):
    n = mq.shape[0]
    tq = _tile(seq, 1024)
    nq = seq // tq
    return pl.pallas_call(
        _mem_attn_kernel,
        grid=(nb, nq),
        in_specs=[
            pl.BlockSpec((tq, 512), lambda b, i: (b * nq + i, 0)),
            pl.BlockSpec((mlen, 512), lambda b, i: (b, 0)),
            pl.BlockSpec((mlen, 512), lambda b, i: (b, 0)),
        ],
        out_specs=pl.BlockSpec((tq, 512), lambda b, i: (b * nq + i, 0)),
        out_shape=jax.ShapeDtypeStruct((n, BRANCH_WIDTH), BF16),
        compiler_params=_params(("parallel", "arbitrary")),
        name="mem_attn",
    )(mq, mk, mv)


def _merge_kernel(x_ref, oda_ref, owa_ref, oma_ref, gate_ref, wb_ref, wo_ref, o_ref):
    merged = None
    for nbr, br in enumerate((oda_ref, owa_ref, oma_ref)):
        proj = jnp.dot(br[...], wb_ref[nbr], preferred_element_type=F32)
        term = gate_ref[:, nbr * D_MODEL:(nbr + 1) * D_MODEL] * proj
        merged = term if merged is None else merged + term
    o_ref[...] = x_ref[...] + jnp.dot(merged.astype(BF16), wo_ref[...], preferred_element_type=F32)


def _merge(x2d, oda, owa, oma, gates, wb, wo):
    n = x2d.shape[0]
    tm = _tile(n, 512)
    row = lambda w: pl.BlockSpec((tm, w), lambda i: (i, 0))
    return pl.pallas_call(
        _merge_kernel,
        grid=(n // tm,),
        in_specs=[row(D_MODEL), row(512), row(512), row(512), row(N_BRANCH * D_MODEL),
                  pl.BlockSpec((N_BRANCH, BRANCH_WIDTH, D_MODEL), lambda i: (0, 0, 0)),
                  pl.BlockSpec((D_MODEL, D_MODEL), lambda i: (0, 0))],
        out_specs=row(D_MODEL),
        out_shape=jax.ShapeDtypeStruct((n, D_MODEL), F32),
        compiler_params=_params(("parallel",)),
        name="merge",
    )(x2d, oda, owa, oma, gates, wb, wo)


def _top16_lead(rows, ids):
    assert len(rows) % 2 == 0
    win_v, win_i, los_v = [], [], []
    id_sum = [ids[j] + ids[j + 1] for j in range(0, len(rows), 2)]
    for j in range(0, len(rows), 2):
        take = rows[j] >= rows[j + 1]
        win_v.append(jnp.maximum(rows[j], rows[j + 1]))
        los_v.append(jnp.minimum(rows[j], rows[j + 1]))
        win_i.append(jnp.where(take, ids[j], ids[j + 1]))
    vals, idxs = [], []
    for _ in range(PEER_TOPK):
        level = list(zip(win_v, win_i))
        while len(level) > 1:
            nxt = []
            for j in range(0, len(level) - 1, 2):
                (va, ia), (vb, ib) = level[j], level[j + 1]
                take = va >= vb
                nxt.append((jnp.maximum(va, vb), jnp.where(take, ia, ib)))
            if len(level) % 2:
                nxt.append(level[-1])
            level = nxt
        m, sel = level[0]
        for p in range(len(win_v)):
            hit = sel == win_i[p]
            win_v[p] = jnp.where(hit, los_v[p], win_v[p])
            win_i[p] = jnp.where(hit, id_sum[p] - win_i[p], win_i[p])
            los_v[p] = jnp.where(hit, -jnp.inf, los_v[p])
        vals.append(m)
        idxs.append(sel)
    return vals, idxs


def _peer_q_kernel(x_ref, fg_ref, wq_ref, xn_ref, q_ref):
    xn = _rms(x_ref[...], fg_ref[...], D_MODEL).astype(BF16)
    xn_ref[...] = xn
    q = jnp.dot(xn, wq_ref[...], preferred_element_type=F32).astype(BF16)
    for hp in range(2 * PEER_HEADS):
        q_ref[hp] = q[:, hp * LANES:(hp + 1) * LANES]


def _peer_q(x2d, fg, wq):
    n = x2d.shape[0]
    tm = _tile(n, 512)
    return pl.pallas_call(
        _peer_q_kernel,
        grid=(n // tm,),
        in_specs=[
            pl.BlockSpec((tm, D_MODEL), lambda i: (i, 0)),
            pl.BlockSpec((1, D_MODEL), lambda i: (0, 0)),
            pl.BlockSpec((D_MODEL, 2 * PEER_HEADS * PEER_HALF), lambda i: (0, 0)),
        ],
        out_specs=[pl.BlockSpec((tm, D_MODEL), lambda i: (i, 0)),
                   pl.BlockSpec((2 * PEER_HEADS, tm, PEER_HALF), lambda i: (0, i, 0))],
        out_shape=[jax.ShapeDtypeStruct((n, D_MODEL), BF16),
                   jax.ShapeDtypeStruct((2 * PEER_HEADS, n, PEER_HALF), BF16)],
        compiler_params=_params(("parallel",)),
        name="peer_q",
    )(x2d, fg, wq)


_CAND_PAIRS = tuple((a, b) for a in range(PEER_TOPK) for b in range(PEER_TOPK) if (a + 1) * (b + 1) <= PEER_TOPK)


def _peer_topk_kernel(q_ref, keys_ref, g_ref, ei_ref, ej_ref, s0_sc, s1_sc, rg_sc, ri_sc, rj_sc, *, tm):
    nt = (((1,), (1,)), ((), ()))
    gt = tm // LANES
    key_ids = [float(k) for k in range(N_KEYS)]
    cand_ids = [float(a * PEER_TOPK + b) for a, b in _CAND_PAIRS]

    def head_body(h, carry):
        sv, si = [], []
        for p, s_sc in enumerate((s0_sc, s1_sc)):
            s = lax.dot_general(keys_ref[2 * h + p], q_ref[2 * h + p], nt,
                                preferred_element_type=F32)
            for tt in range(gt):
                s_sc[pl.ds(tt, N_KEYS, stride=gt), :] = s[:, tt * LANES:(tt + 1) * LANES]
            v, ix = _top16_lead([s_sc[k * gt:(k + 1) * gt, :] for k in range(N_KEYS)], key_ids)
            sv.append(v)
            si.append(ix)
        fv, fi = _top16_lead([sv[0][a] + sv[1][b] for a, b in _CAND_PAIRS], cand_ids)
        ex = [jnp.exp(v - fv[0]) for v in fv]
        den = ex[0]
        for e in ex[1:]:
            den = den + e
        for k in range(PEER_TOPK):
            fik = fi[k].astype(jnp.int32)
            fa = fik >> 4
            fb = fik & (PEER_TOPK - 1)
            ei = jnp.zeros((gt, LANES), F32)
            ej = jnp.zeros((gt, LANES), F32)
            for a in range(PEER_TOPK):
                ei = jnp.where(fa == a, si[0][a], ei)
                ej = jnp.where(fb == a, si[1][a], ej)
            off = pl.multiple_of((h * PEER_TOPK + k) * gt, gt)
            rg_sc[pl.ds(off, gt), :] = ex[k] / den
            ri_sc[pl.ds(off, gt), :] = ei
            rj_sc[pl.ds(off, gt), :] = ej
        return carry

    lax.fori_loop(0, PEER_HEADS, head_body, 0)

    for tt in range(gt):
        sl = slice(tt * LANES, (tt + 1) * LANES)
        g_ref[sl, :] = rg_sc[pl.ds(tt, N_KEYS, stride=gt), :].T
        ei_ref[sl, :] = ri_sc[pl.ds(tt, N_KEYS, stride=gt), :].T
        ej_ref[sl, :] = rj_sc[pl.ds(tt, N_KEYS, stride=gt), :].T


def _peer_topk(q3, keys):
    n = q3.shape[1]
    tm = _tile(n, 8 * LANES)
    slots = PEER_HEADS * PEER_TOPK
    out = jax.ShapeDtypeStruct((n, slots), F32)
    return pl.pallas_call(
        functools.partial(_peer_topk_kernel, tm=tm),
        grid=(n // tm,),
        in_specs=[
            pl.BlockSpec((2 * PEER_HEADS, tm, PEER_HALF), lambda i: (0, i, 0)),
            pl.BlockSpec((2 * PEER_HEADS, N_KEYS, PEER_HALF), lambda i: (0, 0, 0)),
        ],
        out_specs=[pl.BlockSpec((tm, slots), lambda i: (i, 0))] * 3,
        out_shape=[out, out, out],
        scratch_shapes=[pltpu.VMEM((N_KEYS * tm // LANES, LANES), F32)] * 2
        + [pltpu.VMEM((slots * tm // LANES, LANES), F32)] * 3,
        compiler_params=_params(("parallel",)),
        name="peer_topk",
    )(q3, keys)


def _peer_wbuild_kernel(g_ref, ei_ref, ej_ref, wt_ref, w_sc, *, tm, pitch):
    nt = (((1,), (1,)), ((), ()))
    sub = lax.broadcasted_iota(jnp.int32, (N_KEYS, LANES), 0).astype(F32)

    def tok_body(t, carry):
        g_r = g_ref[pl.ds(t, 1), :]
        at = jnp.where(sub == ei_ref[pl.ds(t, 1), :], g_r, 0.0).astype(BF16)
        bt = jnp.where(sub == ej_ref[pl.ds(t, 1), :], 1.0, 0.0).astype(BF16)
        w_t = lax.dot_general(at, bt, nt, preferred_element_type=F32)
        w_sc[pl.ds(t, N_KEYS, stride=pitch), :] = w_t
        return carry

    lax.fori_loop(0, tm, tok_body, 0, unroll=True)

    for i in range(N_KEYS):
        wt_ref[i] = w_sc[i * pitch:i * pitch + tm, :].astype(BF16)


def _peer_wbuild(g, ei, ej):
    n = g.shape[0]
    tm = _tile(n, 128)
    pitch = tm + 4
    row = pl.BlockSpec((tm, PEER_HEADS * PEER_TOPK), lambda i: (i, 0))
    return pl.pallas_call(
        functools.partial(_peer_wbuild_kernel, tm=tm, pitch=pitch),
        grid=(n // tm,),
        in_specs=[row, row, row],
        out_specs=pl.BlockSpec((N_KEYS, tm, N_KEYS), lambda i: (0, i, 0)),
        out_shape=jax.ShapeDtypeStruct((N_KEYS, n, N_KEYS), BF16),
        scratch_shapes=[pltpu.VMEM((N_KEYS * pitch, N_KEYS), F32)],
        compiler_params=_params(("parallel",)),
        name="peer_wbuild",
    )(g, ei, ej)


def _peer_dense_kernel(xn_ref, ut_ref, wt_ref, v_ref, x_ref, o_ref, acc_sc, *, ce):
    c = pl.program_id(1)

    @pl.when(c == 0)
    def _():
        acc_sc[...] = jnp.zeros(acc_sc.shape, F32)

    a = jnp.dot(xn_ref[...], ut_ref[...], preferred_element_type=F32)
    act = 0.5 * a * (1.0 + lax.erf(a * math.sqrt(0.5)))
    h = jnp.concatenate([(wt_ref[s].astype(F32) * act[:, s * LANES:(s + 1) * LANES]).astype(BF16)
                         for s in range(ce // LANES)], axis=1)
    acc_sc[...] += jnp.dot(h, v_ref[...], preferred_element_type=F32)

    @pl.when(c == pl.num_programs(1) - 1)
    def _():
        o_ref[...] = x_ref[...] + acc_sc[...]


def _peer_dense(xn, ut, wt, v, x2d):
    n = x2d.shape[0]
    tm = _tile(n, 512)
    ce = DENSE_CE
    kern = functools.partial(_peer_dense_kernel, ce=ce)
    return pl.pallas_call(
        kern,
        grid=(n // tm, N_EXPERTS // ce),
        in_specs=[
            pl.BlockSpec((tm, D_MODEL), lambda i, c: (i, 0)),
            pl.BlockSpec((D_MODEL, ce), lambda i, c: (0, c)),
            pl.BlockSpec((ce // N_KEYS, tm, N_KEYS), lambda i, c: (c, i, 0)),
            pl.BlockSpec((ce, D_MODEL), lambda i, c: (c, 0)),
            pl.BlockSpec((tm, D_MODEL), lambda i, c: (i, 0)),
        ],
        out_specs=pl.BlockSpec((tm, D_MODEL), lambda i, c: (i, 0)),
        out_shape=jax.ShapeDtypeStruct((n, D_MODEL), F32),
        scratch_shapes=[pltpu.VMEM((tm, D_MODEL), F32)],
        compiler_params=_params(("parallel", "arbitrary")),
        name="peer_dense",
    )(xn, ut, wt, v, x2d)


def _rope_tiles(seq):
    inv = 1.0 / (ROPE_THETA ** (jnp.arange(0, DA_DH, 2, dtype=F32) / DA_DH))
    ang = jnp.arange(seq, dtype=F32)[:, None] * inv[None, :]
    cos, sin = jnp.cos(ang), jnp.sin(ang)
    cos_t = jnp.tile(cos, (1, 4))
    sign = jnp.tile(jnp.concatenate([-jnp.ones((32,), F32), jnp.ones((32,), F32)]), 2)
    sin_t = jnp.tile(sin, (1, 4)) * sign[None, :]
    return cos_t, sin_t


def _prep_layer(l, mix_norm, w_in, da_q_norm, da_k_norm, da_lambda, da_out_norm, wa_q_norm, wa_k_norm,
                wa_sink, mem_norm, w_mem_kv, ma_q_norm, ma_k_norm, w_branch, w_out, ffn_norm,
                peer_wq, peer_keys, peer_u, peer_v):
    w = w_in[l]
    pad = jnp.zeros((D_MODEL, 256), F32)
    w_r = jnp.concatenate([
        w[:, DA_Q_OFF:DA_V_OFF],
        w[:, WA_Q_OFF:MA_Q_OFF], pad,
        w[:, MA_Q_OFF:GATE_OFF], w[:, DA_V_OFF:WA_Q_OFF],
        w[:, GATE_OFF:]], axis=1).astype(BF16)
    t2 = lambda g: jnp.tile(g.astype(F32), 2)
    head_gains = jnp.stack([
        t2(da_q_norm[l]) * (DA_DH ** -0.5 * math.log2(math.e)), t2(da_k_norm[l]),
        t2(wa_q_norm[l]) * (WA_DH ** -0.5 * math.log2(math.e)), t2(wa_k_norm[l]),
        ma_q_norm[l].astype(F32) * MA_DH ** -0.5,
        jnp.zeros((LANES,), F32), jnp.zeros((LANES,), F32), jnp.zeros((LANES,), F32)])
    return dict(
        mixg=mix_norm[l][None, :], w_r=w_r, head_gains=head_gains,
        da_lambda=da_lambda[l], og=da_out_norm[l][None, :],
        sink=jnp.broadcast_to(wa_sink[l].astype(F32)[:, None], (WA_HEADS, LANES)),
        mem_g=mem_norm[l][None, :], w_kv=w_mem_kv[l].astype(BF16), ma_kg=ma_k_norm[l][None, :],
        wb=w_branch[l].astype(BF16), wo=w_out[l].astype(BF16),
        ffn_g=ffn_norm[l][None, :], peer_wq=peer_wq[l].astype(BF16),
        keys=peer_keys[l].reshape(2 * PEER_HEADS, N_KEYS, PEER_HALF).astype(BF16),
        ut=peer_u[l].astype(BF16).T, v=peer_v[l].astype(BF16),
    )


def _trunk(x, mem, layers):
    nb, seq, _ = x.shape
    mlen = mem.shape[1]
    x2d = x.reshape(nb * seq, D_MODEL)
    mem2d = mem.reshape(nb * mlen, D_MODEL)
    cos_t, sin_t = _rope_tiles(seq)
    for l, p in enumerate(layers):
        lam_init = 0.8 - 0.6 * math.exp(-0.3 * l)
        dq, dk, dv, wq, wkd, wvd, mq, gates = _in_proj(x2d, seq, p["mixg"], p["w_r"], cos_t, sin_t,
                                                      p["head_gains"])
        oda = _diff_attn(dq, dk, dv, p["da_lambda"], p["og"], nb, seq, lam_init)
        owa = _window_attn(wq, wkd, wvd, p["sink"], nb, seq)
        mk, mv = _mem_kv(mem2d, mlen, p["mem_g"], p["w_kv"], p["ma_kg"])
        oma = _mem_attn(mq, mk, mv, nb, seq, mlen)
        x1 = _merge(x2d, oda, owa, oma, gates, p["wb"], p["wo"])
        xn, q3 = _peer_q(x1, p["ffn_g"], p["peer_wq"])
        wt = _peer_wbuild(*_peer_topk(q3, p["keys"]))
        x2d = _peer_dense(xn, p["ut"], wt, p["v"], x1)
    return x2d.reshape(nb, seq, D_MODEL)


def kernel(x_prompt, x_sample, mem_prompt, mem_sample, mix_norm, w_in, da_q_norm, da_k_norm, da_lambda, da_out_norm, wa_q_norm, wa_k_norm, wa_sink, mem_norm, w_mem_kv, ma_q_norm, ma_k_norm, w_branch, w_out, ffn_norm, peer_wq, peer_keys, peer_u, peer_v):
    weights = (mix_norm, w_in, da_q_norm, da_k_norm, da_lambda, da_out_norm, wa_q_norm, wa_k_norm,
               wa_sink, mem_norm, w_mem_kv, ma_q_norm, ma_k_norm, w_branch, w_out, ffn_norm,
               peer_wq, peer_keys, peer_u, peer_v)
    layers = [_prep_layer(l, *weights) for l in range(w_in.shape[0])]
    y_prompt = _trunk(x_prompt, mem_prompt, layers)
    y_sample = _trunk(x_sample, mem_sample, layers)
    return (y_prompt, y_sample)
```

```python
import functools
import math

import jax
import jax.numpy as jnp
import numpy as np
from jax import lax
from jax.experimental import pallas as pl
from jax.experimental.pallas import tpu as pltpu

F32 = jnp.float32
BF16 = jnp.bfloat16

D_MODEL = 1024
NORM_EPS = 1e-6
NEG_INF = -1e30
ROPE_THETA = 10000.0
LANES = 128

DA_HEADS, DA_DH = 4, 64
WA_HEADS, WA_KV_HEADS, WA_DH, WINDOW = 8, 2, 64, 128
MA_HEADS, MA_DH = 4, 128
N_BRANCH, BRANCH_WIDTH = 3, 512
PEER_HEADS, N_KEYS, PEER_HALF, PEER_TOPK = 8, 128, 128, 16
N_EXPERTS = N_KEYS * N_KEYS

DA_Q_OFF = 0
DA_K_OFF = 512
DA_V_OFF = 1024
WA_Q_OFF = 1536
WA_K_OFF = 2048
WA_V_OFF = 2176
MA_Q_OFF = 2304
GATE_OFF = 2816
IN_CHUNK = 1024
N_IN_CHUNKS = 6

VMEM_LIMIT = 48 * 1024 * 1024
DIFF_TQ, DIFF_TK = 512, 2048
DENSE_CE = 2048
DV_ROWS = 144


def _tile(n, pref):
    t = min(pref, n)
    t -= t % LANES
    while n % t:
        t -= LANES
    return t


def _params(sem):
    return pltpu.CompilerParams(dimension_semantics=sem, vmem_limit_bytes=VMEM_LIMIT)


def _lane_iota(shape):
    return lax.broadcasted_iota(jnp.int32, shape, len(shape) - 1)


def _rms(xf, gain_row, width):
    ms = jnp.sum(xf * xf, axis=-1, keepdims=True) * (1.0 / width)
    return xf * lax.rsqrt(ms + NORM_EPS) * gain_row


def _seg64_norm_rope(zt, gain_row, cosb, sinb):
    lane = _lane_iota(zt.shape)
    lo = lane < 64
    sq = zt * zt
    s0 = jnp.sum(jnp.where(lo, sq, 0.0), axis=-1, keepdims=True)
    s1 = jnp.sum(jnp.where(lo, 0.0, sq), axis=-1, keepdims=True)
    ms = jnp.where(lo, s0, s1) * (1.0 / 64)
    y = zt * lax.rsqrt(ms + NORM_EPS) * gain_row
    first = (lane % 64) < 32
    rot = jnp.where(first, pltpu.roll(y, 96, 1), pltpu.roll(y, 32, 1))
    return y * cosb + rot * sinb


def _in_proj_kernel(x_ref, mixg_ref, w_ref, cos_ref, sin_ref, hg_ref,
                    dq_ref, dk_ref, dv_ref, wq_ref, wkd_ref, wvd_ref, mq_ref, gate_ref):
    xn = _rms(x_ref[...], mixg_ref[...], D_MODEL).astype(BF16)
    cosb = cos_ref[...]
    sinb = sin_ref[...]

    def chunk(c):
        return jnp.dot(xn, w_ref[:, c * IN_CHUNK:(c + 1) * IN_CHUNK], preferred_element_type=F32)

    z = chunk(0)
    for t in range(4):
        sl = slice(t * LANES, (t + 1) * LANES)
        dq_ref[:, sl] = _seg64_norm_rope(z[:, sl], hg_ref[0:1, :], cosb, sinb).astype(BF16)
        zs = z[:, 512 + t * LANES:512 + (t + 1) * LANES]
        dk_ref[:, sl] = _seg64_norm_rope(zs, hg_ref[1:2, :], cosb, sinb).astype(BF16)

    z = chunk(1)
    for t in range(4):
        sl = slice(t * LANES, (t + 1) * LANES)
        wq_ref[:, sl] = _seg64_norm_rope(z[:, sl], hg_ref[2:3, :], cosb, sinb).astype(BF16)
    k = _seg64_norm_rope(z[:, 512:640], hg_ref[3:4, :], cosb, sinb)
    v = z[:, 640:768]
    lo = _lane_iota(k.shape) < 64
    ones = jnp.ones(k.shape, BF16)
    for src, dst, stride in ((k, wkd_ref, 1), (v, wvd_ref, 2)):
        sw = pltpu.roll(src, 64, 1)
        dst[:, 0:LANES] = jnp.where(lo, src, sw).astype(BF16)
        dst[:, stride * LANES:(stride + 1) * LANES] = jnp.where(lo, sw, src).astype(BF16)
    wvd_ref[:, LANES:2 * LANES] = ones
    wvd_ref[:, 3 * LANES:4 * LANES] = ones

    z = chunk(2)
    for t in range(4):
        sl = slice(t * LANES, (t + 1) * LANES)
        mq_ref[:, sl] = _rms(z[:, sl], hg_ref[4:5, :], MA_DH).astype(BF16)
    ones = jnp.ones((DV_ROWS - LANES, z.shape[0]), BF16)
    for t in range(DA_HEADS):
        vt = z[:, 512 + t * LANES:512 + (t + 1) * LANES].T
        dv_ref[t * DV_ROWS:t * DV_ROWS + LANES, :] = vt.astype(BF16)
        dv_ref[t * DV_ROWS + LANES:(t + 1) * DV_ROWS, :] = ones

    for c in range(3, N_IN_CHUNKS):
        gate_ref[:, (c - 3) * IN_CHUNK:(c - 2) * IN_CHUNK] = jax.nn.sigmoid(chunk(c))


def _in_proj(x2d, seq, mixg, w_r, cos_t, sin_t, head_gains):
    n = x2d.shape[0]
    tm = _tile(seq, 512)
    nsb = seq // tm
    bf = lambda w: jax.ShapeDtypeStruct((n, w), BF16)
    row = lambda w: pl.BlockSpec((tm, w), lambda i: (i, 0))
    return pl.pallas_call(
        _in_proj_kernel,
        grid=(n // tm,),
        in_specs=[
            pl.BlockSpec((tm, D_MODEL), lambda i: (i, 0)),
            pl.BlockSpec((1, D_MODEL), lambda i: (0, 0)),
            pl.BlockSpec((D_MODEL, N_IN_CHUNKS * IN_CHUNK), lambda i: (0, 0), pipeline_mode=pl.Buffered(1)),
            pl.BlockSpec((tm, LANES), lambda i: (i % nsb, 0)),
            pl.BlockSpec((tm, LANES), lambda i: (i % nsb, 0)),
            pl.BlockSpec((8, LANES), lambda i: (0, 0)),
        ],
        out_specs=[row(512), row(512),
                   pl.BlockSpec((DA_HEADS * DV_ROWS, tm), lambda i: (i // nsb, i % nsb)),
                   row(512), row(256), row(512), row(512), row(N_BRANCH * D_MODEL)],
        out_shape=[bf(512), bf(512), jax.ShapeDtypeStruct((n // seq * DA_HEADS * DV_ROWS, seq), BF16),
                   bf(512), bf(256), bf(512), bf(512),
                   jax.ShapeDtypeStruct((n, N_BRANCH * D_MODEL), F32)],
        compiler_params=_params(("parallel",)),
        name="in_proj",
    )(x2d, mixg, w_r, cos_t, sin_t, head_gains)


def _diff_attn_kernel(q_ref, qn_ref, k_ref, v_ref, lam_ref, og_ref, o_ref, m_sc, acc_sc, s_sc, cmax_sc, *, tk,
                      lam_init):
    def split(q):
        lo = _lane_iota(q.shape) < 64
        zero = jnp.zeros_like(q)
        return jnp.where(lo, q, zero), jnp.where(lo, zero, q)

    qs = split(q_ref[...])
    m_sc[...] = jnp.full(m_sc.shape, -jnp.inf, F32)
    acc_sc[...] = jnp.zeros(acc_sc.shape, F32)
    nt = (((1,), (1,)), ((), ()))
    n_chunks = k_ref.shape[0] // tk

    def scores(j, slot, qs=qs):
        off = pl.multiple_of(j * tk, tk)
        k = k_ref[pl.ds(off, tk), :]
        for c in range(2):
            s = lax.dot_general(k, qs[c], nt, preferred_element_type=F32)
            s_sc[slot, c] = s
            cmax_sc[slot, c] = jnp.max(s, axis=0, keepdims=True)

    def process(j, slot):
        off = pl.multiple_of(j * tk, tk)
        vt = v_ref[:, pl.ds(off, tk)]
        for c in range(2):
            s = s_sc[slot, c]
            m_old = m_sc[c]
            m_new = jnp.maximum(m_old, cmax_sc[slot, c])
            p = jnp.exp2(s - m_new)
            alpha = jnp.exp2(m_old - m_new)
            acc_sc[c] = alpha * acc_sc[c] + jnp.dot(vt, p.astype(BF16), preferred_element_type=F32)
            m_sc[c] = m_new

    if n_chunks == 1:
        scores(0, 0)
        process(0, 0)
    else:
        @pl.when(pl.program_id(2) == 0)
        def _():
            scores(0, 0)

        def pair(jj, carry):
            scores(2 * jj + 1, 1)
            process(2 * jj, 0)
            scores(2 * jj + 2, 0)
            process(2 * jj + 1, 1)
            return carry

        lax.fori_loop(0, n_chunks // 2 - 1, pair, 0)
        scores(n_chunks - 1, 1)
        process(n_chunks - 2, 0)
        scores(0, 0, split(qn_ref[...]))
        process(n_chunks - 1, 1)

    lp = lam_ref[...]
    lam = (jnp.exp(jnp.sum(lp[0:1] * lp[1:2], axis=-1, keepdims=True))
           - jnp.exp(jnp.sum(lp[2:3] * lp[3:4], axis=-1, keepdims=True)) + lam_init)
    ot = (acc_sc[0, :LANES, :] / acc_sc[0, LANES:LANES + 1, :]
          - lam * (acc_sc[1, :LANES, :] / acc_sc[1, LANES:LANES + 1, :]))
    o_ref[...] = (_rms(ot.T, og_ref[...], 2 * DA_DH) * (1.0 - lam_init)).astype(BF16)


def _diff_attn(dq, dk, dv, da_lambda, og, nb, seq, lam_init):
    n = dq.shape[0]
    tq = _tile(seq, DIFF_TQ)
    tk = _tile(seq, DIFF_TK)
    nq = seq // tq
    assert seq // tk == 1 or (seq // tk) % 2 == 0, (seq, tk)
    kern = functools.partial(_diff_attn_kernel, tk=tk, lam_init=lam_init)
    return pl.pallas_call(
        kern,
        grid=(nb, DA_HEADS, nq),
        in_specs=[
            pl.BlockSpec((tq, LANES), lambda b, h, i: (b * nq + i, h)),
            pl.BlockSpec((tq, LANES), lambda b, h, i: (b * nq + jnp.minimum(i + 1, nq - 1), h)),
            pl.BlockSpec((seq, LANES), lambda b, h, i: (b, h)),
            pl.BlockSpec((DV_ROWS, seq), lambda b, h, i: (b * DA_HEADS + h, 0)),
            pl.BlockSpec((4, DA_DH), lambda b, h, i: (0, 0)),
            pl.BlockSpec((1, LANES), lambda b, h, i: (0, 0)),
        ],
        out_specs=pl.BlockSpec((tq, LANES), lambda b, h, i: (b * nq + i, h)),
        out_shape=jax.ShapeDtypeStruct((n, BRANCH_WIDTH), BF16),
        scratch_shapes=[pltpu.VMEM((2, 1, tq), F32), pltpu.VMEM((2, DV_ROWS, tq), F32),
                        pltpu.VMEM((2, 2, tk, tq), F32), pltpu.VMEM((2, 2, 1, tq), F32)],
        compiler_params=_params(("parallel", "parallel", "arbitrary")),
        name="diff_attn",
    )(dq, dq, dk, dv, da_lambda, og)


def _window_attn_kernel(q_ref, kd_ref, vd_ref, sink_ref, o_ref, *, tq, seq):
    i = pl.program_id(1)
    nsub = tq // WINDOW
    kw_len = 3 * WINDOW
    nt = (((1,), (1,)), ((), ()))
    rep = WA_HEADS // WA_KV_HEADS
    lo = _lane_iota((WINDOW, LANES)) < 64
    log2e = math.log2(math.e)
    for n in range(nsub):
        gb = i * nsub + n
        start = pl.multiple_of(jnp.clip((gb - 1) * WINDOW, 0, seq - kw_len), WINDOW)
        qpos = gb * WINDOW + lax.broadcasted_iota(jnp.int32, (WINDOW, kw_len), 0)
        kpos = start + lax.broadcasted_iota(jnp.int32, (WINDOW, kw_len), 1)
        valid1 = jnp.abs(qpos - kpos) <= WINDOW
        valid = jnp.concatenate([valid1] * rep, axis=0)
        for g in range(WA_KV_HEADS):
            kg = kd_ref[pl.ds(start, kw_len), g * LANES:(g + 1) * LANES]
            vg = vd_ref[pl.ds(start, kw_len), 2 * g * LANES:2 * (g + 1) * LANES]
            tiles, sinks = [], []
            for e in range(rep):
                h = g * rep + e
                qp = q_ref[n * WINDOW:(n + 1) * WINDOW, (h // 2) * LANES:(h // 2 + 1) * LANES]
                zero = jnp.zeros_like(qp)
                tiles.append(jnp.where(lo, qp, zero) if h % 2 == 0 else jnp.where(lo, zero, qp))
                sinks.append(jnp.broadcast_to(sink_ref[h:h + 1, 0:1] * log2e, (WINDOW, 1)))
            qs = jnp.concatenate(tiles, axis=0)
            sk = jnp.concatenate(sinks, axis=0)
            s = lax.dot_general(qs, kg, nt, preferred_element_type=F32)
            s = jnp.where(valid, s, NEG_INF)
            m = jnp.maximum(jnp.max(s, axis=-1, keepdims=True), sk)
            r = jnp.dot(jnp.exp2(s - m).astype(BF16), vg, preferred_element_type=F32)
            o = r[:, :LANES] / (r[:, LANES:] + jnp.exp2(sk - m))
            for pr in range(rep // 2):
                col = (g * rep // 2 + pr) * LANES
                o_ref[n * WINDOW:(n + 1) * WINDOW, col:col + LANES] = jnp.where(
                    lo, o[2 * pr * WINDOW:(2 * pr + 1) * WINDOW], o[(2 * pr + 1) * WINDOW:(2 * pr + 2) * WINDOW]
                ).astype(BF16)


def _window_attn(wq, wkd, wvd, sink_b, nb, seq):
    n = wq.shape[0]
    tq = _tile(seq, 512)
    nq = seq // tq
    kern = functools.partial(_window_attn_kernel, tq=tq, seq=seq)
    return pl.pallas_call(
        kern,
        grid=(nb, nq),
        in_specs=[
            pl.BlockSpec((tq, 512), lambda b, i: (b * nq + i, 0)),
            pl.BlockSpec((seq, 256), lambda b, i: (b, 0)),
            pl.BlockSpec((seq, 512), lambda b, i: (b, 0)),
            pl.BlockSpec((8, LANES), lambda b, i: (0, 0)),
        ],
        out_specs=pl.BlockSpec((tq, 512), lambda b, i: (b * nq + i, 0)),
        out_shape=jax.ShapeDtypeStruct((n, BRANCH_WIDTH), BF16),
        compiler_params=_params(("parallel", "arbitrary")),
        name="window_attn",
    )(wq, wkd, wvd, sink_b)


def _mem_kv_kernel(mem_ref, mg_ref, w_ref, kg_ref, mk_ref, mv_ref):
    h = _rms(mem_ref[...], mg_ref[...], D_MODEL).astype(BF16)
    kv = jnp.dot(h, w_ref[...], preferred_element_type=F32)
    for t in range(MA_HEADS):
        sl = slice(t * LANES, (t + 1) * LANES)
        mk_ref[:, sl] = _rms(kv[:, sl], kg_ref[...], MA_DH).astype(BF16)
    mv_ref[...] = kv[:, 512:1024].astype(BF16)


def _mem_kv(mem2d, mlen, mg, w_kv, kg):
    n = mem2d.shape[0]
    return pl.pallas_call(
        _mem_kv_kernel,
        grid=(n // mlen,),
        in_specs=[
            pl.BlockSpec((mlen, D_MODEL), lambda b: (b, 0)),
            pl.BlockSpec((1, D_MODEL), lambda b: (0, 0)),
            pl.BlockSpec((D_MODEL, 2 * MA_HEADS * MA_DH), lambda b: (0, 0)),
            pl.BlockSpec((1, LANES), lambda b: (0, 0)),
        ],
        out_specs=[pl.BlockSpec((mlen, 512), lambda b: (b, 0)), pl.BlockSpec((mlen, 512), lambda b: (b, 0))],
        out_shape=[jax.ShapeDtypeStruct((n, 512), BF16), jax.ShapeDtypeStruct((n, 512), BF16)],
        compiler_params=_params(("parallel",)),
        name="mem_kv",
    )(mem2d, mg, w_kv, kg)


def _mem_attn_kernel(q_ref, mk_ref, mv_ref, o_ref):
    nt = (((1,), (1,)), ((), ()))
    for t in range(MA_HEADS):
        sl = slice(t * LANES, (t + 1) * LANES)
        s = lax.dot_general(q_ref[:, sl], mk_ref[:, sl], nt, preferred_element_type=F32)
        m = jnp.max(s, axis=-1, keepdims=True)
        ex = jnp.exp(s - m)
        p = ex / jnp.sum(ex, axis=-1, keepdims=True)
        o_ref[:, sl] = jnp.dot(p.astype(BF16), mv_ref[:, sl], preferred_element_type=F32).astype(BF16)


def _mem_attn(mq, mk, mv, nb, seq, mlen):
    n = mq.shape[0]
    tq = _tile(seq, 1024)
    nq = seq // tq
    return pl.pallas_call(
        _mem_attn_kernel,
        grid=(nb, nq),
        in_specs=[
            pl.BlockSpec((tq, 512), lambda b, i: (b * nq + i, 0)),
            pl.BlockSpec((mlen, 512), lambda b, i: (b, 0)),
            pl.BlockSpec((mlen, 512), lambda b, i: (b, 0)),
        ],
        out_specs=pl.BlockSpec((tq, 512), lambda b, i: (b * nq + i, 0)),
        out_shape=jax.ShapeDtypeStruct((n, BRANCH_WIDTH), BF16),
        compiler_params=_params(("parallel", "arbitrary")),
        name="mem_attn",
    )(mq, mk, mv)


def _merge_kernel(x_ref, oda_ref, owa_ref, oma_ref, gate_ref, wb_ref, wo_ref, o_ref):
    merged = None
    for nbr, br in enumerate((oda_ref, owa_ref, oma_ref)):
        proj = jnp.dot(br[...], wb_ref[nbr], preferred_element_type=F32)
        term = gate_ref[:, nbr * D_MODEL:(nbr + 1) * D_MODEL] * proj
        merged = term if merged is None else merged + term
    o_ref[...] = x_ref[...] + jnp.dot(merged.astype(BF16), wo_ref[...], preferred_element_type=F32)


def _merge(x2d, oda, owa, oma, gates, wb, wo):
    n = x2d.shape[0]
    tm = _tile(n, 512)
    row = lambda w: pl.BlockSpec((tm, w), lambda i: (i, 0))
    return pl.pallas_call(
        _merge_kernel,
        grid=(n // tm,),
        in_specs=[row(D_MODEL), row(512), row(512), row(512), row(N_BRANCH * D_MODEL),
                  pl.BlockSpec((N_BRANCH, BRANCH_WIDTH, D_MODEL), lambda i: (0, 0, 0)),
                  pl.BlockSpec((D_MODEL, D_MODEL), lambda i: (0, 0))],
        out_specs=row(D_MODEL),
        out_shape=jax.ShapeDtypeStruct((n, D_MODEL), F32),
        compiler_params=_params(("parallel",)),
        name="merge",
    )(x2d, oda, owa, oma, gates, wb, wo)


_SORT_NETS = {
    2: ((0, 1, True),),
    4: ((0, 1, True), (2, 3, True), (0, 2, True), (1, 3, True), (1, 2, False)),
}


def _top16_lead(rows, ids, group):
    assert len(rows) % group == 0 and group in _SORT_NETS
    gv, gi = [], []
    for j in range(0, len(rows), group):
        vs, js = list(rows[j:j + group]), list(ids[j:j + group])
        for a, b, left_has_lower_ids in _SORT_NETS[group]:
            va, vb, ia, ib = vs[a], vs[b], js[a], js[b]
            take = (va >= vb) if left_has_lower_ids else ((va > vb) | ((va == vb) & (ia < ib)))
            vs[a], vs[b] = jnp.maximum(va, vb), jnp.minimum(va, vb)
            js[a], js[b] = jnp.where(take, ia, ib), jnp.where(take, ib, ia)
        gv.append(vs)
        gi.append(js)
    vals, idxs = [], []
    for _ in range(PEER_TOPK):
        level = [(v[0], i[0]) for v, i in zip(gv, gi)]
        while len(level) > 1:
            nxt = []
            for j in range(0, len(level) - 1, 2):
                (va, ia), (vb, ib) = level[j], level[j + 1]
                take = va >= vb
                nxt.append((jnp.maximum(va, vb), jnp.where(take, ia, ib)))
            if len(level) % 2:
                nxt.append(level[-1])
            level = nxt
        m, sel = level[0]
        for vs, js in zip(gv, gi):
            hit = sel == js[0]
            for k in range(group - 1):
                vs[k] = jnp.where(hit, vs[k + 1], vs[k])
                js[k] = jnp.where(hit, js[k + 1], js[k])
            vs[group - 1] = jnp.where(hit, -jnp.inf, vs[group - 1])
        vals.append(m)
        idxs.append(sel)
    return vals, idxs


def _peer_q_kernel(x_ref, fg_ref, wq_ref, xn_ref, q_ref):
    xn = _rms(x_ref[...], fg_ref[...], D_MODEL).astype(BF16)
    xn_ref[...] = xn
    q = jnp.dot(xn, wq_ref[...], preferred_element_type=F32).astype(BF16)
    for hp in range(2 * PEER_HEADS):
        q_ref[hp] = q[:, hp * LANES:(hp + 1) * LANES]


def _peer_q(x2d, fg, wq):
    n = x2d.shape[0]
    tm = _tile(n, 512)
    return pl.pallas_call(
        _peer_q_kernel,
        grid=(n // tm,),
        in_specs=[
            pl.BlockSpec((tm, D_MODEL), lambda i: (i, 0)),
            pl.BlockSpec((1, D_MODEL), lambda i: (0, 0)),
            pl.BlockSpec((D_MODEL, 2 * PEER_HEADS * PEER_HALF), lambda i: (0, 0)),
        ],
        out_specs=[pl.BlockSpec((tm, D_MODEL), lambda i: (i, 0)),
                   pl.BlockSpec((2 * PEER_HEADS, tm, PEER_HALF), lambda i: (0, i, 0))],
        out_shape=[jax.ShapeDtypeStruct((n, D_MODEL), BF16),
                   jax.ShapeDtypeStruct((2 * PEER_HEADS, n, PEER_HALF), BF16)],
        compiler_params=_params(("parallel",)),
        name="peer_q",
    )(x2d, fg, wq)


_CAND_PAIRS = tuple((a, b) for a in range(PEER_TOPK) for b in range(PEER_TOPK) if (a + 1) * (b + 1) <= PEER_TOPK)


def _peer_topk_kernel(q_ref, keys_ref, g_ref, ei_ref, ej_ref, s0_sc, s1_sc, rg_sc, ri_sc, rj_sc, *, tm):
    nt = (((1,), (1,)), ((), ()))
    gt = tm // LANES
    key_ids = [float(k) for k in range(N_KEYS)]
    cand_ids = [float(a * PEER_TOPK + b) for a, b in _CAND_PAIRS]

    def head_body(h, carry):
        sv, si = [], []
        for p, s_sc in enumerate((s0_sc, s1_sc)):
            s = lax.dot_general(keys_ref[2 * h + p], q_ref[2 * h + p], nt,
                                preferred_element_type=F32)
            for tt in range(gt):
                s_sc[pl.ds(tt, N_KEYS, stride=gt), :] = s[:, tt * LANES:(tt + 1) * LANES]
            v, ix = _top16_lead([s_sc[k * gt:(k + 1) * gt, :] for k in range(N_KEYS)], key_ids, 4)
            sv.append(v)
            si.append(ix)
        fv, fi = _top16_lead([sv[0][a] + sv[1][b] for a, b in _CAND_PAIRS], cand_ids, 2)
        ex = [jnp.exp(v - fv[0]) for v in fv]
        den = ex[0]
        for e in ex[1:]:
            den = den + e
        for k in range(PEER_TOPK):
            fik = fi[k].astype(jnp.int32)
            fa = fik >> 4
            fb = fik & (PEER_TOPK - 1)
            ei = jnp.zeros((gt, LANES), F32)
            ej = jnp.zeros((gt, LANES), F32)
            for a in range(PEER_TOPK):
                ei = jnp.where(fa == a, si[0][a], ei)
                ej = jnp.where(fb == a, si[1][a], ej)
            off = pl.multiple_of((h * PEER_TOPK + k) * gt, gt)
            rg_sc[pl.ds(off, gt), :] = ex[k] / den
            ri_sc[pl.ds(off, gt), :] = ei
            rj_sc[pl.ds(off, gt), :] = ej
        return carry

    lax.fori_loop(0, PEER_HEADS, head_body, 0)

    for tt in range(gt):
        sl = slice(tt * LANES, (tt + 1) * LANES)
        g_ref[sl, :] = rg_sc[pl.ds(tt, N_KEYS, stride=gt), :].T
        ei_ref[sl, :] = ri_sc[pl.ds(tt, N_KEYS, stride=gt), :].T
        ej_ref[sl, :] = rj_sc[pl.ds(tt, N_KEYS, stride=gt), :].T


def _peer_topk(q3, keys):
    n = q3.shape[1]
    tm = _tile(n, 8 * LANES)
    slots = PEER_HEADS * PEER_TOPK
    out = jax.ShapeDtypeStruct((n, slots), F32)
    return pl.pallas_call(
        functools.partial(_peer_topk_kernel, tm=tm),
        grid=(n // tm,),
        in_specs=[
            pl.BlockSpec((2 * PEER_HEADS, tm, PEER_HALF), lambda i: (0, i, 0)),
            pl.BlockSpec((2 * PEER_HEADS, N_KEYS, PEER_HALF), lambda i: (0, 0, 0)),
        ],
        out_specs=[pl.BlockSpec((tm, slots), lambda i: (i, 0))] * 3,
        out_shape=[out, out, out],
        scratch_shapes=[pltpu.VMEM((N_KEYS * tm // LANES, LANES), F32)] * 2
        + [pltpu.VMEM((slots * tm // LANES, LANES), F32)] * 3,
        compiler_params=_params(("parallel",)),
        name="peer_topk",
    )(q3, keys)


def _peer_wbuild_kernel(g_ref, ei_ref, ej_ref, wt_ref, w_sc, *, tm, pitch):
    nt = (((1,), (1,)), ((), ()))
    sub = lax.broadcasted_iota(jnp.int32, (N_KEYS, LANES), 0).astype(F32)

    def tok_body(t, carry):
        g_r = g_ref[pl.ds(t, 1), :]
        at = jnp.where(sub == ei_ref[pl.ds(t, 1), :], g_r, 0.0).astype(BF16)
        bt = jnp.where(sub == ej_ref[pl.ds(t, 1), :], 1.0, 0.0).astype(BF16)
        w_t = lax.dot_general(at, bt, nt, preferred_element_type=F32)
        w_sc[pl.ds(t, N_KEYS, stride=pitch), :] = w_t
        return carry

    lax.fori_loop(0, tm, tok_body, 0, unroll=True)

    for i in range(N_KEYS):
        wt_ref[0, i] = w_sc[i * pitch:i * pitch + tm, :].astype(BF16)


def _peer_wbuild(g, ei, ej):
    n = g.shape[0]
    tm = _tile(n, 128)
    pitch = tm + 4
    row = pl.BlockSpec((tm, PEER_HEADS * PEER_TOPK), lambda i: (i, 0))
    return pl.pallas_call(
        functools.partial(_peer_wbuild_kernel, tm=tm, pitch=pitch),
        grid=(n // tm,),
        in_specs=[row, row, row],
        out_specs=pl.BlockSpec((1, N_KEYS, tm, N_KEYS), lambda i: (i, 0, 0, 0)),
        out_shape=jax.ShapeDtypeStruct((n // tm, N_KEYS, tm, N_KEYS), BF16),
        scratch_shapes=[pltpu.VMEM((N_KEYS * pitch, N_KEYS), F32)],
        compiler_params=_params(("parallel",)),
        name="peer_wbuild",
    )(g, ei, ej)


def _peer_dense_kernel(xn_ref, ut_ref, wt_ref, v_ref, x_ref, o_ref, acc_sc, *, ce):
    c = pl.program_id(1)

    @pl.when(c == 0)
    def _():
        acc_sc[...] = jnp.zeros(acc_sc.shape, F32)

    a = jnp.dot(xn_ref[...], ut_ref[...], preferred_element_type=F32)
    act = 0.5 * a * (1.0 + lax.erf(a * math.sqrt(0.5)))
    cols = []
    for s in range(ce // LANES):
        w = jnp.concatenate([wt_ref[b, s] for b in range(wt_ref.shape[0])], axis=0)
        cols.append((w.astype(F32) * act[:, s * LANES:(s + 1) * LANES]).astype(BF16))
    h = jnp.concatenate(cols, axis=1)
    acc_sc[...] += jnp.dot(h, v_ref[...], preferred_element_type=F32)

    @pl.when(c == pl.num_programs(1) - 1)
    def _():
        o_ref[...] = x_ref[...] + acc_sc[...]


def _peer_dense(xn, ut, wt, v, x2d):
    n = x2d.shape[0]
    tw = wt.shape[2]
    tm = _tile(n, 512)
    assert tm % tw == 0, (tm, tw)
    ce = DENSE_CE
    kern = functools.partial(_peer_dense_kernel, ce=ce)
    return pl.pallas_call(
        kern,
        grid=(n // tm, N_EXPERTS // ce),
        in_specs=[
            pl.BlockSpec((tm, D_MODEL), lambda i, c: (i, 0)),
            pl.BlockSpec((D_MODEL, ce), lambda i, c: (0, c)),
            pl.BlockSpec((tm // tw, ce // N_KEYS, tw, N_KEYS), lambda i, c: (i, c, 0, 0)),
            pl.BlockSpec((ce, D_MODEL), lambda i, c: (c, 0)),
            pl.BlockSpec((tm, D_MODEL), lambda i, c: (i, 0)),
        ],
        out_specs=pl.BlockSpec((tm, D_MODEL), lambda i, c: (i, 0)),
        out_shape=jax.ShapeDtypeStruct((n, D_MODEL), F32),
        scratch_shapes=[pltpu.VMEM((tm, D_MODEL), F32)],
        compiler_params=_params(("parallel", "arbitrary")),
        name="peer_dense",
    )(xn, ut, wt, v, x2d)


def _rope_tiles(seq):
    inv = 1.0 / (ROPE_THETA ** (jnp.arange(0, DA_DH, 2, dtype=F32) / DA_DH))
    ang = jnp.arange(seq, dtype=F32)[:, None] * inv[None, :]
    cos, sin = jnp.cos(ang), jnp.sin(ang)
    cos_t = jnp.tile(cos, (1, 4))
    sign = jnp.tile(jnp.concatenate([-jnp.ones((32,), F32), jnp.ones((32,), F32)]), 2)
    sin_t = jnp.tile(sin, (1, 4)) * sign[None, :]
    return cos_t, sin_t


def _prep_layer(l, mix_norm, w_in, da_q_norm, da_k_norm, da_lambda, da_out_norm, wa_q_norm, wa_k_norm,
                wa_sink, mem_norm, w_mem_kv, ma_q_norm, ma_k_norm, w_branch, w_out, ffn_norm,
                peer_wq, peer_keys, peer_u, peer_v):
    w = w_in[l]
    pad = jnp.zeros((D_MODEL, 256), F32)
    w_r = jnp.concatenate([
        w[:, DA_Q_OFF:DA_V_OFF],
        w[:, WA_Q_OFF:MA_Q_OFF], pad,
        w[:, MA_Q_OFF:GATE_OFF], w[:, DA_V_OFF:WA_Q_OFF],
        w[:, GATE_OFF:]], axis=1).astype(BF16)
    t2 = lambda g: jnp.tile(g.astype(F32), 2)
    head_gains = jnp.stack([
        t2(da_q_norm[l]) * (DA_DH ** -0.5 * math.log2(math.e)), t2(da_k_norm[l]),
        t2(wa_q_norm[l]) * (WA_DH ** -0.5 * math.log2(math.e)), t2(wa_k_norm[l]),
        ma_q_norm[l].astype(F32) * MA_DH ** -0.5,
        jnp.zeros((LANES,), F32), jnp.zeros((LANES,), F32), jnp.zeros((LANES,), F32)])
    return dict(
        mixg=mix_norm[l][None, :], w_r=w_r, head_gains=head_gains,
        da_lambda=da_lambda[l], og=da_out_norm[l][None, :],
        sink=jnp.broadcast_to(wa_sink[l].astype(F32)[:, None], (WA_HEADS, LANES)),
        mem_g=mem_norm[l][None, :], w_kv=w_mem_kv[l].astype(BF16), ma_kg=ma_k_norm[l][None, :],
        wb=w_branch[l].astype(BF16), wo=w_out[l].astype(BF16),
        ffn_g=ffn_norm[l][None, :], peer_wq=peer_wq[l].astype(BF16),
        keys=peer_keys[l].reshape(2 * PEER_HEADS, N_KEYS, PEER_HALF).astype(BF16),
        ut=peer_u[l].astype(BF16).T, v=peer_v[l].astype(BF16),
    )


def _trunk(x, mem, layers):
    nb, seq, _ = x.shape
    mlen = mem.shape[1]
    x2d = x.reshape(nb * seq, D_MODEL)
    mem2d = mem.reshape(nb * mlen, D_MODEL)
    cos_t, sin_t = _rope_tiles(seq)
    for l, p in enumerate(layers):
        lam_init = 0.8 - 0.6 * math.exp(-0.3 * l)
        dq, dk, dv, wq, wkd, wvd, mq, gates = _in_proj(x2d, seq, p["mixg"], p["w_r"], cos_t, sin_t,
                                                      p["head_gains"])
        oda = _diff_attn(dq, dk, dv, p["da_lambda"], p["og"], nb, seq, lam_init)
        owa = _window_attn(wq, wkd, wvd, p["sink"], nb, seq)
        mk, mv = _mem_kv(mem2d, mlen, p["mem_g"], p["w_kv"], p["ma_kg"])
        oma = _mem_attn(mq, mk, mv, nb, seq, mlen)
        x1 = _merge(x2d, oda, owa, oma, gates, p["wb"], p["wo"])
        xn, q3 = _peer_q(x1, p["ffn_g"], p["peer_wq"])
        wt = _peer_wbuild(*_peer_topk(q3, p["keys"]))
        x2d = _peer_dense(xn, p["ut"], wt, p["v"], x1)
    return x2d.reshape(nb, seq, D_MODEL)


def kernel(x_prompt, x_sample, mem_prompt, mem_sample, mix_norm, w_in, da_q_norm, da_k_norm, da_lambda, da_out_norm, wa_q_norm, wa_k_norm, wa_sink, mem_norm, w_mem_kv, ma_q_norm, ma_k_norm, w_branch, w_out, ffn_norm, peer_wq, peer_keys, peer_u, peer_v):
    weights = (mix_norm, w_in, da_q_norm, da_k_norm, da_lambda, da_out_norm, wa_q_norm, wa_k_norm,
               wa_sink, mem_norm, w_mem_kv, ma_q_norm, ma_k_norm, w_branch, w_out, ffn_norm,
               peer_wq, peer_keys, peer_u, peer_v)
    layers = [_prep_layer(l, *weights) for l in range(w_in.shape[0])]
    y_prompt = _trunk(x_prompt, mem_prompt, layers)
    y_sample = _trunk(x_sample, mem_sample, layers)
    return (y_prompt, y_sample)
```

```python
import functools
import math

import jax
import jax.numpy as jnp
import numpy as np
from jax import lax
from jax.experimental import pallas as pl
from jax.experimental.pallas import tpu as pltpu

F32 = jnp.float32
BF16 = jnp.bfloat16

D_MODEL = 1024
NORM_EPS = 1e-6
NEG_INF = -1e30
ROPE_THETA = 10000.0
LANES = 128

DA_HEADS, DA_DH = 4, 64
WA_HEADS, WA_KV_HEADS, WA_DH, WINDOW = 8, 2, 64, 128
MA_HEADS, MA_DH = 4, 128
N_BRANCH, BRANCH_WIDTH = 3, 512
PEER_HEADS, N_KEYS, PEER_HALF, PEER_TOPK = 8, 128, 128, 16
N_EXPERTS = N_KEYS * N_KEYS

DA_Q_OFF = 0
DA_K_OFF = 512
DA_V_OFF = 1024
WA_Q_OFF = 1536
WA_K_OFF = 2048
WA_V_OFF = 2176
MA_Q_OFF = 2304
GATE_OFF = 2816
IN_CHUNK = 1024
N_IN_CHUNKS = 6

VMEM_LIMIT = 48 * 1024 * 1024
DIFF_TQ, DIFF_TK = 512, 2048
DENSE_CE = 2048
DV_ROWS = 144
WV_ROWS = 80


def _tile(n, pref):
    t = min(pref, n)
    t -= t % LANES
    while n % t:
        t -= LANES
    return t


def _params(sem):
    return pltpu.CompilerParams(dimension_semantics=sem, vmem_limit_bytes=VMEM_LIMIT)


def _lane_iota(shape):
    return lax.broadcasted_iota(jnp.int32, shape, len(shape) - 1)


def _rms(xf, gain_row, width):
    ms = jnp.sum(xf * xf, axis=-1, keepdims=True) * (1.0 / width)
    return xf * lax.rsqrt(ms + NORM_EPS) * gain_row


def _seg64_norm_rope(zt, gain_row, cosb, sinb):
    lane = _lane_iota(zt.shape)
    lo = lane < 64
    sq = zt * zt
    s0 = jnp.sum(jnp.where(lo, sq, 0.0), axis=-1, keepdims=True)
    s1 = jnp.sum(jnp.where(lo, 0.0, sq), axis=-1, keepdims=True)
    ms = jnp.where(lo, s0, s1) * (1.0 / 64)
    y = zt * lax.rsqrt(ms + NORM_EPS) * gain_row
    first = (lane % 64) < 32
    rot = jnp.where(first, pltpu.roll(y, 96, 1), pltpu.roll(y, 32, 1))
    return y * cosb + rot * sinb


def _in_proj_kernel(x_ref, mixg_ref, w_ref, cos_ref, sin_ref, hg_ref,
                    dq_ref, dk_ref, dv_ref, wq_ref, wkd_ref, wvt_ref, mq_ref, gate_ref):
    xn = _rms(x_ref[...], mixg_ref[...], D_MODEL).astype(BF16)
    cosb = cos_ref[...]
    sinb = sin_ref[...]

    def chunk(c):
        return jnp.dot(xn, w_ref[:, c * IN_CHUNK:(c + 1) * IN_CHUNK], preferred_element_type=F32)

    z = chunk(0)
    for t in range(4):
        sl = slice(t * LANES, (t + 1) * LANES)
        dq_ref[:, sl] = _seg64_norm_rope(z[:, sl], hg_ref[0:1, :], cosb, sinb).astype(BF16)
        zs = z[:, 512 + t * LANES:512 + (t + 1) * LANES]
        dk_ref[:, sl] = _seg64_norm_rope(zs, hg_ref[1:2, :], cosb, sinb).astype(BF16)

    z = chunk(1)
    for t in range(4):
        sl = slice(t * LANES, (t + 1) * LANES)
        wq_ref[:, sl] = _seg64_norm_rope(z[:, sl], hg_ref[2:3, :], cosb, sinb).astype(BF16)
    k = _seg64_norm_rope(z[:, 512:640], hg_ref[3:4, :], cosb, sinb)
    lo = _lane_iota(k.shape) < 64
    sw = pltpu.roll(k, 64, 1)
    wkd_ref[:, 0:LANES] = jnp.where(lo, k, sw).astype(BF16)
    wkd_ref[:, LANES:2 * LANES] = jnp.where(lo, sw, k).astype(BF16)
    vt = z[:, 640:768].T
    ones = jnp.ones((WV_ROWS - WA_DH, z.shape[0]), BF16)
    for g in range(WA_KV_HEADS):
        wvt_ref[g * WV_ROWS:g * WV_ROWS + WA_DH, :] = vt[g * WA_DH:(g + 1) * WA_DH].astype(BF16)
        wvt_ref[g * WV_ROWS + WA_DH:(g + 1) * WV_ROWS, :] = ones

    z = chunk(2)
    for t in range(4):
        sl = slice(t * LANES, (t + 1) * LANES)
        mq_ref[:, sl] = _rms(z[:, sl], hg_ref[4:5, :], MA_DH).astype(BF16)
    ones = jnp.ones((DV_ROWS - LANES, z.shape[0]), BF16)
    for t in range(DA_HEADS):
        vt = z[:, 512 + t * LANES:512 + (t + 1) * LANES].T
        dv_ref[t * DV_ROWS:t * DV_ROWS + LANES, :] = vt.astype(BF16)
        dv_ref[t * DV_ROWS + LANES:(t + 1) * DV_ROWS, :] = ones

    for c in range(3, N_IN_CHUNKS):
        gate_ref[:, (c - 3) * IN_CHUNK:(c - 2) * IN_CHUNK] = jax.nn.sigmoid(chunk(c))


def _in_proj(x2d, seq, mixg, w_r, cos_t, sin_t, head_gains):
    n = x2d.shape[0]
    tm = _tile(seq, 512)
    nsb = seq // tm
    bf = lambda w: jax.ShapeDtypeStruct((n, w), BF16)
    row = lambda w: pl.BlockSpec((tm, w), lambda i: (i, 0))
    return pl.pallas_call(
        _in_proj_kernel,
        grid=(n // tm,),
        in_specs=[
            pl.BlockSpec((tm, D_MODEL), lambda i: (i, 0)),
            pl.BlockSpec((1, D_MODEL), lambda i: (0, 0)),
            pl.BlockSpec((D_MODEL, N_IN_CHUNKS * IN_CHUNK), lambda i: (0, 0), pipeline_mode=pl.Buffered(1)),
            pl.BlockSpec((tm, LANES), lambda i: (i % nsb, 0)),
            pl.BlockSpec((tm, LANES), lambda i: (i % nsb, 0)),
            pl.BlockSpec((8, LANES), lambda i: (0, 0)),
        ],
        out_specs=[row(512), row(512),
                   pl.BlockSpec((DA_HEADS * DV_ROWS, tm), lambda i: (i // nsb, i % nsb)),
                   row(512), row(256),
                   pl.BlockSpec((WA_KV_HEADS * WV_ROWS, tm), lambda i: (i // nsb, i % nsb)),
                   row(512), row(N_BRANCH * D_MODEL)],
        out_shape=[bf(512), bf(512), jax.ShapeDtypeStruct((n // seq * DA_HEADS * DV_ROWS, seq), BF16),
                   bf(512), bf(256), jax.ShapeDtypeStruct((n // seq * WA_KV_HEADS * WV_ROWS, seq), BF16), bf(512),
                   jax.ShapeDtypeStruct((n, N_BRANCH * D_MODEL), F32)],
        compiler_params=_params(("parallel",)),
        name="in_proj",
    )(x2d, mixg, w_r, cos_t, sin_t, head_gains)


def _diff_attn_kernel(q_ref, qn_ref, k_ref, v_ref, lam_ref, og_ref, o_ref, m_sc, acc_sc, s_sc, cmax_sc, *, tk,
                      lam_init):
    def split(q):
        lo = _lane_iota(q.shape) < 64
        zero = jnp.zeros_like(q)
        return jnp.where(lo, q, zero), jnp.where(lo, zero, q)

    qs = split(q_ref[...])
    m_sc[...] = jnp.full(m_sc.shape, -jnp.inf, F32)
    acc_sc[...] = jnp.zeros(acc_sc.shape, F32)
    nt = (((1,), (1,)), ((), ()))
    n_chunks = k_ref.shape[0] // tk

    def scores(j, slot, qs=qs):
        off = pl.multiple_of(j * tk, tk)
        k = k_ref[pl.ds(off, tk), :]
        for c in range(2):
            s = lax.dot_general(k, qs[c], nt, preferred_element_type=F32)
            s_sc[slot, c] = s
            cmax_sc[slot, c] = jnp.max(s, axis=0, keepdims=True)

    def process(j, slot):
        off = pl.multiple_of(j * tk, tk)
        vt = v_ref[:, pl.ds(off, tk)]
        for c in range(2):
            s = s_sc[slot, c]
            m_old = m_sc[c]
            m_new = jnp.maximum(m_old, cmax_sc[slot, c])
            p = jnp.exp2(s - m_new)
            alpha = jnp.exp2(m_old - m_new)
            acc_sc[c] = alpha * acc_sc[c] + jnp.dot(vt, p.astype(BF16), preferred_element_type=F32)
            m_sc[c] = m_new

    if n_chunks == 1:
        scores(0, 0)
        process(0, 0)
    else:
        @pl.when(pl.program_id(2) == 0)
        def _():
            scores(0, 0)

        def pair(jj, carry):
            scores(2 * jj + 1, 1)
            process(2 * jj, 0)
            scores(2 * jj + 2, 0)
            process(2 * jj + 1, 1)
            return carry

        lax.fori_loop(0, n_chunks // 2 - 1, pair, 0)
        scores(n_chunks - 1, 1)
        process(n_chunks - 2, 0)
        scores(0, 0, split(qn_ref[...]))
        process(n_chunks - 1, 1)

    lp = lam_ref[...]
    lam = (jnp.exp(jnp.sum(lp[0:1] * lp[1:2], axis=-1, keepdims=True))
           - jnp.exp(jnp.sum(lp[2:3] * lp[3:4], axis=-1, keepdims=True)) + lam_init)
    ot = (acc_sc[0, :LANES, :] / acc_sc[0, LANES:LANES + 1, :]
          - lam * (acc_sc[1, :LANES, :] / acc_sc[1, LANES:LANES + 1, :]))
    o_ref[...] = (_rms(ot.T, og_ref[...], 2 * DA_DH) * (1.0 - lam_init)).astype(BF16)


def _diff_attn(dq, dk, dv, da_lambda, og, nb, seq, lam_init):
    n = dq.shape[0]
    tq = _tile(seq, DIFF_TQ)
    tk = _tile(seq, DIFF_TK)
    nq = seq // tq
    assert seq // tk == 1 or (seq // tk) % 2 == 0, (seq, tk)
    kern = functools.partial(_diff_attn_kernel, tk=tk, lam_init=lam_init)
    return pl.pallas_call(
        kern,
        grid=(nb, DA_HEADS, nq),
        in_specs=[
            pl.BlockSpec((tq, LANES), lambda b, h, i: (b * nq + i, h)),
            pl.BlockSpec((tq, LANES), lambda b, h, i: (b * nq + jnp.minimum(i + 1, nq - 1), h)),
            pl.BlockSpec((seq, LANES), lambda b, h, i: (b, h)),
            pl.BlockSpec((DV_ROWS, seq), lambda b, h, i: (b * DA_HEADS + h, 0)),
            pl.BlockSpec((4, DA_DH), lambda b, h, i: (0, 0)),
            pl.BlockSpec((1, LANES), lambda b, h, i: (0, 0)),
        ],
        out_specs=pl.BlockSpec((tq, LANES), lambda b, h, i: (b * nq + i, h)),
        out_shape=jax.ShapeDtypeStruct((n, BRANCH_WIDTH), BF16),
        scratch_shapes=[pltpu.VMEM((2, 1, tq), F32), pltpu.VMEM((2, DV_ROWS, tq), F32),
                        pltpu.VMEM((2, 2, tk, tq), F32), pltpu.VMEM((2, 2, 1, tq), F32)],
        compiler_params=_params(("parallel", "parallel", "arbitrary")),
        name="diff_attn",
    )(dq, dq, dk, dv, da_lambda, og)


def _window_attn_kernel(q_ref, kd_ref, vt_ref, sink_ref, o_ref, *, tq, seq):
    i = pl.program_id(1)
    nsub = tq // WINDOW
    kw_len = 3 * WINDOW
    nt = (((1,), (1,)), ((), ()))
    rep = WA_HEADS // WA_KV_HEADS
    lo = _lane_iota((WINDOW, LANES)) < 64
    log2e = math.log2(math.e)
    for n in range(nsub):
        gb = i * nsub + n
        start = pl.multiple_of(jnp.clip((gb - 1) * WINDOW, 0, seq - kw_len), WINDOW)
        kpos = start + lax.broadcasted_iota(jnp.int32, (kw_len, WINDOW), 0)
        qpos = gb * WINDOW + lax.broadcasted_iota(jnp.int32, (kw_len, WINDOW), 1)
        valid1 = jnp.abs(qpos - kpos) <= WINDOW
        valid = jnp.concatenate([valid1] * rep, axis=1)
        for g in range(WA_KV_HEADS):
            kg = kd_ref[pl.ds(start, kw_len), g * LANES:(g + 1) * LANES]
            vt = vt_ref[g * WV_ROWS:(g + 1) * WV_ROWS, pl.ds(start, kw_len)]
            tiles, sinks = [], []
            for e in range(rep):
                h = g * rep + e
                qp = q_ref[n * WINDOW:(n + 1) * WINDOW, (h // 2) * LANES:(h // 2 + 1) * LANES]
                zero = jnp.zeros_like(qp)
                tiles.append(jnp.where(lo, qp, zero) if h % 2 == 0 else jnp.where(lo, zero, qp))
                sinks.append(jnp.broadcast_to(sink_ref[h:h + 1, 0:1] * log2e, (1, WINDOW)))
            qs = jnp.concatenate(tiles, axis=0)
            sk = jnp.concatenate(sinks, axis=1)
            s = lax.dot_general(kg, qs, nt, preferred_element_type=F32)
            s = jnp.where(valid, s, NEG_INF)
            m = jnp.maximum(jnp.max(s, axis=0, keepdims=True), sk)
            r = jnp.dot(vt, jnp.exp2(s - m).astype(BF16), preferred_element_type=F32)
            ot = r[:WA_DH] / (r[WA_DH:WA_DH + 1] + jnp.exp2(sk - m))
            for pr in range(rep // 2):
                col = (g * rep // 2 + pr) * LANES
                pair = jnp.concatenate([ot[:, 2 * pr * WINDOW:(2 * pr + 1) * WINDOW],
                                        ot[:, (2 * pr + 1) * WINDOW:(2 * pr + 2) * WINDOW]], axis=0)
                o_ref[n * WINDOW:(n + 1) * WINDOW, col:col + LANES] = pair.T.astype(BF16)


def _window_attn(wq, wkd, wvd, sink_b, nb, seq):
    n = wq.shape[0]
    tq = _tile(seq, 512)
    nq = seq // tq
    kern = functools.partial(_window_attn_kernel, tq=tq, seq=seq)
    return pl.pallas_call(
        kern,
        grid=(nb, nq),
        in_specs=[
            pl.BlockSpec((tq, 512), lambda b, i: (b * nq + i, 0)),
            pl.BlockSpec((seq, 256), lambda b, i: (b, 0)),
            pl.BlockSpec((WA_KV_HEADS * WV_ROWS, seq), lambda b, i: (b, 0)),
            pl.BlockSpec((8, LANES), lambda b, i: (0, 0)),
        ],
        out_specs=pl.BlockSpec((tq, 512), lambda b, i: (b * nq + i, 0)),
        out_shape=jax.ShapeDtypeStruct((n, BRANCH_WIDTH), BF16),
        compiler_params=_params(("parallel", "arbitrary")),
        name="window_attn",
    )(wq, wkd, wvd, sink_b)


def _mem_kv_kernel(mem_ref, mg_ref, w_ref, kg_ref, mk_ref, mv_ref):
    h = _rms(mem_ref[...], mg_ref[...], D_MODEL).astype(BF16)
    kv = jnp.dot(h, w_ref[...], preferred_element_type=F32)
    for t in range(MA_HEADS):
        sl = slice(t * LANES, (t + 1) * LANES)
        mk_ref[:, sl] = _rms(kv[:, sl], kg_ref[...], MA_DH).astype(BF16)
    mv_ref[...] = kv[:, 512:1024].astype(BF16)


def _mem_kv(mem2d, mlen, mg, w_kv, kg):
    n = mem2d.shape[0]
    return pl.pallas_call(
        _mem_kv_kernel,
        grid=(n // mlen,),
        in_specs=[
            pl.BlockSpec((mlen, D_MODEL), lambda b: (b, 0)),
            pl.BlockSpec((1, D_MODEL), lambda b: (0, 0)),
            pl.BlockSpec((D_MODEL, 2 * MA_HEADS * MA_DH), lambda b: (0, 0)),
            pl.BlockSpec((1, LANES), lambda b: (0, 0)),
        ],
        out_specs=[pl.BlockSpec((mlen, 512), lambda b: (b, 0)), pl.BlockSpec((mlen, 512), lambda b: (b, 0))],
        out_shape=[jax.ShapeDtypeStruct((n, 512), BF16), jax.ShapeDtypeStruct((n, 512), BF16)],
        compiler_params=_params(("parallel",)),
        name="mem_kv",
    )(mem2d, mg, w_kv, kg)


def _mem_attn_kernel(q_ref, mk_ref, mv_ref, o_ref):
    nt = (((1,), (1,)), ((), ()))
    for t in range(MA_HEADS):
        sl = slice(t * LANES, (t + 1) * LANES)
        s = lax.dot_general(q_ref[:, sl], mk_ref[:, sl], nt, preferred_element_type=F32)
        m = jnp.max(s, axis=-1, keepdims=True)
        ex = jnp.exp(s - m)
        p = ex / jnp.sum(ex, axis=-1, keepdims=True)
        o_ref[:, sl] = jnp.dot(p.astype(BF16), mv_ref[:, sl], preferred_element_type=F32).astype(BF16)


def _mem_attn(mq, mk, mv, nb, seq, mlen):
    n = mq.shape[0]
    tq = _tile(seq, 1024)
    nq = seq // tq
    return pl.pallas_call(
        _mem_attn_kernel,
        grid=(nb, nq),
        in_specs=[
            pl.BlockSpec((tq, 512), lambda b, i: (b * nq + i, 0)),
            pl.BlockSpec((mlen, 512), lambda b, i: (b, 0)),
            pl.BlockSpec((mlen, 512), lambda b, i: (b, 0)),
        ],
        out_specs=pl.BlockSpec((tq, 512), lambda b, i: (b * nq + i, 0)),
        out_shape=jax.ShapeDtypeStruct((n, BRANCH_WIDTH), BF16),
        compiler_params=_params(("parallel", "arbitrary")),
        name="mem_attn",
    )(mq, mk, mv)


def _merge_kernel(x_ref, oda_ref, owa_ref, oma_ref, gate_ref, wb_ref, wo_ref, o_ref):
    merged = None
    for nbr, br in enumerate((oda_ref, owa_ref, oma_ref)):
        proj = jnp.dot(br[...], wb_ref[nbr], preferred_element_type=F32)
        term = gate_ref[:, nbr * D_MODEL:(nbr + 1) * D_MODEL] * proj
        merged = term if merged is None else merged + term
    o_ref[...] = x_ref[...] + jnp.dot(merged.astype(BF16), wo_ref[...], preferred_element_type=F32)


def _merge(x2d, oda, owa, oma, gates, wb, wo):
    n = x2d.shape[0]
    tm = _tile(n, 512)
    row = lambda w: pl.BlockSpec((tm, w), lambda i: (i, 0))
    return pl.pallas_call(
        _merge_kernel,
        grid=(n // tm,),
        in_specs=[row(D_MODEL), row(512), row(512), row(512), row(N_BRANCH * D_MODEL),
                  pl.BlockSpec((N_BRANCH, BRANCH_WIDTH, D_MODEL), lambda i: (0, 0, 0)),
                  pl.BlockSpec((D_MODEL, D_MODEL), lambda i: (0, 0))],
        out_specs=row(D_MODEL),
        out_shape=jax.ShapeDtypeStruct((n, D_MODEL), F32),
        compiler_params=_params(("parallel",)),
        name="merge",
    )(x2d, oda, owa, oma, gates, wb, wo)


_SORT_NETS = {
    2: ((0, 1, True),),
    4: ((0, 1, True), (2, 3, True), (0, 2, True), (1, 3, True), (1, 2, False)),
}


def _top16_lead(rows, ids, group):
    assert len(rows) % group == 0 and group in _SORT_NETS
    gv, gi = [], []
    for j in range(0, len(rows), group):
        vs, js = list(rows[j:j + group]), list(ids[j:j + group])
        for a, b, left_has_lower_ids in _SORT_NETS[group]:
            va, vb, ia, ib = vs[a], vs[b], js[a], js[b]
            take = (va >= vb) if left_has_lower_ids else ((va > vb) | ((va == vb) & (ia < ib)))
            vs[a], vs[b] = jnp.maximum(va, vb), jnp.minimum(va, vb)
            js[a], js[b] = jnp.where(take, ia, ib), jnp.where(take, ib, ia)
        gv.append(vs)
        gi.append(js)
    vals, idxs = [], []
    for _ in range(PEER_TOPK):
        level = [(v[0], i[0]) for v, i in zip(gv, gi)]
        while len(level) > 1:
            nxt = []
            for j in range(0, len(level) - 1, 2):
                (va, ia), (vb, ib) = level[j], level[j + 1]
                take = va >= vb
                nxt.append((jnp.maximum(va, vb), jnp.where(take, ia, ib)))
            if len(level) % 2:
                nxt.append(level[-1])
            level = nxt
        m, sel = level[0]
        for vs, js in zip(gv, gi):
            hit = sel == js[0]
            for k in range(group - 1):
                vs[k] = jnp.where(hit, vs[k + 1], vs[k])
                js[k] = jnp.where(hit, js[k + 1], js[k])
            vs[group - 1] = jnp.where(hit, -jnp.inf, vs[group - 1])
        vals.append(m)
        idxs.append(sel)
    return vals, idxs


def _peer_q_kernel(x_ref, fg_ref, wq_ref, xn_ref, q_ref):
    xn = _rms(x_ref[...], fg_ref[...], D_MODEL).astype(BF16)
    xn_ref[...] = xn
    q = jnp.dot(xn, wq_ref[...], preferred_element_type=F32).astype(BF16)
    for hp in range(2 * PEER_HEADS):
        q_ref[hp] = q[:, hp * LANES:(hp + 1) * LANES]


def _peer_q(x2d, fg, wq):
    n = x2d.shape[0]
    tm = _tile(n, 512)
    return pl.pallas_call(
        _peer_q_kernel,
        grid=(n // tm,),
        in_specs=[
            pl.BlockSpec((tm, D_MODEL), lambda i: (i, 0)),
            pl.BlockSpec((1, D_MODEL), lambda i: (0, 0)),
            pl.BlockSpec((D_MODEL, 2 * PEER_HEADS * PEER_HALF), lambda i: (0, 0)),
        ],
        out_specs=[pl.BlockSpec((tm, D_MODEL), lambda i: (i, 0)),
                   pl.BlockSpec((2 * PEER_HEADS, tm, PEER_HALF), lambda i: (0, i, 0))],
        out_shape=[jax.ShapeDtypeStruct((n, D_MODEL), BF16),
                   jax.ShapeDtypeStruct((2 * PEER_HEADS, n, PEER_HALF), BF16)],
        compiler_params=_params(("parallel",)),
        name="peer_q",
    )(x2d, fg, wq)


_CAND_PAIRS = tuple((a, b) for a in range(PEER_TOPK) for b in range(PEER_TOPK) if (a + 1) * (b + 1) <= PEER_TOPK)


def _peer_topk_kernel(q_ref, keys_ref, g_ref, ei_ref, ej_ref, s0_sc, s1_sc, rg_sc, ri_sc, rj_sc, *, tm):
    nt = (((1,), (1,)), ((), ()))
    gt = tm // LANES
    key_ids = [float(k) for k in range(N_KEYS)]
    cand_ids = [float(a * PEER_TOPK + b) for a, b in _CAND_PAIRS]

    def head_body(h, carry):
        sv, si = [], []
        for p, s_sc in enumerate((s0_sc, s1_sc)):
            s = lax.dot_general(keys_ref[2 * h + p], q_ref[2 * h + p], nt,
                                preferred_element_type=F32)
            for tt in range(gt):
                s_sc[pl.ds(tt, N_KEYS, stride=gt), :] = s[:, tt * LANES:(tt + 1) * LANES]
            v, ix = _top16_lead([s_sc[k * gt:(k + 1) * gt, :] for k in range(N_KEYS)], key_ids, 4)
            sv.append(v)
            si.append(ix)
        fv, fi = _top16_lead([sv[0][a] + sv[1][b] for a, b in _CAND_PAIRS], cand_ids, 2)
        ex = [jnp.exp(v - fv[0]) for v in fv]
        den = ex[0]
        for e in ex[1:]:
            den = den + e
        for k in range(PEER_TOPK):
            fik = fi[k].astype(jnp.int32)
            fa = fik >> 4
            fb = fik & (PEER_TOPK - 1)
            ei = jnp.zeros((gt, LANES), F32)
            ej = jnp.zeros((gt, LANES), F32)
            for a in range(PEER_TOPK):
                ei = jnp.where(fa == a, si[0][a], ei)
                ej = jnp.where(fb == a, si[1][a], ej)
            off = pl.multiple_of((h * PEER_TOPK + k) * gt, gt)
            rg_sc[pl.ds(off, gt), :] = ex[k] / den
            ri_sc[pl.ds(off, gt), :] = ei
            rj_sc[pl.ds(off, gt), :] = ej
        return carry

    lax.fori_loop(0, PEER_HEADS, head_body, 0)

    for tt in range(gt):
        sl = slice(tt * LANES, (tt + 1) * LANES)
        g_ref[sl, :] = rg_sc[pl.ds(tt, N_KEYS, stride=gt), :].T
        ei_ref[sl, :] = ri_sc[pl.ds(tt, N_KEYS, stride=gt), :].T
        ej_ref[sl, :] = rj_sc[pl.ds(tt, N_KEYS, stride=gt), :].T


def _peer_topk(q3, keys):
    n = q3.shape[1]
    tm = _tile(n, 8 * LANES)
    slots = PEER_HEADS * PEER_TOPK
    out = jax.ShapeDtypeStruct((n, slots), F32)
    return pl.pallas_call(
        functools.partial(_peer_topk_kernel, tm=tm),
        grid=(n // tm,),
        in_specs=[
            pl.BlockSpec((2 * PEER_HEADS, tm, PEER_HALF), lambda i: (0, i, 0)),
            pl.BlockSpec((2 * PEER_HEADS, N_KEYS, PEER_HALF), lambda i: (0, 0, 0)),
        ],
        out_specs=[pl.BlockSpec((tm, slots), lambda i: (i, 0))] * 3,
        out_shape=[out, out, out],
        scratch_shapes=[pltpu.VMEM((N_KEYS * tm // LANES, LANES), F32)] * 2
        + [pltpu.VMEM((slots * tm // LANES, LANES), F32)] * 3,
        compiler_params=_params(("parallel",)),
        name="peer_topk",
    )(q3, keys)


def _peer_wbuild_kernel(g_ref, ei_ref, ej_ref, wt_ref, w_sc, *, tm, pitch):
    nt = (((1,), (1,)), ((), ()))
    sub = lax.broadcasted_iota(jnp.int32, (N_KEYS, LANES), 0).astype(F32)

    def tok_body(t, carry):
        g_r = g_ref[pl.ds(t, 1), :]
        at = jnp.where(sub == ei_ref[pl.ds(t, 1), :], g_r, 0.0).astype(BF16)
        bt = jnp.where(sub == ej_ref[pl.ds(t, 1), :], 1.0, 0.0).astype(BF16)
        w_t = lax.dot_general(at, bt, nt, preferred_element_type=F32)
        w_sc[pl.ds(t, N_KEYS, stride=pitch), :] = w_t
        return carry

    lax.fori_loop(0, tm, tok_body, 0, unroll=True)

    for i in range(N_KEYS):
        wt_ref[0, i] = w_sc[i * pitch:i * pitch + tm, :].astype(BF16)


def _peer_wbuild(g, ei, ej):
    n = g.shape[0]
    tm = _tile(n, 128)
    pitch = tm + 4
    row = pl.BlockSpec((tm, PEER_HEADS * PEER_TOPK), lambda i: (i, 0))
    return pl.pallas_call(
        functools.partial(_peer_wbuild_kernel, tm=tm, pitch=pitch),
        grid=(n // tm,),
        in_specs=[row, row, row],
        out_specs=pl.BlockSpec((1, N_KEYS, tm, N_KEYS), lambda i: (i, 0, 0, 0)),
        out_shape=jax.ShapeDtypeStruct((n // tm, N_KEYS, tm, N_KEYS), BF16),
        scratch_shapes=[pltpu.VMEM((N_KEYS * pitch, N_KEYS), F32)],
        compiler_params=_params(("parallel",)),
        name="peer_wbuild",
    )(g, ei, ej)


def _peer_dense_kernel(xn_ref, ut_ref, wt_ref, v_ref, x_ref, o_ref, acc_sc, *, ce):
    c = pl.program_id(1)

    @pl.when(c == 0)
    def _():
        acc_sc[...] = jnp.zeros(acc_sc.shape, F32)

    a = jnp.dot(xn_ref[...], ut_ref[...], preferred_element_type=F32)
    act = 0.5 * a * (1.0 + lax.erf(a * math.sqrt(0.5)))
    cols = []
    for s in range(ce // LANES):
        w = jnp.concatenate([wt_ref[b, s] for b in range(wt_ref.shape[0])], axis=0)
        cols.append((w.astype(F32) * act[:, s * LANES:(s + 1) * LANES]).astype(BF16))
    h = jnp.concatenate(cols, axis=1)
    acc_sc[...] += jnp.dot(h, v_ref[...], preferred_element_type=F32)

    @pl.when(c == pl.num_programs(1) - 1)
    def _():
        o_ref[...] = x_ref[...] + acc_sc[...]


def _peer_dense(xn, ut, wt, v, x2d):
    n = x2d.shape[0]
    tw = wt.shape[2]
    tm = _tile(n, 512)
    assert tm % tw == 0, (tm, tw)
    ce = DENSE_CE
    kern = functools.partial(_peer_dense_kernel, ce=ce)
    return pl.pallas_call(
        kern,
        grid=(n // tm, N_EXPERTS // ce),
        in_specs=[
            pl.BlockSpec((tm, D_MODEL), lambda i, c: (i, 0)),
            pl.BlockSpec((D_MODEL, ce), lambda i, c: (0, c)),
            pl.BlockSpec((tm // tw, ce // N_KEYS, tw, N_KEYS), lambda i, c: (i, c, 0, 0)),
            pl.BlockSpec((ce, D_MODEL), lambda i, c: (c, 0)),
            pl.BlockSpec((tm, D_MODEL), lambda i, c: (i, 0)),
        ],
        out_specs=pl.BlockSpec((tm, D_MODEL), lambda i, c: (i, 0)),
        out_shape=jax.ShapeDtypeStruct((n, D_MODEL), F32),
        scratch_shapes=[pltpu.VMEM((tm, D_MODEL), F32)],
        compiler_params=_params(("parallel", "arbitrary")),
        name="peer_dense",
    )(xn, ut, wt, v, x2d)


def _rope_tiles(seq):
    inv = 1.0 / (ROPE_THETA ** (jnp.arange(0, DA_DH, 2, dtype=F32) / DA_DH))
    ang = jnp.arange(seq, dtype=F32)[:, None] * inv[None, :]
    cos, sin = jnp.cos(ang), jnp.sin(ang)
    cos_t = jnp.tile(cos, (1, 4))
    sign = jnp.tile(jnp.concatenate([-jnp.ones((32,), F32), jnp.ones((32,), F32)]), 2)
    sin_t = jnp.tile(sin, (1, 4)) * sign[None, :]
    return cos_t, sin_t


def _prep_layer(l, mix_norm, w_in, da_q_norm, da_k_norm, da_lambda, da_out_norm, wa_q_norm, wa_k_norm,
                wa_sink, mem_norm, w_mem_kv, ma_q_norm, ma_k_norm, w_branch, w_out, ffn_norm,
                peer_wq, peer_keys, peer_u, peer_v):
    w = w_in[l]
    pad = jnp.zeros((D_MODEL, 256), F32)
    w_r = jnp.concatenate([
        w[:, DA_Q_OFF:DA_V_OFF],
        w[:, WA_Q_OFF:MA_Q_OFF], pad,
        w[:, MA_Q_OFF:GATE_OFF], w[:, DA_V_OFF:WA_Q_OFF],
        w[:, GATE_OFF:]], axis=1).astype(BF16)
    t2 = lambda g: jnp.tile(g.astype(F32), 2)
    head_gains = jnp.stack([
        t2(da_q_norm[l]) * (DA_DH ** -0.5 * math.log2(math.e)), t2(da_k_norm[l]),
        t2(wa_q_norm[l]) * (WA_DH ** -0.5 * math.log2(math.e)), t2(wa_k_norm[l]),
        ma_q_norm[l].astype(F32) * MA_DH ** -0.5,
        jnp.zeros((LANES,), F32), jnp.zeros((LANES,), F32), jnp.zeros((LANES,), F32)])
    return dict(
        mixg=mix_norm[l][None, :], w_r=w_r, head_gains=head_gains,
        da_lambda=da_lambda[l], og=da_out_norm[l][None, :],
        sink=jnp.broadcast_to(wa_sink[l].astype(F32)[:, None], (WA_HEADS, LANES)),
        mem_g=mem_norm[l][None, :], w_kv=w_mem_kv[l].astype(BF16), ma_kg=ma_k_norm[l][None, :],
        wb=w_branch[l].astype(BF16), wo=w_out[l].astype(BF16),
        ffn_g=ffn_norm[l][None, :], peer_wq=peer_wq[l].astype(BF16),
        keys=peer_keys[l].reshape(2 * PEER_HEADS, N_KEYS, PEER_HALF).astype(BF16),
        ut=peer_u[l].astype(BF16).T, v=peer_v[l].astype(BF16),
    )


def _trunk(x, mem, layers):
    nb, seq, _ = x.shape
    mlen = mem.shape[1]
    x2d = x.reshape(nb * seq, D_MODEL)
    mem2d = mem.reshape(nb * mlen, D_MODEL)
    cos_t, sin_t = _rope_tiles(seq)
    for l, p in enumerate(layers):
        lam_init = 0.8 - 0.6 * math.exp(-0.3 * l)
        dq, dk, dv, wq, wkd, wvd, mq, gates = _in_proj(x2d, seq, p["mixg"], p["w_r"], cos_t, sin_t,
                                                      p["head_gains"])
        oda = _diff_attn(dq, dk, dv, p["da_lambda"], p["og"], nb, seq, lam_init)
        owa = _window_attn(wq, wkd, wvd, p["sink"], nb, seq)
        mk, mv = _mem_kv(mem2d, mlen, p["mem_g"], p["w_kv"], p["ma_kg"])
        oma = _mem_attn(mq, mk, mv, nb, seq, mlen)
        x1 = _merge(x2d, oda, owa, oma, gates, p["wb"], p["wo"])
        xn, q3 = _peer_q(x1, p["ffn_g"], p["peer_wq"])
        wt = _peer_wbuild(*_peer_topk(q3, p["keys"]))
        x2d = _peer_dense(xn, p["ut"], wt, p["v"], x1)
    return x2d.reshape(nb, seq, D_MODEL)


def kernel(x_prompt, x_sample, mem_prompt, mem_sample, mix_norm, w_in, da_q_norm, da_k_norm, da_lambda, da_out_norm, wa_q_norm, wa_k_norm, wa_sink, mem_norm, w_mem_kv, ma_q_norm, ma_k_norm, w_branch, w_out, ffn_norm, peer_wq, peer_keys, peer_u, peer_v):
    weights = (mix_norm, w_in, da_q_norm, da_k_norm, da_lambda, da_out_norm, wa_q_norm, wa_k_norm,
               wa_sink, mem_norm, w_mem_kv, ma_q_norm, ma_k_norm, w_branch, w_out, ffn_norm,
               peer_wq, peer_keys, peer_u, peer_v)
    layers = [_prep_layer(l, *weights) for l in range(w_in.shape[0])]
    y_prompt = _trunk(x_prompt, mem_prompt, layers)
    y_sample = _trunk(x_sample, mem_sample, layers)
    return (y_prompt, y_sample)
```

```python
import functools
import math

import jax
import jax.numpy as jnp
import numpy as np
from jax import lax
from jax.experimental import pallas as pl
from jax.experimental.pallas import tpu as pltpu

F32 = jnp.float32
BF16 = jnp.bfloat16

D_MODEL = 1024
NORM_EPS = 1e-6
NEG_INF = -1e30
ROPE_THETA = 10000.0
LANES = 128

DA_HEADS, DA_DH = 4, 64
WA_HEADS, WA_KV_HEADS, WA_DH, WINDOW = 8, 2, 64, 128
MA_HEADS, MA_DH = 4, 128
N_BRANCH, BRANCH_WIDTH = 3, 512
PEER_HEADS, N_KEYS, PEER_HALF, PEER_TOPK = 8, 128, 128, 16
N_EXPERTS = N_KEYS * N_KEYS

DA_Q_OFF = 0
DA_K_OFF = 512
DA_V_OFF = 1024
WA_Q_OFF = 1536
WA_K_OFF = 2048
WA_V_OFF = 2176
MA_Q_OFF = 2304
GATE_OFF = 2816
IN_CHUNK = 1024
N_IN_CHUNKS = 6

VMEM_LIMIT = 48 * 1024 * 1024
DIFF_TQ, DIFF_TK = 512, 2048
DENSE_CE = 2048
DV_ROWS = 144
WV_ROWS = 80


def _tile(n, pref):
    t = min(pref, n)
    t -= t % LANES
    while n % t:
        t -= LANES
    return t


def _params(sem):
    return pltpu.CompilerParams(dimension_semantics=sem, vmem_limit_bytes=VMEM_LIMIT)


def _lane_iota(shape):
    return lax.broadcasted_iota(jnp.int32, shape, len(shape) - 1)


def _rms(xf, gain_row, width):
    ms = jnp.sum(xf * xf, axis=-1, keepdims=True) * (1.0 / width)
    return xf * lax.rsqrt(ms + NORM_EPS) * gain_row


def _seg64_norm_rope(zt, gain_row, cosb, sinb):
    lane = _lane_iota(zt.shape)
    lo = lane < 64
    sq = zt * zt
    s0 = jnp.sum(jnp.where(lo, sq, 0.0), axis=-1, keepdims=True)
    s1 = jnp.sum(jnp.where(lo, 0.0, sq), axis=-1, keepdims=True)
    ms = jnp.where(lo, s0, s1) * (1.0 / 64)
    y = zt * lax.rsqrt(ms + NORM_EPS) * gain_row
    first = (lane % 64) < 32
    rot = jnp.where(first, pltpu.roll(y, 96, 1), pltpu.roll(y, 32, 1))
    return y * cosb + rot * sinb


def _in_proj_kernel(x_ref, mixg_ref, w_ref, cos_ref, sin_ref, hg_ref,
                    dq_ref, dk_ref, dv_ref, wq_ref, wkd_ref, wvt_ref, mq_ref, gate_ref):
    xn = _rms(x_ref[...], mixg_ref[...], D_MODEL).astype(BF16)
    cosb = cos_ref[...]
    sinb = sin_ref[...]

    def chunk(c):
        return jnp.dot(xn, w_ref[:, c * IN_CHUNK:(c + 1) * IN_CHUNK], preferred_element_type=F32)

    z = chunk(0)
    for t in range(4):
        sl = slice(t * LANES, (t + 1) * LANES)
        dq_ref[:, sl] = _seg64_norm_rope(z[:, sl], hg_ref[0:1, :], cosb, sinb).astype(BF16)
        zs = z[:, 512 + t * LANES:512 + (t + 1) * LANES]
        dk_ref[:, sl] = _seg64_norm_rope(zs, hg_ref[1:2, :], cosb, sinb).astype(BF16)

    z = chunk(1)
    for t in range(4):
        sl = slice(t * LANES, (t + 1) * LANES)
        wq_ref[:, sl] = _seg64_norm_rope(z[:, sl], hg_ref[2:3, :], cosb, sinb).astype(BF16)
    k = _seg64_norm_rope(z[:, 512:640], hg_ref[3:4, :], cosb, sinb)
    lo = _lane_iota(k.shape) < 64
    sw = pltpu.roll(k, 64, 1)
    wkd_ref[:, 0:LANES] = jnp.where(lo, k, sw).astype(BF16)
    wkd_ref[:, LANES:2 * LANES] = jnp.where(lo, sw, k).astype(BF16)
    vt = z[:, 640:768].T
    ones = jnp.ones((WV_ROWS - WA_DH, z.shape[0]), BF16)
    for g in range(WA_KV_HEADS):
        wvt_ref[g * WV_ROWS:g * WV_ROWS + WA_DH, :] = vt[g * WA_DH:(g + 1) * WA_DH].astype(BF16)
        wvt_ref[g * WV_ROWS + WA_DH:(g + 1) * WV_ROWS, :] = ones

    z = chunk(2)
    for t in range(4):
        sl = slice(t * LANES, (t + 1) * LANES)
        mq_ref[:, sl] = _rms(z[:, sl], hg_ref[4:5, :], MA_DH).astype(BF16)
    ones = jnp.ones((DV_ROWS - LANES, z.shape[0]), BF16)
    for t in range(DA_HEADS):
        vt = z[:, 512 + t * LANES:512 + (t + 1) * LANES].T
        dv_ref[t * DV_ROWS:t * DV_ROWS + LANES, :] = vt.astype(BF16)
        dv_ref[t * DV_ROWS + LANES:(t + 1) * DV_ROWS, :] = ones

    for c in range(3, N_IN_CHUNKS):
        gate_ref[:, (c - 3) * IN_CHUNK:(c - 2) * IN_CHUNK] = jax.nn.sigmoid(chunk(c))


def _in_proj(x2d, seq, mixg, w_r, cos_t, sin_t, head_gains):
    n = x2d.shape[0]
    tm = _tile(seq, 512)
    nsb = seq // tm
    bf = lambda w: jax.ShapeDtypeStruct((n, w), BF16)
    row = lambda w: pl.BlockSpec((tm, w), lambda i: (i, 0))
    return pl.pallas_call(
        _in_proj_kernel,
        grid=(n // tm,),
        in_specs=[
            pl.BlockSpec((tm, D_MODEL), lambda i: (i, 0)),
            pl.BlockSpec((1, D_MODEL), lambda i: (0, 0)),
            pl.BlockSpec((D_MODEL, N_IN_CHUNKS * IN_CHUNK), lambda i: (0, 0), pipeline_mode=pl.Buffered(1)),
            pl.BlockSpec((tm, LANES), lambda i: (i % nsb, 0)),
            pl.BlockSpec((tm, LANES), lambda i: (i % nsb, 0)),
            pl.BlockSpec((8, LANES), lambda i: (0, 0)),
        ],
        out_specs=[row(512), row(512),
                   pl.BlockSpec((DA_HEADS * DV_ROWS, tm), lambda i: (i // nsb, i % nsb)),
                   row(512), row(256),
                   pl.BlockSpec((WA_KV_HEADS * WV_ROWS, tm), lambda i: (i // nsb, i % nsb)),
                   row(512), row(N_BRANCH * D_MODEL)],
        out_shape=[bf(512), bf(512), jax.ShapeDtypeStruct((n // seq * DA_HEADS * DV_ROWS, seq), BF16),
                   bf(512), bf(256), jax.ShapeDtypeStruct((n // seq * WA_KV_HEADS * WV_ROWS, seq), BF16), bf(512),
                   jax.ShapeDtypeStruct((n, N_BRANCH * D_MODEL), F32)],
        compiler_params=_params(("parallel",)),
        name="in_proj",
    )(x2d, mixg, w_r, cos_t, sin_t, head_gains)


def _diff_attn_kernel(q_ref, qn_ref, k_ref, v_ref, lam_ref, og_ref, o_ref, m_sc, acc_sc, s_sc, cmax_sc, *, tk,
                      lam_init):
    def split(q):
        lo = _lane_iota(q.shape) < 64
        zero = jnp.zeros_like(q)
        return jnp.where(lo, q, zero), jnp.where(lo, zero, q)

    qs = split(q_ref[...])
    m_sc[...] = jnp.full(m_sc.shape, -jnp.inf, F32)
    acc_sc[...] = jnp.zeros(acc_sc.shape, F32)
    nt = (((1,), (1,)), ((), ()))
    n_chunks = k_ref.shape[0] // tk

    def scores(j, slot, qs=qs):
        off = pl.multiple_of(j * tk, tk)
        k = k_ref[pl.ds(off, tk), :]
        for c in range(2):
            s = lax.dot_general(k, qs[c], nt, preferred_element_type=F32)
            s_sc[slot, c] = s
            cmax_sc[slot, c] = jnp.max(s, axis=0, keepdims=True)

    def process(j, slot):
        off = pl.multiple_of(j * tk, tk)
        vt = v_ref[:, pl.ds(off, tk)]
        for c in range(2):
            s = s_sc[slot, c]
            m_old = m_sc[c]
            m_new = jnp.maximum(m_old, cmax_sc[slot, c])
            p = jnp.exp2(s - m_new)
            alpha = jnp.exp2(m_old - m_new)
            acc_sc[c] = alpha * acc_sc[c] + jnp.dot(vt, p.astype(BF16), preferred_element_type=F32)
            m_sc[c] = m_new

    if n_chunks == 1:
        scores(0, 0)
        process(0, 0)
    else:
        @pl.when(pl.program_id(2) == 0)
        def _():
            scores(0, 0)

        def pair(jj, carry):
            scores(2 * jj + 1, 1)
            process(2 * jj, 0)
            scores(2 * jj + 2, 0)
            process(2 * jj + 1, 1)
            return carry

        lax.fori_loop(0, n_chunks // 2 - 1, pair, 0)
        scores(n_chunks - 1, 1)
        process(n_chunks - 2, 0)
        scores(0, 0, split(qn_ref[...]))
        process(n_chunks - 1, 1)

    lp = lam_ref[...]
    lam = (jnp.exp(jnp.sum(lp[0:1] * lp[1:2], axis=-1, keepdims=True))
           - jnp.exp(jnp.sum(lp[2:3] * lp[3:4], axis=-1, keepdims=True)) + lam_init)
    ot = (acc_sc[0, :LANES, :] / acc_sc[0, LANES:LANES + 1, :]
          - lam * (acc_sc[1, :LANES, :] / acc_sc[1, LANES:LANES + 1, :]))
    o_ref[...] = (_rms(ot.T, og_ref[...], 2 * DA_DH) * (1.0 - lam_init)).astype(BF16)


def _diff_attn(dq, dk, dv, da_lambda, og, nb, seq, lam_init):
    n = dq.shape[0]
    tq = _tile(seq, DIFF_TQ)
    tk = _tile(seq, DIFF_TK)
    nq = seq // tq
    assert seq // tk == 1 or (seq // tk) % 2 == 0, (seq, tk)
    kern = functools.partial(_diff_attn_kernel, tk=tk, lam_init=lam_init)
    return pl.pallas_call(
        kern,
        grid=(nb, DA_HEADS, nq),
        in_specs=[
            pl.BlockSpec((tq, LANES), lambda b, h, i: (b * nq + i, h)),
            pl.BlockSpec((tq, LANES), lambda b, h, i: (b * nq + jnp.minimum(i + 1, nq - 1), h)),
            pl.BlockSpec((seq, LANES), lambda b, h, i: (b, h)),
            pl.BlockSpec((DV_ROWS, seq), lambda b, h, i: (b * DA_HEADS + h, 0)),
            pl.BlockSpec((4, DA_DH), lambda b, h, i: (0, 0)),
            pl.BlockSpec((1, LANES), lambda b, h, i: (0, 0)),
        ],
        out_specs=pl.BlockSpec((tq, LANES), lambda b, h, i: (b * nq + i, h)),
        out_shape=jax.ShapeDtypeStruct((n, BRANCH_WIDTH), BF16),
        scratch_shapes=[pltpu.VMEM((2, 1, tq), F32), pltpu.VMEM((2, DV_ROWS, tq), F32),
                        pltpu.VMEM((2, 2, tk, tq), F32), pltpu.VMEM((2, 2, 1, tq), F32)],
        compiler_params=_params(("parallel", "parallel", "arbitrary")),
        name="diff_attn",
    )(dq, dq, dk, dv, da_lambda, og)


def _window_attn_kernel(q_ref, kd_ref, vt_ref, sink_ref, o_ref, *, tq, seq):
    i = pl.program_id(1)
    nsub = tq // WINDOW
    kw_len = 3 * WINDOW
    nt = (((1,), (1,)), ((), ()))
    rep = WA_HEADS // WA_KV_HEADS
    lo = _lane_iota((WINDOW, LANES)) < 64
    log2e = math.log2(math.e)
    for n in range(nsub):
        gb = i * nsub + n
        start = pl.multiple_of(jnp.clip((gb - 1) * WINDOW, 0, seq - kw_len), WINDOW)
        kpos = start + lax.broadcasted_iota(jnp.int32, (kw_len, WINDOW), 0)
        qpos = gb * WINDOW + lax.broadcasted_iota(jnp.int32, (kw_len, WINDOW), 1)
        valid1 = jnp.abs(qpos - kpos) <= WINDOW
        valid = jnp.concatenate([valid1] * rep, axis=1)
        for g in range(WA_KV_HEADS):
            kg = kd_ref[pl.ds(start, kw_len), g * LANES:(g + 1) * LANES]
            vt = vt_ref[g * WV_ROWS:(g + 1) * WV_ROWS, pl.ds(start, kw_len)]
            tiles, sinks = [], []
            for e in range(rep):
                h = g * rep + e
                qp = q_ref[n * WINDOW:(n + 1) * WINDOW, (h // 2) * LANES:(h // 2 + 1) * LANES]
                zero = jnp.zeros_like(qp)
                tiles.append(jnp.where(lo, qp, zero) if h % 2 == 0 else jnp.where(lo, zero, qp))
                sinks.append(jnp.broadcast_to(sink_ref[h:h + 1, 0:1] * log2e, (1, WINDOW)))
            qs = jnp.concatenate(tiles, axis=0)
            sk = jnp.concatenate(sinks, axis=1)
            s = lax.dot_general(kg, qs, nt, preferred_element_type=F32)
            s = jnp.where(valid, s, NEG_INF)
            m = jnp.maximum(jnp.max(s, axis=0, keepdims=True), sk)
            r = jnp.dot(vt, jnp.exp2(s - m).astype(BF16), preferred_element_type=F32)
            ot = r[:WA_DH] / (r[WA_DH:WA_DH + 1] + jnp.exp2(sk - m))
            for pr in range(rep // 2):
                col = (g * rep // 2 + pr) * LANES
                pair = jnp.concatenate([ot[:, 2 * pr * WINDOW:(2 * pr + 1) * WINDOW],
                                        ot[:, (2 * pr + 1) * WINDOW:(2 * pr + 2) * WINDOW]], axis=0)
                o_ref[n * WINDOW:(n + 1) * WINDOW, col:col + LANES] = pair.T.astype(BF16)


def _window_attn(wq, wkd, wvd, sink_b, nb, seq):
    n = wq.shape[0]
    tq = _tile(seq, 512)
    nq = seq // tq
    kern = functools.partial(_window_attn_kernel, tq=tq, seq=seq)
    return pl.pallas_call(
        kern,
        grid=(nb, nq),
        in_specs=[
            pl.BlockSpec((tq, 512), lambda b, i: (b * nq + i, 0)),
            pl.BlockSpec((seq, 256), lambda b, i: (b, 0)),
            pl.BlockSpec((WA_KV_HEADS * WV_ROWS, seq), lambda b, i: (b, 0)),
            pl.BlockSpec((8, LANES), lambda b, i: (0, 0)),
        ],
        out_specs=pl.BlockSpec((tq, 512), lambda b, i: (b * nq + i, 0)),
        out_shape=jax.ShapeDtypeStruct((n, BRANCH_WIDTH), BF16),
        compiler_params=_params(("parallel", "arbitrary")),
        name="window_attn",
    )(wq, wkd, wvd, sink_b)


def _mem_kv_kernel(mem_ref, mg_ref, w_ref, kg_ref, mk_ref, mv_ref):
    h = _rms(mem_ref[...], mg_ref[...], D_MODEL).astype(BF16)
    kv = jnp.dot(h, w_ref[...], preferred_element_type=F32)
    for t in range(MA_HEADS):
        sl = slice(t * LANES, (t + 1) * LANES)
        mk_ref[:, sl] = _rms(kv[:, sl], kg_ref[...], MA_DH).astype(BF16)
    mv_ref[...] = kv[:, 512:1024].astype(BF16)


def _mem_kv(mem2d, mlen, mg, w_kv, kg):
    n = mem2d.shape[0]
    return pl.pallas_call(
        _mem_kv_kernel,
        grid=(n // mlen,),
        in_specs=[
            pl.BlockSpec((mlen, D_MODEL), lambda b: (b, 0)),
            pl.BlockSpec((1, D_MODEL), lambda b: (0, 0)),
            pl.BlockSpec((D_MODEL, 2 * MA_HEADS * MA_DH), lambda b: (0, 0)),
            pl.BlockSpec((1, LANES), lambda b: (0, 0)),
        ],
        out_specs=[pl.BlockSpec((mlen, 512), lambda b: (b, 0)), pl.BlockSpec((mlen, 512), lambda b: (b, 0))],
        out_shape=[jax.ShapeDtypeStruct((n, 512), BF16), jax.ShapeDtypeStruct((n, 512), BF16)],
        compiler_params=_params(("parallel",)),
        name="mem_kv",
    )(mem2d, mg, w_kv, kg)


def _mem_attn_kernel(q_ref, mk_ref, mv_ref, o_ref):
    nt = (((1,), (1,)), ((), ()))
    for t in range(MA_HEADS):
        sl = slice(t * LANES, (t + 1) * LANES)
        s = lax.dot_general(q_ref[:, sl], mk_ref[:, sl], nt, preferred_element_type=F32)
        m = jnp.max(s, axis=-1, keepdims=True)
        ex = jnp.exp(s - m)
        p = ex / jnp.sum(ex, axis=-1, keepdims=True)
        o_ref[:, sl] = jnp.dot(p.astype(BF16), mv_ref[:, sl], preferred_element_type=F32).astype(BF16)


def _mem_attn(mq, mk, mv, nb, seq, mlen):
    n = mq.shape[0]
    tq = _tile(seq, 1024)
    nq = seq // tq
    return pl.pallas_call(
        _mem_attn_kernel,
        grid=(nb, nq),
        in_specs=[
            pl.BlockSpec((tq, 512), lambda b, i: (b * nq + i, 0)),
            pl.BlockSpec((mlen, 512), lambda b, i: (b, 0)),
            pl.BlockSpec((mlen, 512), lambda b, i: (b, 0)),
        ],
        out_specs=pl.BlockSpec((tq, 512), lambda b, i: (b * nq + i, 0)),
        out_shape=jax.ShapeDtypeStruct((n, BRANCH_WIDTH), BF16),
        compiler_params=_params(("parallel", "arbitrary")),
        name="mem_attn",
    )(mq, mk, mv)


def _merge_kernel(x_ref, oda_ref, owa_ref, oma_ref, gate_ref, wb_ref, wo_ref, o_ref):
    merged = None
    for nbr, br in enumerate((oda_ref, owa_ref, oma_ref)):
        proj = jnp.dot(br[...], wb_ref[nbr], preferred_element_type=F32)
        term = gate_ref[:, nbr * D_MODEL:(nbr + 1) * D_MODEL] * proj
        merged = term if merged is None else merged + term
    o_ref[...] = x_ref[...] + jnp.dot(merged.astype(BF16), wo_ref[...], preferred_element_type=F32)


def _merge(x2d, oda, owa, oma, gates, wb, wo):
    n = x2d.shape[0]
    tm = _tile(n, 512)
    row = lambda w: pl.BlockSpec((tm, w), lambda i: (i, 0))
    return pl.pallas_call(
        _merge_kernel,
        grid=(n // tm,),
        in_specs=[row(D_MODEL), row(512), row(512), row(512), row(N_BRANCH * D_MODEL),
                  pl.BlockSpec((N_BRANCH, BRANCH_WIDTH, D_MODEL), lambda i: (0, 0, 0)),
                  pl.BlockSpec((D_MODEL, D_MODEL), lambda i: (0, 0))],
        out_specs=row(D_MODEL),
        out_shape=jax.ShapeDtypeStruct((n, D_MODEL), F32),
        compiler_params=_params(("parallel",)),
        name="merge",
    )(x2d, oda, owa, oma, gates, wb, wo)


_SORT_NETS = {
    2: ((0, 1, True),),
    4: ((0, 1, True), (2, 3, True), (0, 2, True), (1, 3, True), (1, 2, False)),
}


def _top16_lead(rows, ids, group):
    assert len(rows) % group == 0 and group in _SORT_NETS
    gv, gi = [], []
    for j in range(0, len(rows), group):
        vs, js = list(rows[j:j + group]), list(ids[j:j + group])
        for a, b, left_has_lower_ids in _SORT_NETS[group]:
            va, vb, ia, ib = vs[a], vs[b], js[a], js[b]
            take = (va >= vb) if left_has_lower_ids else ((va > vb) | ((va == vb) & (ia < ib)))
            vs[a], vs[b] = jnp.maximum(va, vb), jnp.minimum(va, vb)
            js[a], js[b] = jnp.where(take, ia, ib), jnp.where(take, ib, ia)
        gv.append(vs)
        gi.append(js)
    vals, idxs = [], []
    for _ in range(PEER_TOPK):
        level = [(v[0], i[0]) for v, i in zip(gv, gi)]
        while len(level) > 1:
            nxt = []
            for j in range(0, len(level) - 1, 2):
                (va, ia), (vb, ib) = level[j], level[j + 1]
                take = va >= vb
                nxt.append((jnp.maximum(va, vb), jnp.where(take, ia, ib)))
            if len(level) % 2:
                nxt.append(level[-1])
            level = nxt
        m, sel = level[0]
        for vs, js in zip(gv, gi):
            hit = sel == js[0]
            for k in range(group - 1):
                vs[k] = jnp.where(hit, vs[k + 1], vs[k])
                js[k] = jnp.where(hit, js[k + 1], js[k])
            vs[group - 1] = jnp.where(hit, -jnp.inf, vs[group - 1])
        vals.append(m)
        idxs.append(sel)
    return vals, idxs


def _peer_q_kernel(x_ref, fg_ref, wq_ref, xn_ref, q_ref):
    xn = _rms(x_ref[...], fg_ref[...], D_MODEL).astype(BF16)
    xn_ref[...] = xn
    q = jnp.dot(xn, wq_ref[...], preferred_element_type=F32).astype(BF16)
    for hp in range(2 * PEER_HEADS):
        q_ref[hp] = q[:, hp * LANES:(hp + 1) * LANES]


def _peer_q(x2d, fg, wq):
    n = x2d.shape[0]
    tm = _tile(n, 512)
    return pl.pallas_call(
        _peer_q_kernel,
        grid=(n // tm,),
        in_specs=[
            pl.BlockSpec((tm, D_MODEL), lambda i: (i, 0)),
            pl.BlockSpec((1, D_MODEL), lambda i: (0, 0)),
            pl.BlockSpec((D_MODEL, 2 * PEER_HEADS * PEER_HALF), lambda i: (0, 0)),
        ],
        out_specs=[pl.BlockSpec((tm, D_MODEL), lambda i: (i, 0)),
                   pl.BlockSpec((2 * PEER_HEADS, tm, PEER_HALF), lambda i: (0, i, 0))],
        out_shape=[jax.ShapeDtypeStruct((n, D_MODEL), BF16),
                   jax.ShapeDtypeStruct((2 * PEER_HEADS, n, PEER_HALF), BF16)],
        compiler_params=_params(("parallel",)),
        name="peer_q",
    )(x2d, fg, wq)


_CAND_PAIRS = tuple((a, b) for a in range(PEER_TOPK) for b in range(PEER_TOPK) if (a + 1) * (b + 1) <= PEER_TOPK)


def _peer_topk_kernel(q_ref, keys_ref, g_ref, ei_ref, ej_ref, s0_sc, s1_sc, rg_sc, ri_sc, rj_sc, *, tm):
    nt = (((1,), (1,)), ((), ()))
    gt = tm // LANES
    key_ids = [float(k) for k in range(N_KEYS)]
    cand_ids = [float(a * PEER_TOPK + b) for a, b in _CAND_PAIRS]

    def head_body(h, carry):
        sv, si = [], []
        for p, s_sc in enumerate((s0_sc, s1_sc)):
            s = lax.dot_general(keys_ref[2 * h + p], q_ref[2 * h + p], nt,
                                preferred_element_type=F32)
            for tt in range(gt):
                s_sc[pl.ds(tt, N_KEYS, stride=gt), :] = s[:, tt * LANES:(tt + 1) * LANES]
            v, ix = _top16_lead([s_sc[k * gt:(k + 1) * gt, :] for k in range(N_KEYS)], key_ids, 4)
            sv.append(v)
            si.append(ix)
        fv, fi = _top16_lead([sv[0][a] + sv[1][b] for a, b in _CAND_PAIRS], cand_ids, 2)
        ex = [jnp.exp(v - fv[0]) for v in fv]
        den = ex[0]
        for e in ex[1:]:
            den = den + e
        for k in range(PEER_TOPK):
            fik = fi[k].astype(jnp.int32)
            fa = fik >> 4
            fb = fik & (PEER_TOPK - 1)
            ei = jnp.zeros((gt, LANES), F32)
            ej = jnp.zeros((gt, LANES), F32)
            for a in range(PEER_TOPK):
                ei = jnp.where(fa == a, si[0][a], ei)
                ej = jnp.where(fb == a, si[1][a], ej)
            off = pl.multiple_of((h * PEER_TOPK + k) * gt, gt)
            rg_sc[pl.ds(off, gt), :] = ex[k] / den
            ri_sc[pl.ds(off, gt), :] = ei
            rj_sc[pl.ds(off, gt), :] = ej
        return carry

    lax.fori_loop(0, PEER_HEADS, head_body, 0)

    for tt in range(gt):
        sl = slice(tt * LANES, (tt + 1) * LANES)
        g_ref[sl, :] = rg_sc[pl.ds(tt, N_KEYS, stride=gt), :].T
        ei_ref[sl, :] = ri_sc[pl.ds(tt, N_KEYS, stride=gt), :].T
        ej_ref[sl, :] = rj_sc[pl.ds(tt, N_KEYS, stride=gt), :].T


def _peer_topk(q3, keys):
    n = q3.shape[1]
    tm = _tile(n, 8 * LANES)
    slots = PEER_HEADS * PEER_TOPK
    out = jax.ShapeDtypeStruct((n, slots), F32)
    return pl.pallas_call(
        functools.partial(_peer_topk_kernel, tm=tm),
        grid=(n // tm,),
        in_specs=[
            pl.BlockSpec((2 * PEER_HEADS, tm, PEER_HALF), lambda i: (0, i, 0)),
            pl.BlockSpec((2 * PEER_HEADS, N_KEYS, PEER_HALF), lambda i: (0, 0, 0)),
        ],
        out_specs=[pl.BlockSpec((tm, slots), lambda i: (i, 0))] * 3,
        out_shape=[out, out, out],
        scratch_shapes=[pltpu.VMEM((N_KEYS * tm // LANES, LANES), F32)] * 2
        + [pltpu.VMEM((slots * tm // LANES, LANES), F32)] * 3,
        compiler_params=_params(("parallel",)),
        name="peer_topk",
    )(q3, keys)


def _peer_wbuild_kernel(g_ref, ei_ref, ej_ref, wt_ref, w_sc, *, tm, pitch):
    sub = lax.broadcasted_iota(jnp.int32, (N_KEYS, LANES), 0).astype(F32)

    def tok_body(t, carry):
        g_r = g_ref[pl.ds(t, 1), :]
        at = jnp.where(sub == ei_ref[pl.ds(t, 1), :], g_r, 0.0).astype(BF16)
        bt = jnp.where(sub == ej_ref[pl.ds(t, 1), :], 1.0, 0.0).T.astype(BF16)
        w_t = jnp.dot(at, bt, preferred_element_type=F32)
        w_sc[pl.ds(t, N_KEYS, stride=pitch), :] = w_t
        return carry

    lax.fori_loop(0, tm, tok_body, 0, unroll=True)

    for i in range(N_KEYS):
        wt_ref[0, i] = w_sc[i * pitch:i * pitch + tm, :].astype(BF16)


def _peer_wbuild(g, ei, ej):
    n = g.shape[0]
    tm = _tile(n, 128)
    pitch = tm + 4
    row = pl.BlockSpec((tm, PEER_HEADS * PEER_TOPK), lambda i: (i, 0))
    return pl.pallas_call(
        functools.partial(_peer_wbuild_kernel, tm=tm, pitch=pitch),
        grid=(n // tm,),
        in_specs=[row, row, row],
        out_specs=pl.BlockSpec((1, N_KEYS, tm, N_KEYS), lambda i: (i, 0, 0, 0)),
        out_shape=jax.ShapeDtypeStruct((n // tm, N_KEYS, tm, N_KEYS), BF16),
        scratch_shapes=[pltpu.VMEM((N_KEYS * pitch, N_KEYS), F32)],
        compiler_params=_params(("parallel",)),
        name="peer_wbuild",
    )(g, ei, ej)


def _peer_dense_kernel(xn_ref, ut_ref, wt_ref, v_ref, x_ref, o_ref, acc_sc, *, ce):
    c = pl.program_id(1)

    @pl.when(c == 0)
    def _():
        acc_sc[...] = jnp.zeros(acc_sc.shape, F32)

    a = jnp.dot(xn_ref[...], ut_ref[...], preferred_element_type=F32)
    act = 0.5 * a * (1.0 + lax.erf(a * math.sqrt(0.5)))
    cols = []
    for s in range(ce // LANES):
        w = jnp.concatenate([wt_ref[b, s] for b in range(wt_ref.shape[0])], axis=0)
        cols.append((w.astype(F32) * act[:, s * LANES:(s + 1) * LANES]).astype(BF16))
    h = jnp.concatenate(cols, axis=1)
    acc_sc[...] += jnp.dot(h, v_ref[...], preferred_element_type=F32)

    @pl.when(c == pl.num_programs(1) - 1)
    def _():
        o_ref[...] = x_ref[...] + acc_sc[...]


def _peer_dense(xn, ut, wt, v, x2d):
    n = x2d.shape[0]
    tw = wt.shape[2]
    tm = _tile(n, 512)
    assert tm % tw == 0, (tm, tw)
    ce = DENSE_CE
    kern = functools.partial(_peer_dense_kernel, ce=ce)
    return pl.pallas_call(
        kern,
        grid=(n // tm, N_EXPERTS // ce),
        in_specs=[
            pl.BlockSpec((tm, D_MODEL), lambda i, c: (i, 0)),
            pl.BlockSpec((D_MODEL, ce), lambda i, c: (0, c)),
            pl.BlockSpec((tm // tw, ce // N_KEYS, tw, N_KEYS), lambda i, c: (i, c, 0, 0)),
            pl.BlockSpec((ce, D_MODEL), lambda i, c: (c, 0)),
            pl.BlockSpec((tm, D_MODEL), lambda i, c: (i, 0)),
        ],
        out_specs=pl.BlockSpec((tm, D_MODEL), lambda i, c: (i, 0)),
        out_shape=jax.ShapeDtypeStruct((n, D_MODEL), F32),
        scratch_shapes=[pltpu.VMEM((tm, D_MODEL), F32)],
        compiler_params=_params(("parallel", "arbitrary")),
        name="peer_dense",
    )(xn, ut, wt, v, x2d)


def _rope_tiles(seq):
    inv = 1.0 / (ROPE_THETA ** (jnp.arange(0, DA_DH, 2, dtype=F32) / DA_DH))
    ang = jnp.arange(seq, dtype=F32)[:, None] * inv[None, :]
    cos, sin = jnp.cos(ang), jnp.sin(ang)
    cos_t = jnp.tile(cos, (1, 4))
    sign = jnp.tile(jnp.concatenate([-jnp.ones((32,), F32), jnp.ones((32,), F32)]), 2)
    sin_t = jnp.tile(sin, (1, 4)) * sign[None, :]
    return cos_t, sin_t


def _prep_layer(l, mix_norm, w_in, da_q_norm, da_k_norm, da_lambda, da_out_norm, wa_q_norm, wa_k_norm,
                wa_sink, mem_norm, w_mem_kv, ma_q_norm, ma_k_norm, w_branch, w_out, ffn_norm,
                peer_wq, peer_keys, peer_u, peer_v):
    w = w_in[l]
    pad = jnp.zeros((D_MODEL, 256), F32)
    w_r = jnp.concatenate([
        w[:, DA_Q_OFF:DA_V_OFF],
        w[:, WA_Q_OFF:MA_Q_OFF], pad,
        w[:, MA_Q_OFF:GATE_OFF], w[:, DA_V_OFF:WA_Q_OFF],
        w[:, GATE_OFF:]], axis=1).astype(BF16)
    t2 = lambda g: jnp.tile(g.astype(F32), 2)
    head_gains = jnp.stack([
        t2(da_q_norm[l]) * (DA_DH ** -0.5 * math.log2(math.e)), t2(da_k_norm[l]),
        t2(wa_q_norm[l]) * (WA_DH ** -0.5 * math.log2(math.e)), t2(wa_k_norm[l]),
        ma_q_norm[l].astype(F32) * MA_DH ** -0.5,
        jnp.zeros((LANES,), F32), jnp.zeros((LANES,), F32), jnp.zeros((LANES,), F32)])
    return dict(
        mixg=mix_norm[l][None, :], w_r=w_r, head_gains=head_gains,
        da_lambda=da_lambda[l], og=da_out_norm[l][None, :],
        sink=jnp.broadcast_to(wa_sink[l].astype(F32)[:, None], (WA_HEADS, LANES)),
        mem_g=mem_norm[l][None, :], w_kv=w_mem_kv[l].astype(BF16), ma_kg=ma_k_norm[l][None, :],
        wb=w_branch[l].astype(BF16), wo=w_out[l].astype(BF16),
        ffn_g=ffn_norm[l][None, :], peer_wq=peer_wq[l].astype(BF16),
        keys=peer_keys[l].reshape(2 * PEER_HEADS, N_KEYS, PEER_HALF).astype(BF16),
        ut=peer_u[l].astype(BF16).T, v=peer_v[l].astype(BF16),
    )


def _trunk(x, mem, layers):
    nb, seq, _ = x.shape
    mlen = mem.shape[1]
    x2d = x.reshape(nb * seq, D_MODEL)
    mem2d = mem.reshape(nb * mlen, D_MODEL)
    cos_t, sin_t = _rope_tiles(seq)
    for l, p in enumerate(layers):
        lam_init = 0.8 - 0.6 * math.exp(-0.3 * l)
        dq, dk, dv, wq, wkd, wvd, mq, gates = _in_proj(x2d, seq, p["mixg"], p["w_r"], cos_t, sin_t,
                                                      p["head_gains"])
        oda = _diff_attn(dq, dk, dv, p["da_lambda"], p["og"], nb, seq, lam_init)
        owa = _window_attn(wq, wkd, wvd, p["sink"], nb, seq)
        mk, mv = _mem_kv(mem2d, mlen, p["mem_g"], p["w_kv"], p["ma_kg"])
        oma = _mem_attn(mq, mk, mv, nb, seq, mlen)
        x1 = _merge(x2d, oda, owa, oma, gates, p["wb"], p["wo"])
        xn, q3 = _peer_q(x1, p["ffn_g"], p["peer_wq"])
        wt = _peer_wbuild(*_peer_topk(q3, p["keys"]))
        x2d = _peer_dense(xn, p["ut"], wt, p["v"], x1)
    return x2d.reshape(nb, seq, D_MODEL)


def kernel(x_prompt, x_sample, mem_prompt, mem_sample, mix_norm, w_in, da_q_norm, da_k_norm, da_lambda, da_out_norm, wa_q_norm, wa_k_norm, wa_sink, mem_norm, w_mem_kv, ma_q_norm, ma_k_norm, w_branch, w_out, ffn_norm, peer_wq, peer_keys, peer_u, peer_v):
    weights = (mix_norm, w_in, da_q_norm, da_k_norm, da_lambda, da_out_norm, wa_q_norm, wa_k_norm,
               wa_sink, mem_norm, w_mem_kv, ma_q_norm, ma_k_norm, w_branch, w_out, ffn_norm,
               peer_wq, peer_keys, peer_u, peer_v)
    layers = [_prep_layer(l, *weights) for l in range(w_in.shape[0])]
    y_prompt = _trunk(x_prompt, mem_prompt, layers)
    y_sample = _trunk(x_sample, mem_sample, layers)
    return (y_prompt, y_sample)
```

```python
import functools
import math

import jax
import jax.numpy as jnp
import numpy as np
from jax import lax
from jax.experimental import pallas as pl
from jax.experimental.pallas import tpu as pltpu

F32 = jnp.float32
BF16 = jnp.bfloat16

D_MODEL = 1024
NORM_EPS = 1e-6
NEG_INF = -1e30
ROPE_THETA = 10000.0
LANES = 128

DA_HEADS, DA_DH = 4, 64
WA_HEADS, WA_KV_HEADS, WA_DH, WINDOW = 8, 2, 64, 128
MA_HEADS, MA_DH = 4, 128
N_BRANCH, BRANCH_WIDTH = 3, 512
PEER_HEADS, N_KEYS, PEER_HALF, PEER_TOPK = 8, 128, 128, 16
N_EXPERTS = N_KEYS * N_KEYS

DA_Q_OFF = 0
DA_K_OFF = 512
DA_V_OFF = 1024
WA_Q_OFF = 1536
WA_K_OFF = 2048
WA_V_OFF = 2176
MA_Q_OFF = 2304
GATE_OFF = 2816
IN_CHUNK = 1024
N_IN_CHUNKS = 6

VMEM_LIMIT = 48 * 1024 * 1024
DIFF_TQ, DIFF_TK = 512, 2048
DENSE_CE = 2048
DV_ROWS = 144
WV_ROWS = 80


def _tile(n, pref):
    t = min(pref, n)
    t -= t % LANES
    while n % t:
        t -= LANES
    return t


def _params(sem):
    return pltpu.CompilerParams(dimension_semantics=sem, vmem_limit_bytes=VMEM_LIMIT)


def _lane_iota(shape):
    return lax.broadcasted_iota(jnp.int32, shape, len(shape) - 1)


def _rms(xf, gain_row, width):
    ms = jnp.sum(xf * xf, axis=-1, keepdims=True) * (1.0 / width)
    return xf * lax.rsqrt(ms + NORM_EPS) * gain_row


def _seg64_norm_rope(zt, gain_row, cosb, sinb):
    lane = _lane_iota(zt.shape)
    lo = lane < 64
    sq = zt * zt
    s0 = jnp.sum(jnp.where(lo, sq, 0.0), axis=-1, keepdims=True)
    s1 = jnp.sum(jnp.where(lo, 0.0, sq), axis=-1, keepdims=True)
    ms = jnp.where(lo, s0, s1) * (1.0 / 64)
    y = zt * lax.rsqrt(ms + NORM_EPS) * gain_row
    first = (lane % 64) < 32
    rot = jnp.where(first, pltpu.roll(y, 96, 1), pltpu.roll(y, 32, 1))
    return y * cosb + rot * sinb


def _in_proj_kernel(x_ref, mixg_ref, w_ref, cos_ref, sin_ref, hg_ref,
                    dq_ref, dk_ref, dv_ref, wq_ref, wkd_ref, wvt_ref, mq_ref, gate_ref):
    xn = _rms(x_ref[...], mixg_ref[...], D_MODEL).astype(BF16)
    cosb = cos_ref[...]
    sinb = sin_ref[...]

    def chunk(c):
        return jnp.dot(xn, w_ref[:, c * IN_CHUNK:(c + 1) * IN_CHUNK], preferred_element_type=F32)

    z = chunk(0)
    for t in range(4):
        sl = slice(t * LANES, (t + 1) * LANES)
        dq_ref[:, sl] = _seg64_norm_rope(z[:, sl], hg_ref[0:1, :], cosb, sinb).astype(BF16)
        zs = z[:, 512 + t * LANES:512 + (t + 1) * LANES]
        dk_ref[:, sl] = _seg64_norm_rope(zs, hg_ref[1:2, :], cosb, sinb).astype(BF16)

    z = chunk(1)
    for t in range(4):
        sl = slice(t * LANES, (t + 1) * LANES)
        wq_ref[:, sl] = _seg64_norm_rope(z[:, sl], hg_ref[2:3, :], cosb, sinb).astype(BF16)
    k = _seg64_norm_rope(z[:, 512:640], hg_ref[3:4, :], cosb, sinb)
    lo = _lane_iota(k.shape) < 64
    sw = pltpu.roll(k, 64, 1)
    wkd_ref[:, 0:LANES] = jnp.where(lo, k, sw).astype(BF16)
    wkd_ref[:, LANES:2 * LANES] = jnp.where(lo, sw, k).astype(BF16)
    vt = z[:, 640:768].T
    ones = jnp.ones((WV_ROWS - WA_DH, z.shape[0]), BF16)
    for g in range(WA_KV_HEADS):
        wvt_ref[g * WV_ROWS:g * WV_ROWS + WA_DH, :] = vt[g * WA_DH:(g + 1) * WA_DH].astype(BF16)
        wvt_ref[g * WV_ROWS + WA_DH:(g + 1) * WV_ROWS, :] = ones

    z = chunk(2)
    for t in range(4):
        sl = slice(t * LANES, (t + 1) * LANES)
        mq_ref[:, sl] = _rms(z[:, sl], hg_ref[4:5, :], MA_DH).astype(BF16)
    ones = jnp.ones((DV_ROWS - LANES, z.shape[0]), BF16)
    for t in range(DA_HEADS):
        vt = z[:, 512 + t * LANES:512 + (t + 1) * LANES].T
        dv_ref[t * DV_ROWS:t * DV_ROWS + LANES, :] = vt.astype(BF16)
        dv_ref[t * DV_ROWS + LANES:(t + 1) * DV_ROWS, :] = ones

    for c in range(3, N_IN_CHUNKS):
        gate_ref[:, (c - 3) * IN_CHUNK:(c - 2) * IN_CHUNK] = jax.nn.sigmoid(chunk(c)).astype(BF16)


def _in_proj(x2d, seq, mixg, w_r, cos_t, sin_t, head_gains):
    n = x2d.shape[0]
    tm = _tile(seq, 512)
    nsb = seq // tm
    bf = lambda w: jax.ShapeDtypeStruct((n, w), BF16)
    row = lambda w: pl.BlockSpec((tm, w), lambda i: (i, 0))
    return pl.pallas_call(
        _in_proj_kernel,
        grid=(n // tm,),
        in_specs=[
            pl.BlockSpec((tm, D_MODEL), lambda i: (i, 0)),
            pl.BlockSpec((1, D_MODEL), lambda i: (0, 0)),
            pl.BlockSpec((D_MODEL, N_IN_CHUNKS * IN_CHUNK), lambda i: (0, 0), pipeline_mode=pl.Buffered(1)),
            pl.BlockSpec((tm, LANES), lambda i: (i % nsb, 0)),
            pl.BlockSpec((tm, LANES), lambda i: (i % nsb, 0)),
            pl.BlockSpec((8, LANES), lambda i: (0, 0)),
        ],
        out_specs=[row(512), row(512),
                   pl.BlockSpec((DA_HEADS * DV_ROWS, tm), lambda i: (i // nsb, i % nsb)),
                   row(512), row(256),
                   pl.BlockSpec((WA_KV_HEADS * WV_ROWS, tm), lambda i: (i // nsb, i % nsb)),
                   row(512), row(N_BRANCH * D_MODEL)],
        out_shape=[bf(512), bf(512), jax.ShapeDtypeStruct((n // seq * DA_HEADS * DV_ROWS, seq), BF16),
                   bf(512), bf(256), jax.ShapeDtypeStruct((n // seq * WA_KV_HEADS * WV_ROWS, seq), BF16), bf(512),
                   bf(N_BRANCH * D_MODEL)],
        compiler_params=_params(("parallel",)),
        name="in_proj",
    )(x2d, mixg, w_r, cos_t, sin_t, head_gains)


def _diff_attn_kernel(q_ref, qn_ref, k_ref, v_ref, lam_ref, og_ref, o_ref, m_sc, acc_sc, s_sc, cmax_sc, *, tk,
                      lam_init):
    def split(q):
        lo = _lane_iota(q.shape) < 64
        zero = jnp.zeros_like(q)
        return jnp.where(lo, q, zero), jnp.where(lo, zero, q)

    qs = split(q_ref[...])
    m_sc[...] = jnp.full(m_sc.shape, -jnp.inf, F32)
    acc_sc[...] = jnp.zeros(acc_sc.shape, F32)
    nt = (((1,), (1,)), ((), ()))
    n_chunks = k_ref.shape[0] // tk

    def scores(j, slot, qs=qs):
        off = pl.multiple_of(j * tk, tk)
        k = k_ref[pl.ds(off, tk), :]
        for c in range(2):
            s = lax.dot_general(k, qs[c], nt, preferred_element_type=F32)
            s_sc[slot, c] = s
            cmax_sc[slot, c] = jnp.max(s, axis=0, keepdims=True)

    def process(j, slot):
        off = pl.multiple_of(j * tk, tk)
        vt = v_ref[:, pl.ds(off, tk)]
        for c in range(2):
            s = s_sc[slot, c]
            m_old = m_sc[c]
            m_new = jnp.maximum(m_old, cmax_sc[slot, c])
            p = jnp.exp2(s - m_new)
            alpha = jnp.exp2(m_old - m_new)
            acc_sc[c] = alpha * acc_sc[c] + jnp.dot(vt, p.astype(BF16), preferred_element_type=F32)
            m_sc[c] = m_new

    if n_chunks == 1:
        scores(0, 0)
        process(0, 0)
    else:
        @pl.when(pl.program_id(2) == 0)
        def _():
            scores(0, 0)

        def pair(jj, carry):
            scores(2 * jj + 1, 1)
            process(2 * jj, 0)
            scores(2 * jj + 2, 0)
            process(2 * jj + 1, 1)
            return carry

        lax.fori_loop(0, n_chunks // 2 - 1, pair, 0)
        scores(n_chunks - 1, 1)
        process(n_chunks - 2, 0)
        scores(0, 0, split(qn_ref[...]))
        process(n_chunks - 1, 1)

    lp = lam_ref[...]
    lam = (jnp.exp(jnp.sum(lp[0:1] * lp[1:2], axis=-1, keepdims=True))
           - jnp.exp(jnp.sum(lp[2:3] * lp[3:4], axis=-1, keepdims=True)) + lam_init)
    ot = (acc_sc[0, :LANES, :] / acc_sc[0, LANES:LANES + 1, :]
          - lam * (acc_sc[1, :LANES, :] / acc_sc[1, LANES:LANES + 1, :]))
    o_ref[...] = (_rms(ot.T, og_ref[...], 2 * DA_DH) * (1.0 - lam_init)).astype(BF16)


def _diff_attn(dq, dk, dv, da_lambda, og, nb, seq, lam_init):
    n = dq.shape[0]
    tq = _tile(seq, DIFF_TQ)
    tk = _tile(seq, DIFF_TK)
    nq = seq // tq
    assert seq // tk == 1 or (seq // tk) % 2 == 0, (seq, tk)
    kern = functools.partial(_diff_attn_kernel, tk=tk, lam_init=lam_init)
    return pl.pallas_call(
        kern,
        grid=(nb, DA_HEADS, nq),
        in_specs=[
            pl.BlockSpec((tq, LANES), lambda b, h, i: (b * nq + i, h)),
            pl.BlockSpec((tq, LANES), lambda b, h, i: (b * nq + jnp.minimum(i + 1, nq - 1), h)),
            pl.BlockSpec((seq, LANES), lambda b, h, i: (b, h)),
            pl.BlockSpec((DV_ROWS, seq), lambda b, h, i: (b * DA_HEADS + h, 0)),
            pl.BlockSpec((4, DA_DH), lambda b, h, i: (0, 0)),
            pl.BlockSpec((1, LANES), lambda b, h, i: (0, 0)),
        ],
        out_specs=pl.BlockSpec((tq, LANES), lambda b, h, i: (b * nq + i, h)),
        out_shape=jax.ShapeDtypeStruct((n, BRANCH_WIDTH), BF16),
        scratch_shapes=[pltpu.VMEM((2, 1, tq), F32), pltpu.VMEM((2, DV_ROWS, tq), F32),
                        pltpu.VMEM((2, 2, tk, tq), F32), pltpu.VMEM((2, 2, 1, tq), F32)],
        compiler_params=_params(("parallel", "parallel", "arbitrary")),
        name="diff_attn",
    )(dq, dq, dk, dv, da_lambda, og)


def _window_attn_kernel(q_ref, kd_ref, vt_ref, sink_ref, o_ref, *, tq, seq):
    i = pl.program_id(1)
    nsub = tq // WINDOW
    kw_len = 3 * WINDOW
    nt = (((1,), (1,)), ((), ()))
    rep = WA_HEADS // WA_KV_HEADS
    lo = _lane_iota((WINDOW, LANES)) < 64
    log2e = math.log2(math.e)
    for n in range(nsub):
        gb = i * nsub + n
        start = pl.multiple_of(jnp.clip((gb - 1) * WINDOW, 0, seq - kw_len), WINDOW)
        kpos = start + lax.broadcasted_iota(jnp.int32, (kw_len, WINDOW), 0)
        qpos = gb * WINDOW + lax.broadcasted_iota(jnp.int32, (kw_len, WINDOW), 1)
        valid1 = jnp.abs(qpos - kpos) <= WINDOW
        valid = jnp.concatenate([valid1] * rep, axis=1)
        for g in range(WA_KV_HEADS):
            kg = kd_ref[pl.ds(start, kw_len), g * LANES:(g + 1) * LANES]
            vt = vt_ref[g * WV_ROWS:(g + 1) * WV_ROWS, pl.ds(start, kw_len)]
            tiles, sinks = [], []
            for e in range(rep):
                h = g * rep + e
                qp = q_ref[n * WINDOW:(n + 1) * WINDOW, (h // 2) * LANES:(h // 2 + 1) * LANES]
                zero = jnp.zeros_like(qp)
                tiles.append(jnp.where(lo, qp, zero) if h % 2 == 0 else jnp.where(lo, zero, qp))
                sinks.append(jnp.broadcast_to(sink_ref[h:h + 1, 0:1] * log2e, (1, WINDOW)))
            qs = jnp.concatenate(tiles, axis=0)
            sk = jnp.concatenate(sinks, axis=1)
            s = lax.dot_general(kg, qs, nt, preferred_element_type=F32)
            s = jnp.where(valid, s, NEG_INF)
            m = jnp.maximum(jnp.max(s, axis=0, keepdims=True), sk)
            r = jnp.dot(vt, jnp.exp2(s - m).astype(BF16), preferred_element_type=F32)
            ot = r[:WA_DH] / (r[WA_DH:WA_DH + 1] + jnp.exp2(sk - m))
            for pr in range(rep // 2):
                col = (g * rep // 2 + pr) * LANES
                pair = jnp.concatenate([ot[:, 2 * pr * WINDOW:(2 * pr + 1) * WINDOW],
                                        ot[:, (2 * pr + 1) * WINDOW:(2 * pr + 2) * WINDOW]], axis=0)
                o_ref[n * WINDOW:(n + 1) * WINDOW, col:col + LANES] = pair.T.astype(BF16)


def _window_attn(wq, wkd, wvd, sink_b, nb, seq):
    n = wq.shape[0]
    tq = _tile(seq, 512)
    nq = seq // tq
    kern = functools.partial(_window_attn_kernel, tq=tq, seq=seq)
    return pl.pallas_call(
        kern,
        grid=(nb, nq),
        in_specs=[
            pl.BlockSpec((tq, 512), lambda b, i: (b * nq + i, 0)),
            pl.BlockSpec((seq, 256), lambda b, i: (b, 0)),
            pl.BlockSpec((WA_KV_HEADS * WV_ROWS, seq), lambda b, i: (b, 0)),
            pl.BlockSpec((8, LANES), lambda b, i: (0, 0)),
        ],
        out_specs=pl.BlockSpec((tq, 512), lambda b, i: (b * nq + i, 0)),
        out_shape=jax.ShapeDtypeStruct((n, BRANCH_WIDTH), BF16),
        compiler_params=_params(("parallel", "arbitrary")),
        name="window_attn",
    )(wq, wkd, wvd, sink_b)


def _mem_kv_kernel(mem_ref, mg_ref, w_ref, kg_ref, mk_ref, mv_ref):
    h = _rms(mem_ref[...], mg_ref[...], D_MODEL).astype(BF16)
    kv = jnp.dot(h, w_ref[...], preferred_element_type=F32)
    for t in range(MA_HEADS):
        sl = slice(t * LANES, (t + 1) * LANES)
        mk_ref[:, sl] = _rms(kv[:, sl], kg_ref[...], MA_DH).astype(BF16)
    mv_ref[...] = kv[:, 512:1024].astype(BF16)


def _mem_kv(mem2d, mlen, mg, w_kv, kg):
    n = mem2d.shape[0]
    return pl.pallas_call(
        _mem_kv_kernel,
        grid=(n // mlen,),
        in_specs=[
            pl.BlockSpec((mlen, D_MODEL), lambda b: (b, 0)),
            pl.BlockSpec((1, D_MODEL), lambda b: (0, 0)),
            pl.BlockSpec((D_MODEL, 2 * MA_HEADS * MA_DH), lambda b: (0, 0)),
            pl.BlockSpec((1, LANES), lambda b: (0, 0)),
        ],
        out_specs=[pl.BlockSpec((mlen, 512), lambda b: (b, 0)), pl.BlockSpec((mlen, 512), lambda b: (b, 0))],
        out_shape=[jax.ShapeDtypeStruct((n, 512), BF16), jax.ShapeDtypeStruct((n, 512), BF16)],
        compiler_params=_params(("parallel",)),
        name="mem_kv",
    )(mem2d, mg, w_kv, kg)


def _mem_attn_kernel(q_ref, mk_ref, mv_ref, o_ref):
    nt = (((1,), (1,)), ((), ()))
    for t in range(MA_HEADS):
        sl = slice(t * LANES, (t + 1) * LANES)
        s = lax.dot_general(q_ref[:, sl], mk_ref[:, sl], nt, preferred_element_type=F32)
        m = jnp.max(s, axis=-1, keepdims=True)
        ex = jnp.exp(s - m)
        p = ex / jnp.sum(ex, axis=-1, keepdims=True)
        o_ref[:, sl] = jnp.dot(p.astype(BF16), mv_ref[:, sl], preferred_element_type=F32).astype(BF16)


def _mem_attn(mq, mk, mv, nb, seq, mlen):
    n = mq.shape[0]
    tq = _tile(seq, 1024)
    nq = seq // tq
    return pl.pallas_call(
        _mem_attn_kernel,
        grid=(nb, nq),
        in_specs=[
            pl.BlockSpec((tq, 512), lambda b, i: (b * nq + i, 0)),
            pl.BlockSpec((mlen, 512), lambda b, i: (b, 0)),
            pl.BlockSpec((mlen, 512), lambda b, i: (b, 0)),
        ],
        out_specs=pl.BlockSpec((tq, 512), lambda b, i: (b * nq + i, 0)),
        out_shape=jax.ShapeDtypeStruct((n, BRANCH_WIDTH), BF16),
        compiler_params=_params(("parallel", "arbitrary")),
        name="mem_attn",
    )(mq, mk, mv)


def _merge_kernel(x_ref, oda_ref, owa_ref, oma_ref, gate_ref, wb_ref, wo_ref, o_ref):
    merged = None
    for nbr, br in enumerate((oda_ref, owa_ref, oma_ref)):
        proj = jnp.dot(br[...], wb_ref[nbr], preferred_element_type=F32)
        term = gate_ref[:, nbr * D_MODEL:(nbr + 1) * D_MODEL].astype(F32) * proj
        merged = term if merged is None else merged + term
    o_ref[...] = x_ref[...] + jnp.dot(merged.astype(BF16), wo_ref[...], preferred_element_type=F32)


def _merge(x2d, oda, owa, oma, gates, wb, wo):
    n = x2d.shape[0]
    tm = _tile(n, 512)
    row = lambda w: pl.BlockSpec((tm, w), lambda i: (i, 0))
    return pl.pallas_call(
        _merge_kernel,
        grid=(n // tm,),
        in_specs=[row(D_MODEL), row(512), row(512), row(512), row(N_BRANCH * D_MODEL),
                  pl.BlockSpec((N_BRANCH, BRANCH_WIDTH, D_MODEL), lambda i: (0, 0, 0)),
                  pl.BlockSpec((D_MODEL, D_MODEL), lambda i: (0, 0))],
        out_specs=row(D_MODEL),
        out_shape=jax.ShapeDtypeStruct((n, D_MODEL), F32),
        compiler_params=_params(("parallel",)),
        name="merge",
    )(x2d, oda, owa, oma, gates, wb, wo)


_SORT_NETS = {
    2: ((0, 1, True),),
    4: ((0, 1, True), (2, 3, True), (0, 2, True), (1, 3, True), (1, 2, False)),
}


def _top16_lead(rows, ids, group):
    assert len(rows) % group == 0 and group in _SORT_NETS
    gv, gi = [], []
    for j in range(0, len(rows), group):
        vs, js = list(rows[j:j + group]), list(ids[j:j + group])
        for a, b, left_has_lower_ids in _SORT_NETS[group]:
            va, vb, ia, ib = vs[a], vs[b], js[a], js[b]
            take = (va >= vb) if left_has_lower_ids else ((va > vb) | ((va == vb) & (ia < ib)))
            vs[a], vs[b] = jnp.maximum(va, vb), jnp.minimum(va, vb)
            js[a], js[b] = jnp.where(take, ia, ib), jnp.where(take, ib, ia)
        gv.append(vs)
        gi.append(js)
    vals, idxs = [], []
    for _ in range(PEER_TOPK):
        level = [(v[0], i[0]) for v, i in zip(gv, gi)]
        while len(level) > 1:
            nxt = []
            for j in range(0, len(level) - 1, 2):
                (va, ia), (vb, ib) = level[j], level[j + 1]
                take = va >= vb
                nxt.append((jnp.maximum(va, vb), jnp.where(take, ia, ib)))
            if len(level) % 2:
                nxt.append(level[-1])
            level = nxt
        m, sel = level[0]
        for vs, js in zip(gv, gi):
            hit = sel == js[0]
            for k in range(group - 1):
                vs[k] = jnp.where(hit, vs[k + 1], vs[k])
                js[k] = jnp.where(hit, js[k + 1], js[k])
            vs[group - 1] = jnp.where(hit, -jnp.inf, vs[group - 1])
        vals.append(m)
        idxs.append(sel)
    return vals, idxs


def _peer_q_kernel(x_ref, fg_ref, wq_ref, xn_ref, q_ref):
    xn = _rms(x_ref[...], fg_ref[...], D_MODEL).astype(BF16)
    xn_ref[...] = xn
    q = jnp.dot(xn, wq_ref[...], preferred_element_type=F32).astype(BF16)
    for hp in range(2 * PEER_HEADS):
        q_ref[hp] = q[:, hp * LANES:(hp + 1) * LANES]


def _peer_q(x2d, fg, wq):
    n = x2d.shape[0]
    tm = _tile(n, 512)
    return pl.pallas_call(
        _peer_q_kernel,
        grid=(n // tm,),
        in_specs=[
            pl.BlockSpec((tm, D_MODEL), lambda i: (i, 0)),
            pl.BlockSpec((1, D_MODEL), lambda i: (0, 0)),
            pl.BlockSpec((D_MODEL, 2 * PEER_HEADS * PEER_HALF), lambda i: (0, 0)),
        ],
        out_specs=[pl.BlockSpec((tm, D_MODEL), lambda i: (i, 0)),
                   pl.BlockSpec((2 * PEER_HEADS, tm, PEER_HALF), lambda i: (0, i, 0))],
        out_shape=[jax.ShapeDtypeStruct((n, D_MODEL), BF16),
                   jax.ShapeDtypeStruct((2 * PEER_HEADS, n, PEER_HALF), BF16)],
        compiler_params=_params(("parallel",)),
        name="peer_q",
    )(x2d, fg, wq)


_CAND_PAIRS = tuple((a, b) for a in range(PEER_TOPK) for b in range(PEER_TOPK) if (a + 1) * (b + 1) <= PEER_TOPK)


def _peer_topk_kernel(q_ref, keys_ref, g_ref, ei_ref, ej_ref, s0_sc, s1_sc, rg_sc, ri_sc, rj_sc, *, tm):
    nt = (((1,), (1,)), ((), ()))
    gt = tm // LANES
    key_ids = [float(k) for k in range(N_KEYS)]
    cand_ids = [float(a * PEER_TOPK + b) for a, b in _CAND_PAIRS]

    def head_body(h, carry):
        sv, si = [], []
        for p, s_sc in enumerate((s0_sc, s1_sc)):
            s = lax.dot_general(keys_ref[2 * h + p], q_ref[2 * h + p], nt,
                                preferred_element_type=F32)
            for tt in range(gt):
                s_sc[pl.ds(tt, N_KEYS, stride=gt), :] = s[:, tt * LANES:(tt + 1) * LANES]
            v, ix = _top16_lead([s_sc[k * gt:(k + 1) * gt, :] for k in range(N_KEYS)], key_ids, 4)
            sv.append(v)
            si.append(ix)
        fv, fi = _top16_lead([sv[0][a] + sv[1][b] for a, b in _CAND_PAIRS], cand_ids, 2)
        ex = [jnp.exp(v - fv[0]) for v in fv]
        den = ex[0]
        for e in ex[1:]:
            den = den + e
        for k in range(PEER_TOPK):
            fik = fi[k].astype(jnp.int32)
            fa = fik >> 4
            fb = fik & (PEER_TOPK - 1)
            ei = jnp.zeros((gt, LANES), F32)
            ej = jnp.zeros((gt, LANES), F32)
            for a in range(PEER_TOPK):
                ei = jnp.where(fa == a, si[0][a], ei)
                ej = jnp.where(fb == a, si[1][a], ej)
            off = pl.multiple_of((h * PEER_TOPK + k) * gt, gt)
            rg_sc[pl.ds(off, gt), :] = ex[k] / den
            ri_sc[pl.ds(off, gt), :] = ei
            rj_sc[pl.ds(off, gt), :] = ej
        return carry

    lax.fori_loop(0, PEER_HEADS, head_body, 0)

    for tt in range(gt):
        sl = slice(tt * LANES, (tt + 1) * LANES)
        g_ref[sl, :] = rg_sc[pl.ds(tt, N_KEYS, stride=gt), :].T
        ei_ref[sl, :] = ri_sc[pl.ds(tt, N_KEYS, stride=gt), :].T
        ej_ref[sl, :] = rj_sc[pl.ds(tt, N_KEYS, stride=gt), :].T


def _peer_topk(q3, keys):
    n = q3.shape[1]
    tm = _tile(n, 8 * LANES)
    slots = PEER_HEADS * PEER_TOPK
    out = jax.ShapeDtypeStruct((n, slots), F32)
    return pl.pallas_call(
        functools.partial(_peer_topk_kernel, tm=tm),
        grid=(n // tm,),
        in_specs=[
            pl.BlockSpec((2 * PEER_HEADS, tm, PEER_HALF), lambda i: (0, i, 0)),
            pl.BlockSpec((2 * PEER_HEADS, N_KEYS, PEER_HALF), lambda i: (0, 0, 0)),
        ],
        out_specs=[pl.BlockSpec((tm, slots), lambda i: (i, 0))] * 3,
        out_shape=[out, out, out],
        scratch_shapes=[pltpu.VMEM((N_KEYS * tm // LANES, LANES), F32)] * 2
        + [pltpu.VMEM((slots * tm // LANES, LANES), F32)] * 3,
        compiler_params=_params(("parallel",)),
        name="peer_topk",
    )(q3, keys)


def _peer_wbuild_kernel(g_ref, ei_ref, ej_ref, wt_ref, w_sc, *, tm, pitch):
    sub = lax.broadcasted_iota(jnp.int32, (N_KEYS, LANES), 0).astype(F32)

    def tok_body(t, carry):
        g_r = g_ref[pl.ds(t, 1), :]
        at = jnp.where(sub == ei_ref[pl.ds(t, 1), :], g_r, 0.0).astype(BF16)
        bt = jnp.where(sub == ej_ref[pl.ds(t, 1), :], 1.0, 0.0).T.astype(BF16)
        w_t = jnp.dot(at, bt, preferred_element_type=F32)
        w_sc[pl.ds(t, N_KEYS, stride=pitch), :] = w_t
        return carry

    lax.fori_loop(0, tm, tok_body, 0, unroll=True)

    for i in range(N_KEYS):
        wt_ref[0, i] = w_sc[i * pitch:i * pitch + tm, :].astype(BF16)


def _peer_wbuild(g, ei, ej):
    n = g.shape[0]
    tm = _tile(n, 128)
    pitch = tm + 4
    row = pl.BlockSpec((tm, PEER_HEADS * PEER_TOPK), lambda i: (i, 0))
    return pl.pallas_call(
        functools.partial(_peer_wbuild_kernel, tm=tm, pitch=pitch),
        grid=(n // tm,),
        in_specs=[row, row, row],
        out_specs=pl.BlockSpec((1, N_KEYS, tm, N_KEYS), lambda i: (i, 0, 0, 0)),
        out_shape=jax.ShapeDtypeStruct((n // tm, N_KEYS, tm, N_KEYS), BF16),
        scratch_shapes=[pltpu.VMEM((N_KEYS * pitch, N_KEYS), F32)],
        compiler_params=_params(("parallel",)),
        name="peer_wbuild",
    )(g, ei, ej)


def _peer_dense_kernel(xn_ref, ut_ref, wt_ref, v_ref, x_ref, o_ref, acc_sc, *, ce):
    c = pl.program_id(1)

    @pl.when(c == 0)
    def _():
        acc_sc[...] = jnp.zeros(acc_sc.shape, F32)

    a = jnp.dot(xn_ref[...], ut_ref[...], preferred_element_type=F32)
    act = 0.5 * a * (1.0 + lax.erf(a * math.sqrt(0.5)))
    cols = []
    for s in range(ce // LANES):
        w = jnp.concatenate([wt_ref[b, s] for b in range(wt_ref.shape[0])], axis=0)
        cols.append((w.astype(F32) * act[:, s * LANES:(s + 1) * LANES]).astype(BF16))
    h = jnp.concatenate(cols, axis=1)
    acc_sc[...] += jnp.dot(h, v_ref[...], preferred_element_type=F32)

    @pl.when(c == pl.num_programs(1) - 1)
    def _():
        o_ref[...] = x_ref[...] + acc_sc[...]


def _peer_dense(xn, ut, wt, v, x2d):
    n = x2d.shape[0]
    tw = wt.shape[2]
    tm = _tile(n, 512)
    assert tm % tw == 0, (tm, tw)
    ce = DENSE_CE
    kern = functools.partial(_peer_dense_kernel, ce=ce)
    return pl.pallas_call(
        kern,
        grid=(n // tm, N_EXPERTS // ce),
        in_specs=[
            pl.BlockSpec((tm, D_MODEL), lambda i, c: (i, 0)),
            pl.BlockSpec((D_MODEL, ce), lambda i, c: (0, c)),
            pl.BlockSpec((tm // tw, ce // N_KEYS, tw, N_KEYS), lambda i, c: (i, c, 0, 0)),
            pl.BlockSpec((ce, D_MODEL), lambda i, c: (c, 0)),
            pl.BlockSpec((tm, D_MODEL), lambda i, c: (i, 0)),
        ],
        out_specs=pl.BlockSpec((tm, D_MODEL), lambda i, c: (i, 0)),
        out_shape=jax.ShapeDtypeStruct((n, D_MODEL), F32),
        scratch_shapes=[pltpu.VMEM((tm, D_MODEL), F32)],
        compiler_params=_params(("parallel", "arbitrary")),
        name="peer_dense",
    )(xn, ut, wt, v, x2d)


def _rope_tiles(seq):
    inv = 1.0 / (ROPE_THETA ** (jnp.arange(0, DA_DH, 2, dtype=F32) / DA_DH))
    ang = jnp.arange(seq, dtype=F32)[:, None] * inv[None, :]
    cos, sin = jnp.cos(ang), jnp.sin(ang)
    cos_t = jnp.tile(cos, (1, 4))
    sign = jnp.tile(jnp.concatenate([-jnp.ones((32,), F32), jnp.ones((32,), F32)]), 2)
    sin_t = jnp.tile(sin, (1, 4)) * sign[None, :]
    return cos_t, sin_t


def _prep_layer(l, mix_norm, w_in, da_q_norm, da_k_norm, da_lambda, da_out_norm, wa_q_norm, wa_k_norm,
                wa_sink, mem_norm, w_mem_kv, ma_q_norm, ma_k_norm, w_branch, w_out, ffn_norm,
                peer_wq, peer_keys, peer_u, peer_v):
    w = w_in[l]
    pad = jnp.zeros((D_MODEL, 256), F32)
    w_r = jnp.concatenate([
        w[:, DA_Q_OFF:DA_V_OFF],
        w[:, WA_Q_OFF:MA_Q_OFF], pad,
        w[:, MA_Q_OFF:GATE_OFF], w[:, DA_V_OFF:WA_Q_OFF],
        w[:, GATE_OFF:]], axis=1).astype(BF16)
    t2 = lambda g: jnp.tile(g.astype(F32), 2)
    head_gains = jnp.stack([
        t2(da_q_norm[l]) * (DA_DH ** -0.5 * math.log2(math.e)), t2(da_k_norm[l]),
        t2(wa_q_norm[l]) * (WA_DH ** -0.5 * math.log2(math.e)), t2(wa_k_norm[l]),
        ma_q_norm[l].astype(F32) * MA_DH ** -0.5,
        jnp.zeros((LANES,), F32), jnp.zeros((LANES,), F32), jnp.zeros((LANES,), F32)])
    return dict(
        mixg=mix_norm[l][None, :], w_r=w_r, head_gains=head_gains,
        da_lambda=da_lambda[l], og=da_out_norm[l][None, :],
        sink=jnp.broadcast_to(wa_sink[l].astype(F32)[:, None], (WA_HEADS, LANES)),
        mem_g=mem_norm[l][None, :], w_kv=w_mem_kv[l].astype(BF16), ma_kg=ma_k_norm[l][None, :],
        wb=w_branch[l].astype(BF16), wo=w_out[l].astype(BF16),
        ffn_g=ffn_norm[l][None, :], peer_wq=peer_wq[l].astype(BF16),
        keys=peer_keys[l].reshape(2 * PEER_HEADS, N_KEYS, PEER_HALF).astype(BF16),
        ut=peer_u[l].astype(BF16).T, v=peer_v[l].astype(BF16),
    )


def _trunk(x, mem, layers):
    nb, seq, _ = x.shape
    mlen = mem.shape[1]
    x2d = x.reshape(nb * seq, D_MODEL)
    mem2d = mem.reshape(nb * mlen, D_MODEL)
    cos_t, sin_t = _rope_tiles(seq)
    for l, p in enumerate(layers):
        lam_init = 0.8 - 0.6 * math.exp(-0.3 * l)
        dq, dk, dv, wq, wkd, wvd, mq, gates = _in_proj(x2d, seq, p["mixg"], p["w_r"], cos_t, sin_t,
                                                      p["head_gains"])
        oda = _diff_attn(dq, dk, dv, p["da_lambda"], p["og"], nb, seq, lam_init)
        owa = _window_attn(wq, wkd, wvd, p["sink"], nb, seq)
        mk, mv = _mem_kv(mem2d, mlen, p["mem_g"], p["w_kv"], p["ma_kg"])
        oma = _mem_attn(mq, mk, mv, nb, seq, mlen)
        x1 = _merge(x2d, oda, owa, oma, gates, p["wb"], p["wo"])
        xn, q3 = _peer_q(x1, p["ffn_g"], p["peer_wq"])
        wt = _peer_wbuild(*_peer_topk(q3, p["keys"]))
        x2d = _peer_dense(xn, p["ut"], wt, p["v"], x1)
    return x2d.reshape(nb, seq, D_MODEL)


def kernel(x_prompt, x_sample, mem_prompt, mem_sample, mix_norm, w_in, da_q_norm, da_k_norm, da_lambda, da_out_norm, wa_q_norm, wa_k_norm, wa_sink, mem_norm, w_mem_kv, ma_q_norm, ma_k_norm, w_branch, w_out, ffn_norm, peer_wq, peer_keys, peer_u, peer_v):
    weights = (mix_norm, w_in, da_q_norm, da_k_norm, da_lambda, da_out_norm, wa_q_norm, wa_k_norm,
               wa_sink, mem_norm, w_mem_kv, ma_q_norm, ma_k_norm, w_branch, w_out, ffn_norm,
               peer_wq, peer_keys, peer_u, peer_v)
    layers = [_prep_layer(l, *weights) for l in range(w_in.shape[0])]
    y_prompt = _trunk(x_prompt, mem_prompt, layers)
    y_sample = _trunk(x_sample, mem_sample, layers)
    return (y_prompt, y_sample)
```

```python
import functools
import math

import jax
import jax.numpy as jnp
import numpy as np
from jax import lax
from jax.experimental import pallas as pl
from jax.experimental.pallas import tpu as pltpu

F32 = jnp.float32
BF16 = jnp.bfloat16

D_MODEL = 1024
NORM_EPS = 1e-6
NEG_INF = -1e30
ROPE_THETA = 10000.0
LANES = 128

DA_HEADS, DA_DH = 4, 64
WA_HEADS, WA_KV_HEADS, WA_DH, WINDOW = 8, 2, 64, 128
MA_HEADS, MA_DH = 4, 128
N_BRANCH, BRANCH_WIDTH = 3, 512
PEER_HEADS, N_KEYS, PEER_HALF, PEER_TOPK = 8, 128, 128, 16
N_EXPERTS = N_KEYS * N_KEYS

DA_Q_OFF = 0
DA_K_OFF = 512
DA_V_OFF = 1024
WA_Q_OFF = 1536
WA_K_OFF = 2048
WA_V_OFF = 2176
MA_Q_OFF = 2304
GATE_OFF = 2816
IN_CHUNK = 1024
N_IN_CHUNKS = 6

VMEM_LIMIT = 48 * 1024 * 1024
DIFF_TQ, DIFF_TK = 512, 2048
DENSE_CE = 2048
DV_ROWS = 144
WV_ROWS = 80


def _tile(n, pref):
    t = min(pref, n)
    t -= t % LANES
    while n % t:
        t -= LANES
    return t


def _params(sem):
    return pltpu.CompilerParams(dimension_semantics=sem, vmem_limit_bytes=VMEM_LIMIT)


def _lane_iota(shape):
    return lax.broadcasted_iota(jnp.int32, shape, len(shape) - 1)


def _rms(xf, gain_row, width):
    ms = jnp.sum(xf * xf, axis=-1, keepdims=True) * (1.0 / width)
    return xf * lax.rsqrt(ms + NORM_EPS) * gain_row


def _seg64_norm_rope(zt, gain_row, cosb, sinb):
    lane = _lane_iota(zt.shape)
    lo = lane < 64
    sq = zt * zt
    s0 = jnp.sum(jnp.where(lo, sq, 0.0), axis=-1, keepdims=True)
    s1 = jnp.sum(jnp.where(lo, 0.0, sq), axis=-1, keepdims=True)
    ms = jnp.where(lo, s0, s1) * (1.0 / 64)
    y = zt * lax.rsqrt(ms + NORM_EPS) * gain_row
    first = (lane % 64) < 32
    rot = jnp.where(first, pltpu.roll(y, 96, 1), pltpu.roll(y, 32, 1))
    return y * cosb + rot * sinb


def _in_proj_kernel(x_ref, mixg_ref, w_ref, cos_ref, sin_ref, hg_ref,
                    dq_ref, dk_ref, dv_ref, wq_ref, wkd_ref, wvt_ref, mq_ref, gate_ref):
    xn = _rms(x_ref[...], mixg_ref[...], D_MODEL).astype(BF16)
    cosb = cos_ref[...]
    sinb = sin_ref[...]

    def chunk(c):
        return jnp.dot(xn, w_ref[:, c * IN_CHUNK:(c + 1) * IN_CHUNK], preferred_element_type=F32)

    z = chunk(0)
    for t in range(4):
        sl = slice(t * LANES, (t + 1) * LANES)
        dq_ref[:, sl] = _seg64_norm_rope(z[:, sl], hg_ref[0:1, :], cosb, sinb).astype(BF16)
        zs = z[:, 512 + t * LANES:512 + (t + 1) * LANES]
        dk_ref[:, sl] = _seg64_norm_rope(zs, hg_ref[1:2, :], cosb, sinb).astype(BF16)

    z = chunk(1)
    for t in range(4):
        sl = slice(t * LANES, (t + 1) * LANES)
        wq_ref[:, sl] = _seg64_norm_rope(z[:, sl], hg_ref[2:3, :], cosb, sinb).astype(BF16)
    k = _seg64_norm_rope(z[:, 512:640], hg_ref[3:4, :], cosb, sinb)
    lo = _lane_iota(k.shape) < 64
    sw = pltpu.roll(k, 64, 1)
    wkd_ref[:, 0:LANES] = jnp.where(lo, k, sw).astype(BF16)
    wkd_ref[:, LANES:2 * LANES] = jnp.where(lo, sw, k).astype(BF16)
    vt = z[:, 640:768].T
    ones = jnp.ones((WV_ROWS - WA_DH, z.shape[0]), BF16)
    for g in range(WA_KV_HEADS):
        wvt_ref[g * WV_ROWS:g * WV_ROWS + WA_DH, :] = vt[g * WA_DH:(g + 1) * WA_DH].astype(BF16)
        wvt_ref[g * WV_ROWS + WA_DH:(g + 1) * WV_ROWS, :] = ones

    z = chunk(2)
    for t in range(4):
        sl = slice(t * LANES, (t + 1) * LANES)
        mq_ref[:, sl] = _rms(z[:, sl], hg_ref[4:5, :], MA_DH).astype(BF16)
    ones = jnp.ones((DV_ROWS - LANES, z.shape[0]), BF16)
    for t in range(DA_HEADS):
        vt = z[:, 512 + t * LANES:512 + (t + 1) * LANES].T
        dv_ref[t * DV_ROWS:t * DV_ROWS + LANES, :] = vt.astype(BF16)
        dv_ref[t * DV_ROWS + LANES:(t + 1) * DV_ROWS, :] = ones

    for c in range(3, N_IN_CHUNKS):
        gate_ref[:, (c - 3) * IN_CHUNK:(c - 2) * IN_CHUNK] = jax.nn.sigmoid(chunk(c)).astype(BF16)


def _in_proj(x2d, seq, mixg, w_r, cos_t, sin_t, head_gains):
    n = x2d.shape[0]
    tm = _tile(seq, 512)
    nsb = seq // tm
    bf = lambda w: jax.ShapeDtypeStruct((n, w), BF16)
    row = lambda w: pl.BlockSpec((tm, w), lambda i: (i, 0))
    return pl.pallas_call(
        _in_proj_kernel,
        grid=(n // tm,),
        in_specs=[
            pl.BlockSpec((tm, D_MODEL), lambda i: (i, 0)),
            pl.BlockSpec((1, D_MODEL), lambda i: (0, 0)),
            pl.BlockSpec((D_MODEL, N_IN_CHUNKS * IN_CHUNK), lambda i: (0, 0), pipeline_mode=pl.Buffered(1)),
            pl.BlockSpec((tm, LANES), lambda i: (i % nsb, 0)),
            pl.BlockSpec((tm, LANES), lambda i: (i % nsb, 0)),
            pl.BlockSpec((8, LANES), lambda i: (0, 0)),
        ],
        out_specs=[row(512), row(512),
                   pl.BlockSpec((DA_HEADS * DV_ROWS, tm), lambda i: (i // nsb, i % nsb)),
                   row(512), row(256),
                   pl.BlockSpec((WA_KV_HEADS * WV_ROWS, tm), lambda i: (i // nsb, i % nsb)),
                   row(512), row(N_BRANCH * D_MODEL)],
        out_shape=[bf(512), bf(512), jax.ShapeDtypeStruct((n // seq * DA_HEADS * DV_ROWS, seq), BF16),
                   bf(512), bf(256), jax.ShapeDtypeStruct((n // seq * WA_KV_HEADS * WV_ROWS, seq), BF16), bf(512),
                   bf(N_BRANCH * D_MODEL)],
        compiler_params=_params(("parallel",)),
        name="in_proj",
    )(x2d, mixg, w_r, cos_t, sin_t, head_gains)


def _diff_attn_kernel(q_ref, qn_ref, k_ref, v_ref, lam_ref, og_ref, o_ref, m_sc, acc_sc, s_sc, cmax_sc, *, tk,
                      lam_init):
    def split(q):
        lo = _lane_iota(q.shape) < 64
        zero = jnp.zeros_like(q)
        return jnp.where(lo, q, zero), jnp.where(lo, zero, q)

    qs = split(q_ref[...])
    m_sc[...] = jnp.full(m_sc.shape, -jnp.inf, F32)
    acc_sc[...] = jnp.zeros(acc_sc.shape, F32)
    nt = (((1,), (1,)), ((), ()))
    n_chunks = k_ref.shape[0] // tk

    def scores(j, slot, qs=qs):
        off = pl.multiple_of(j * tk, tk)
        k = k_ref[pl.ds(off, tk), :]
        for c in range(2):
            s = lax.dot_general(k, qs[c], nt, preferred_element_type=F32)
            s_sc[slot, c] = s
            cmax_sc[slot, c] = jnp.max(s, axis=0, keepdims=True)

    def process(j, slot):
        off = pl.multiple_of(j * tk, tk)
        vt = v_ref[:, pl.ds(off, tk)]
        for c in range(2):
            s = s_sc[slot, c]
            m_old = m_sc[c]
            m_new = jnp.maximum(m_old, cmax_sc[slot, c])
            p = jnp.exp2(s - m_new)
            alpha = jnp.exp2(m_old - m_new)
            acc_sc[c] = alpha * acc_sc[c] + jnp.dot(vt, p.astype(BF16), preferred_element_type=F32)
            m_sc[c] = m_new

    if n_chunks == 1:
        scores(0, 0)
        process(0, 0)
    else:
        @pl.when(pl.program_id(2) == 0)
        def _():
            scores(0, 0)

        def pair(jj, carry):
            scores(2 * jj + 1, 1)
            process(2 * jj, 0)
            scores(2 * jj + 2, 0)
            process(2 * jj + 1, 1)
            return carry

        lax.fori_loop(0, n_chunks // 2 - 1, pair, 0)
        scores(n_chunks - 1, 1)
        process(n_chunks - 2, 0)
        scores(0, 0, split(qn_ref[...]))
        process(n_chunks - 1, 1)

    lp = lam_ref[...]
    lam = (jnp.exp(jnp.sum(lp[0:1] * lp[1:2], axis=-1, keepdims=True))
           - jnp.exp(jnp.sum(lp[2:3] * lp[3:4], axis=-1, keepdims=True)) + lam_init)
    ot = (acc_sc[0, :LANES, :] / acc_sc[0, LANES:LANES + 1, :]
          - lam * (acc_sc[1, :LANES, :] / acc_sc[1, LANES:LANES + 1, :]))
    o_ref[...] = (_rms(ot.T, og_ref[...], 2 * DA_DH) * (1.0 - lam_init)).astype(BF16)


def _diff_attn(dq, dk, dv, da_lambda, og, nb, seq, lam_init):
    n = dq.shape[0]
    tq = _tile(seq, DIFF_TQ)
    tk = _tile(seq, DIFF_TK)
    nq = seq // tq
    assert seq // tk == 1 or (seq // tk) % 2 == 0, (seq, tk)
    kern = functools.partial(_diff_attn_kernel, tk=tk, lam_init=lam_init)
    return pl.pallas_call(
        kern,
        grid=(nb, DA_HEADS, nq),
        in_specs=[
            pl.BlockSpec((tq, LANES), lambda b, h, i: (b * nq + i, h)),
            pl.BlockSpec((tq, LANES), lambda b, h, i: (b * nq + jnp.minimum(i + 1, nq - 1), h)),
            pl.BlockSpec((seq, LANES), lambda b, h, i: (b, h)),
            pl.BlockSpec((DV_ROWS, seq), lambda b, h, i: (b * DA_HEADS + h, 0)),
            pl.BlockSpec((4, DA_DH), lambda b, h, i: (0, 0)),
            pl.BlockSpec((1, LANES), lambda b, h, i: (0, 0)),
        ],
        out_specs=pl.BlockSpec((tq, LANES), lambda b, h, i: (b * nq + i, h)),
        out_shape=jax.ShapeDtypeStruct((n, BRANCH_WIDTH), BF16),
        scratch_shapes=[pltpu.VMEM((2, 1, tq), F32), pltpu.VMEM((2, DV_ROWS, tq), F32),
                        pltpu.VMEM((2, 2, tk, tq), F32), pltpu.VMEM((2, 2, 1, tq), F32)],
        compiler_params=_params(("parallel", "parallel", "arbitrary")),
        name="diff_attn",
    )(dq, dq, dk, dv, da_lambda, og)


def _window_attn_kernel(q_ref, kd_ref, vt_ref, sink_ref, o_ref, *, tq, seq):
    i = pl.program_id(1)
    nsub = tq // WINDOW
    kw_len = 3 * WINDOW
    nt = (((1,), (1,)), ((), ()))
    rep = WA_HEADS // WA_KV_HEADS
    lo = _lane_iota((WINDOW, LANES)) < 64
    log2e = math.log2(math.e)
    for n in range(nsub):
        gb = i * nsub + n
        start = pl.multiple_of(jnp.clip((gb - 1) * WINDOW, 0, seq - kw_len), WINDOW)
        kpos = start + lax.broadcasted_iota(jnp.int32, (kw_len, WINDOW), 0)
        qpos = gb * WINDOW + lax.broadcasted_iota(jnp.int32, (kw_len, WINDOW), 1)
        valid1 = jnp.abs(qpos - kpos) <= WINDOW
        valid = jnp.concatenate([valid1] * rep, axis=1)
        for g in range(WA_KV_HEADS):
            kg = kd_ref[pl.ds(start, kw_len), g * LANES:(g + 1) * LANES]
            vt = vt_ref[g * WV_ROWS:(g + 1) * WV_ROWS, pl.ds(start, kw_len)]
            tiles, sinks = [], []
            for e in range(rep):
                h = g * rep + e
                qp = q_ref[n * WINDOW:(n + 1) * WINDOW, (h // 2) * LANES:(h // 2 + 1) * LANES]
                zero = jnp.zeros_like(qp)
                tiles.append(jnp.where(lo, qp, zero) if h % 2 == 0 else jnp.where(lo, zero, qp))
                sinks.append(jnp.broadcast_to(sink_ref[h:h + 1, 0:1] * log2e, (1, WINDOW)))
            qs = jnp.concatenate(tiles, axis=0)
            sk = jnp.concatenate(sinks, axis=1)
            s = lax.dot_general(kg, qs, nt, preferred_element_type=F32)
            s = jnp.where(valid, s, NEG_INF)
            m = jnp.maximum(jnp.max(s, axis=0, keepdims=True), sk)
            r = jnp.dot(vt, jnp.exp2(s - m).astype(BF16), preferred_element_type=F32)
            ot = r[:WA_DH] / (r[WA_DH:WA_DH + 1] + jnp.exp2(sk - m))
            for pr in range(rep // 2):
                col = (g * rep // 2 + pr) * LANES
                pair = jnp.concatenate([ot[:, 2 * pr * WINDOW:(2 * pr + 1) * WINDOW],
                                        ot[:, (2 * pr + 1) * WINDOW:(2 * pr + 2) * WINDOW]], axis=0)
                o_ref[n * WINDOW:(n + 1) * WINDOW, col:col + LANES] = pair.T.astype(BF16)


def _window_attn(wq, wkd, wvd, sink_b, nb, seq):
    n = wq.shape[0]
    tq = _tile(seq, 512)
    nq = seq // tq
    kern = functools.partial(_window_attn_kernel, tq=tq, seq=seq)
    return pl.pallas_call(
        kern,
        grid=(nb, nq),
        in_specs=[
            pl.BlockSpec((tq, 512), lambda b, i: (b * nq + i, 0)),
            pl.BlockSpec((seq, 256), lambda b, i: (b, 0)),
            pl.BlockSpec((WA_KV_HEADS * WV_ROWS, seq), lambda b, i: (b, 0)),
            pl.BlockSpec((8, LANES), lambda b, i: (0, 0)),
        ],
        out_specs=pl.BlockSpec((tq, 512), lambda b, i: (b * nq + i, 0)),
        out_shape=jax.ShapeDtypeStruct((n, BRANCH_WIDTH), BF16),
        compiler_params=_params(("parallel", "arbitrary")),
        name="window_attn",
    )(wq, wkd, wvd, sink_b)


def _mem_kv_kernel(mem_ref, mg_ref, w_ref, kg_ref, mk_ref, mv_ref):
    h = _rms(mem_ref[...], mg_ref[...], D_MODEL).astype(BF16)
    kv = jnp.dot(h, w_ref[...], preferred_element_type=F32)
    for t in range(MA_HEADS):
        sl = slice(t * LANES, (t + 1) * LANES)
        mk_ref[:, sl] = _rms(kv[:, sl], kg_ref[...], MA_DH).astype(BF16)
    mv_ref[...] = kv[:, 512:1024].astype(BF16)


def _mem_kv(mem2d, mlen, mg, w_kv, kg):
    n = mem2d.shape[0]
    return pl.pallas_call(
        _mem_kv_kernel,
        grid=(n // mlen,),
        in_specs=[
            pl.BlockSpec((mlen, D_MODEL), lambda b: (b, 0)),
            pl.BlockSpec((1, D_MODEL), lambda b: (0, 0)),
            pl.BlockSpec((D_MODEL, 2 * MA_HEADS * MA_DH), lambda b: (0, 0)),
            pl.BlockSpec((1, LANES), lambda b: (0, 0)),
        ],
        out_specs=[pl.BlockSpec((mlen, 512), lambda b: (b, 0)), pl.BlockSpec((mlen, 512), lambda b: (b, 0))],
        out_shape=[jax.ShapeDtypeStruct((n, 512), BF16), jax.ShapeDtypeStruct((n, 512), BF16)],
        compiler_params=_params(("parallel",)),
        name="mem_kv",
    )(mem2d, mg, w_kv, kg)


def _mem_attn_kernel(q_ref, mk_ref, mv_ref, o_ref):
    nt = (((1,), (1,)), ((), ()))
    for t in range(MA_HEADS):
        sl = slice(t * LANES, (t + 1) * LANES)
        s = lax.dot_general(q_ref[:, sl], mk_ref[:, sl], nt, preferred_element_type=F32)
        m = jnp.max(s, axis=-1, keepdims=True)
        ex = jnp.exp(s - m)
        p = ex / jnp.sum(ex, axis=-1, keepdims=True)
        o_ref[:, sl] = jnp.dot(p.astype(BF16), mv_ref[:, sl], preferred_element_type=F32).astype(BF16)


def _mem_attn(mq, mk, mv, nb, seq, mlen):
    n = mq.shape[0]
    tq = _tile(seq, 1024)
    nq = seq // tq
    return pl.pallas_call(
        _mem_attn_kernel,
        grid=(nb, nq),
        in_specs=[
            pl.BlockSpec((tq, 512), lambda b, i: (b * nq + i, 0)),
            pl.BlockSpec((mlen, 512), lambda b, i: (b, 0)),
            pl.BlockSpec((mlen, 512), lambda b, i: (b, 0)),
        ],
        out_specs=pl.BlockSpec((tq, 512), lambda b, i: (b * nq + i, 0)),
        out_shape=jax.ShapeDtypeStruct((n, BRANCH_WIDTH), BF16),
        compiler_params=_params(("parallel", "arbitrary")),
        name="mem_attn",
    )(mq, mk, mv)


def _merge_kernel(x_ref, oda_ref, owa_ref, oma_ref, gate_ref, wb_ref, wo_ref, o_ref):
    merged = None
    for nbr, br in enumerate((oda_ref, owa_ref, oma_ref)):
        proj = jnp.dot(br[...], wb_ref[nbr], preferred_element_type=F32)
        term = gate_ref[:, nbr * D_MODEL:(nbr + 1) * D_MODEL].astype(F32) * proj
        merged = term if merged is None else merged + term
    o_ref[...] = x_ref[...] + jnp.dot(merged.astype(BF16), wo_ref[...], preferred_element_type=F32)


def _merge(x2d, oda, owa, oma, gates, wb, wo):
    n = x2d.shape[0]
    tm = _tile(n, 512)
    row = lambda w: pl.BlockSpec((tm, w), lambda i: (i, 0))
    return pl.pallas_call(
        _merge_kernel,
        grid=(n // tm,),
        in_specs=[row(D_MODEL), row(512), row(512), row(512), row(N_BRANCH * D_MODEL),
                  pl.BlockSpec((N_BRANCH, BRANCH_WIDTH, D_MODEL), lambda i: (0, 0, 0)),
                  pl.BlockSpec((D_MODEL, D_MODEL), lambda i: (0, 0))],
        out_specs=row(D_MODEL),
        out_shape=jax.ShapeDtypeStruct((n, D_MODEL), F32),
        compiler_params=_params(("parallel",)),
        name="merge",
    )(x2d, oda, owa, oma, gates, wb, wo)


_SORT_NETS = {
    2: ((0, 1, True),),
    4: ((0, 1, True), (2, 3, True), (0, 2, True), (1, 3, True), (1, 2, False)),
}


def _top16_lead(rows, ids, group):
    assert len(rows) % group == 0 and group in _SORT_NETS
    gv, gi = [], []
    for j in range(0, len(rows), group):
        vs, js = list(rows[j:j + group]), list(ids[j:j + group])
        for a, b, left_has_lower_ids in _SORT_NETS[group]:
            va, vb, ia, ib = vs[a], vs[b], js[a], js[b]
            take = (va >= vb) if left_has_lower_ids else ((va > vb) | ((va == vb) & (ia < ib)))
            vs[a], vs[b] = jnp.maximum(va, vb), jnp.minimum(va, vb)
            js[a], js[b] = jnp.where(take, ia, ib), jnp.where(take, ib, ia)
        gv.append(vs)
        gi.append(js)
    vals, idxs = [], []
    for _ in range(PEER_TOPK):
        level = [(v[0], i[0]) for v, i in zip(gv, gi)]
        while len(level) > 1:
            nxt = []
            for j in range(0, len(level) - 1, 2):
                (va, ia), (vb, ib) = level[j], level[j + 1]
                take = va >= vb
                nxt.append((jnp.maximum(va, vb), jnp.where(take, ia, ib)))
            if len(level) % 2:
                nxt.append(level[-1])
            level = nxt
        m, sel = level[0]
        for vs, js in zip(gv, gi):
            hit = sel == js[0]
            for k in range(group - 1):
                vs[k] = jnp.where(hit, vs[k + 1], vs[k])
                js[k] = jnp.where(hit, js[k + 1], js[k])
            vs[group - 1] = jnp.where(hit, -jnp.inf, vs[group - 1])
        vals.append(m)
        idxs.append(sel)
    return vals, idxs


def _peer_q_kernel(x_ref, fg_ref, wq_ref, xn_ref, q_ref):
    xn = _rms(x_ref[...], fg_ref[...], D_MODEL).astype(BF16)
    xn_ref[...] = xn
    q = jnp.dot(xn, wq_ref[...], preferred_element_type=F32).astype(BF16)
    for hp in range(2 * PEER_HEADS):
        q_ref[hp] = q[:, hp * LANES:(hp + 1) * LANES]


def _peer_q(x2d, fg, wq):
    n = x2d.shape[0]
    tm = _tile(n, 512)
    return pl.pallas_call(
        _peer_q_kernel,
        grid=(n // tm,),
        in_specs=[
            pl.BlockSpec((tm, D_MODEL), lambda i: (i, 0)),
            pl.BlockSpec((1, D_MODEL), lambda i: (0, 0)),
            pl.BlockSpec((D_MODEL, 2 * PEER_HEADS * PEER_HALF), lambda i: (0, 0)),
        ],
        out_specs=[pl.BlockSpec((tm, D_MODEL), lambda i: (i, 0)),
                   pl.BlockSpec((2 * PEER_HEADS, tm, PEER_HALF), lambda i: (0, i, 0))],
        out_shape=[jax.ShapeDtypeStruct((n, D_MODEL), BF16),
                   jax.ShapeDtypeStruct((2 * PEER_HEADS, n, PEER_HALF), BF16)],
        compiler_params=_params(("parallel",)),
        name="peer_q",
    )(x2d, fg, wq)


_CAND_PAIRS = tuple((a, b) for a in range(PEER_TOPK) for b in range(PEER_TOPK) if (a + 1) * (b + 1) <= PEER_TOPK)


def _peer_topk_kernel(q_ref, keys_ref, g_ref, ei_ref, ej_ref, s0_sc, s1_sc, rg_sc, ri_sc, rj_sc, *, tm):
    nt = (((1,), (1,)), ((), ()))
    gt = tm // LANES
    key_ids = [float(k) for k in range(N_KEYS)]
    cand_ids = [float(a * PEER_TOPK + b) for a, b in _CAND_PAIRS]

    def head_body(h, carry):
        sv, si = [], []
        for p, s_sc in enumerate((s0_sc, s1_sc)):
            s = lax.dot_general(keys_ref[2 * h + p], q_ref[2 * h + p], nt,
                                preferred_element_type=F32)
            for tt in range(gt):
                s_sc[pl.ds(tt, N_KEYS, stride=gt), :] = s[:, tt * LANES:(tt + 1) * LANES]
            v, ix = _top16_lead([s_sc[k * gt:(k + 1) * gt, :] for k in range(N_KEYS)], key_ids, 4)
            sv.append(v)
            si.append(ix)
        fv, fi = _top16_lead([sv[0][a] + sv[1][b] for a, b in _CAND_PAIRS], cand_ids, 2)
        ex = [jnp.exp(v - fv[0]) for v in fv]
        den = ex[0]
        for e in ex[1:]:
            den = den + e
        for k in range(PEER_TOPK):
            fik = fi[k].astype(jnp.int32)
            fa = fik >> 4
            fb = fik & (PEER_TOPK - 1)
            ei = jnp.zeros((gt, LANES), F32)
            ej = jnp.zeros((gt, LANES), F32)
            for a in range(PEER_TOPK):
                ei = jnp.where(fa == a, si[0][a], ei)
                ej = jnp.where(fb == a, si[1][a], ej)
            off = pl.multiple_of((h * PEER_TOPK + k) * gt, gt)
            rg_sc[pl.ds(off, gt), :] = ex[k] / den
            ri_sc[pl.ds(off, gt), :] = ei
            rj_sc[pl.ds(off, gt), :] = ej
        return carry

    lax.fori_loop(0, PEER_HEADS, head_body, 0)

    for tt in range(gt):
        sl = slice(tt * LANES, (tt + 1) * LANES)
        g_ref[sl, :] = rg_sc[pl.ds(tt, N_KEYS, stride=gt), :].T
        ei_ref[sl, :] = ri_sc[pl.ds(tt, N_KEYS, stride=gt), :].T
        ej_ref[sl, :] = rj_sc[pl.ds(tt, N_KEYS, stride=gt), :].T


def _peer_topk(q3, keys):
    n = q3.shape[1]
    tm = _tile(n, 8 * LANES)
    slots = PEER_HEADS * PEER_TOPK
    out = jax.ShapeDtypeStruct((n, slots), F32)
    return pl.pallas_call(
        functools.partial(_peer_topk_kernel, tm=tm),
        grid=(n // tm,),
        in_specs=[
            pl.BlockSpec((2 * PEER_HEADS, tm, PEER_HALF), lambda i: (0, i, 0)),
            pl.BlockSpec((2 * PEER_HEADS, N_KEYS, PEER_HALF), lambda i: (0, 0, 0)),
        ],
        out_specs=[pl.BlockSpec((tm, slots), lambda i: (i, 0))] * 3,
        out_shape=[out, out, out],
        scratch_shapes=[pltpu.VMEM((N_KEYS * tm // LANES, LANES), F32)] * 2
        + [pltpu.VMEM((slots * tm // LANES, LANES), F32)] * 3,
        compiler_params=_params(("parallel",)),
        name="peer_topk",
    )(q3, keys)


def _peer_wbuild_kernel(g_ref, ei_ref, ej_ref, wt_ref, w_sc, *, tm, pitch):
    sub = lax.broadcasted_iota(jnp.int32, (N_KEYS, LANES), 0).astype(F32)

    def tok_body(t, carry):
        g_r = g_ref[pl.ds(t, 1), :]
        at = jnp.where(sub == ei_ref[pl.ds(t, 1), :], g_r, 0.0).astype(BF16)
        bt = jnp.where(sub == ej_ref[pl.ds(t, 1), :], 1.0, 0.0).T.astype(BF16)
        w_t = jnp.dot(at, bt, preferred_element_type=F32)
        w_sc[pl.ds(t, N_KEYS, stride=pitch), :] = w_t
        return carry

    lax.fori_loop(0, tm, tok_body, 0, unroll=True)

    for i in range(N_KEYS):
        wt_ref[0, i] = w_sc[i * pitch:i * pitch + tm, :].astype(BF16)


def _peer_wbuild(g, ei, ej):
    n = g.shape[0]
    tm = _tile(n, 128)
    pitch = tm + 4
    row = pl.BlockSpec((tm, PEER_HEADS * PEER_TOPK), lambda i: (i, 0))
    return pl.pallas_call(
        functools.partial(_peer_wbuild_kernel, tm=tm, pitch=pitch),
        grid=(n // tm,),
        in_specs=[row, row, row],
        out_specs=pl.BlockSpec((1, N_KEYS, tm, N_KEYS), lambda i: (i, 0, 0, 0)),
        out_shape=jax.ShapeDtypeStruct((n // tm, N_KEYS, tm, N_KEYS), BF16),
        scratch_shapes=[pltpu.VMEM((N_KEYS * pitch, N_KEYS), F32)],
        compiler_params=_params(("parallel",)),
        name="peer_wbuild",
    )(g, ei, ej)


def _peer_dense_kernel(xn_ref, ut_ref, wt_ref, v_ref, x_ref, o_ref, acc_sc, *, ce):
    c = pl.program_id(1)

    @pl.when(c == 0)
    def _():
        acc_sc[...] = jnp.zeros(acc_sc.shape, F32)

    a = lax.dot_general(xn_ref[...], ut_ref[...], (((1,), (1,)), ((), ())), preferred_element_type=F32)
    act = 0.5 * a * (1.0 + lax.erf(a * math.sqrt(0.5)))
    cols = []
    for s in range(ce // LANES):
        w = jnp.concatenate([wt_ref[b, s] for b in range(wt_ref.shape[0])], axis=0)
        cols.append((w.astype(F32) * act[:, s * LANES:(s + 1) * LANES]).astype(BF16))
    h = jnp.concatenate(cols, axis=1)
    acc_sc[...] += jnp.dot(h, v_ref[...], preferred_element_type=F32)

    @pl.when(c == pl.num_programs(1) - 1)
    def _():
        o_ref[...] = x_ref[...] + acc_sc[...]


def _peer_dense(xn, ut, wt, v, x2d):
    n = x2d.shape[0]
    tw = wt.shape[2]
    tm = _tile(n, 512)
    assert tm % tw == 0, (tm, tw)
    ce = DENSE_CE
    kern = functools.partial(_peer_dense_kernel, ce=ce)
    return pl.pallas_call(
        kern,
        grid=(n // tm, N_EXPERTS // ce),
        in_specs=[
            pl.BlockSpec((tm, D_MODEL), lambda i, c: (i, 0)),
            pl.BlockSpec((ce, D_MODEL), lambda i, c: (c, 0)),
            pl.BlockSpec((tm // tw, ce // N_KEYS, tw, N_KEYS), lambda i, c: (i, c, 0, 0)),
            pl.BlockSpec((ce, D_MODEL), lambda i, c: (c, 0)),
            pl.BlockSpec((tm, D_MODEL), lambda i, c: (i, 0)),
        ],
        out_specs=pl.BlockSpec((tm, D_MODEL), lambda i, c: (i, 0)),
        out_shape=jax.ShapeDtypeStruct((n, D_MODEL), F32),
        scratch_shapes=[pltpu.VMEM((tm, D_MODEL), F32)],
        compiler_params=_params(("parallel", "arbitrary")),
        name="peer_dense",
    )(xn, ut, wt, v, x2d)


def _rope_tiles(seq):
    inv = 1.0 / (ROPE_THETA ** (jnp.arange(0, DA_DH, 2, dtype=F32) / DA_DH))
    ang = jnp.arange(seq, dtype=F32)[:, None] * inv[None, :]
    cos, sin = jnp.cos(ang), jnp.sin(ang)
    cos_t = jnp.tile(cos, (1, 4))
    sign = jnp.tile(jnp.concatenate([-jnp.ones((32,), F32), jnp.ones((32,), F32)]), 2)
    sin_t = jnp.tile(sin, (1, 4)) * sign[None, :]
    return cos_t, sin_t


def _prep_layer(l, mix_norm, w_in, da_q_norm, da_k_norm, da_lambda, da_out_norm, wa_q_norm, wa_k_norm,
                wa_sink, mem_norm, w_mem_kv, ma_q_norm, ma_k_norm, w_branch, w_out, ffn_norm,
                peer_wq, peer_keys, peer_u, peer_v):
    w = w_in[l]
    pad = jnp.zeros((D_MODEL, 256), F32)
    w_r = jnp.concatenate([
        w[:, DA_Q_OFF:DA_V_OFF],
        w[:, WA_Q_OFF:MA_Q_OFF], pad,
        w[:, MA_Q_OFF:GATE_OFF], w[:, DA_V_OFF:WA_Q_OFF],
        w[:, GATE_OFF:]], axis=1).astype(BF16)
    t2 = lambda g: jnp.tile(g.astype(F32), 2)
    head_gains = jnp.stack([
        t2(da_q_norm[l]) * (DA_DH ** -0.5 * math.log2(math.e)), t2(da_k_norm[l]),
        t2(wa_q_norm[l]) * (WA_DH ** -0.5 * math.log2(math.e)), t2(wa_k_norm[l]),
        ma_q_norm[l].astype(F32) * MA_DH ** -0.5,
        jnp.zeros((LANES,), F32), jnp.zeros((LANES,), F32), jnp.zeros((LANES,), F32)])
    return dict(
        mixg=mix_norm[l][None, :], w_r=w_r, head_gains=head_gains,
        da_lambda=da_lambda[l], og=da_out_norm[l][None, :],
        sink=jnp.broadcast_to(wa_sink[l].astype(F32)[:, None], (WA_HEADS, LANES)),
        mem_g=mem_norm[l][None, :], w_kv=w_mem_kv[l].astype(BF16), ma_kg=ma_k_norm[l][None, :],
        wb=w_branch[l].astype(BF16), wo=w_out[l].astype(BF16),
        ffn_g=ffn_norm[l][None, :], peer_wq=peer_wq[l].astype(BF16),
        keys=peer_keys[l].reshape(2 * PEER_HEADS, N_KEYS, PEER_HALF).astype(BF16),
        ut=peer_u[l].astype(BF16), v=peer_v[l].astype(BF16),
    )


def _trunk(x, mem, layers):
    nb, seq, _ = x.shape
    mlen = mem.shape[1]
    x2d = x.reshape(nb * seq, D_MODEL)
    mem2d = mem.reshape(nb * mlen, D_MODEL)
    cos_t, sin_t = _rope_tiles(seq)
    for l, p in enumerate(layers):
        lam_init = 0.8 - 0.6 * math.exp(-0.3 * l)
        dq, dk, dv, wq, wkd, wvd, mq, gates = _in_proj(x2d, seq, p["mixg"], p["w_r"], cos_t, sin_t,
                                                      p["head_gains"])
        oda = _diff_attn(dq, dk, dv, p["da_lambda"], p["og"], nb, seq, lam_init)
        owa = _window_attn(wq, wkd, wvd, p["sink"], nb, seq)
        mk, mv = _mem_kv(mem2d, mlen, p["mem_g"], p["w_kv"], p["ma_kg"])
        oma = _mem_attn(mq, mk, mv, nb, seq, mlen)
        x1 = _merge(x2d, oda, owa, oma, gates, p["wb"], p["wo"])
        xn, q3 = _peer_q(x1, p["ffn_g"], p["peer_wq"])
        wt = _peer_wbuild(*_peer_topk(q3, p["keys"]))
        x2d = _peer_dense(xn, p["ut"], wt, p["v"], x1)
    return x2d.reshape(nb, seq, D_MODEL)


def kernel(x_prompt, x_sample, mem_prompt, mem_sample, mix_norm, w_in, da_q_norm, da_k_norm, da_lambda, da_out_norm, wa_q_norm, wa_k_norm, wa_sink, mem_norm, w_mem_kv, ma_q_norm, ma_k_norm, w_branch, w_out, ffn_norm, peer_wq, peer_keys, peer_u, peer_v):
    weights = (mix_norm, w_in, da_q_norm, da_k_norm, da_lambda, da_out_norm, wa_q_norm, wa_k_norm,
               wa_sink, mem_norm, w_mem_kv, ma_q_norm, ma_k_norm, w_branch, w_out, ffn_norm,
               peer_wq, peer_keys, peer_u, peer_v)
    layers = [_prep_layer(l, *weights) for l in range(w_in.shape[0])]
    y_prompt = _trunk(x_prompt, mem_prompt, layers)
    y_sample = _trunk(x_sample, mem_sample, layers)
    return (y_prompt, y_sample)
```
